```python
import jax, jax.numpy as jnp
from jax import lax
import numpy as np

D_MODEL = 1024
BATCH = 32
SEQ = 2048
DEPTH = 4
DEC_BATCH = 8
DEC_SEQ = 16
PAST_LEN = 1024

CHUNK = 64
D_PLE = 256
D_FF = 2816
N_EVEN = (DEPTH + 1) // 2
N_ODD = DEPTH // 2
H_A = 4
DK_A = 128
DV_A = 128
CONV_A = 4
QKV_A = H_A * (2 * DK_A + DV_A)
H_B = 4
DK_B = 128
DV_B = 128
CONV_C = 3
MIX_EVEN = H_A * DV_A + H_B * DV_B
EPS = 1e-6

kernel_name = "hybrid_gdn_mlstm_shortconv_stream_step"


def _even_split_sizes():
    return [QKV_A, H_A * DV_A, H_A, H_A,
            H_B * DK_B, H_B * DK_B, H_B * DV_B, H_B * DV_B, H_B, H_B]


def rmsnorm(x, g):
    xf = x.astype(jnp.float32)
    y = xf * lax.rsqrt(jnp.mean(xf * xf, axis=-1, keepdims=True) + EPS)
    return (y * g.astype(jnp.float32)).astype(x.dtype)


def l2norm(x):
    return x * lax.rsqrt(jnp.sum(x * x, axis=-1, keepdims=True) + EPS)


def swiglu(x, w_gu, w_d):
    a, b = jnp.split(x @ w_gu, 2, axis=-1)
    return (jax.nn.silu(a) * b) @ w_d


def causal_dwconv(hist, w):
    K = w.shape[0]
    T = hist.shape[1] - K + 1
    out = hist[:, 0:T] * w[0]
    for j in range(1, K):
        out = out + hist[:, j:j + T] * w[j]
    return out


def _to_chunks(a, cs):
    B, T, H = a.shape[:3]
    a = a.reshape((B, T // cs, cs, H) + a.shape[3:])
    perm = (1, 0, 3, 2) + tuple(range(4, a.ndim))
    return a.transpose(perm)


def _from_chunks(a):
    N, B, H, cs, d = a.shape
    return a.transpose(1, 0, 3, 2, 4).reshape(B, N * cs, H, d)


def gated_delta_chunked(q, k, v, beta, g, S0):
    T = q.shape[1]
    cs = min(CHUNK, T)
    dv = v.shape[-1]
    q, k, v = _to_chunks(q, cs), _to_chunks(k, cs), _to_chunks(v, cs)
    beta, g = _to_chunks(beta, cs), _to_chunks(g, cs)
    G = jnp.cumsum(g, axis=-1)
    idx = jnp.arange(cs)
    incl = idx[:, None] >= idx[None, :]
    strict = idx[:, None] > idx[None, :]
    diff = G[..., :, None] - G[..., None, :]
    gam = jnp.where(incl, jnp.exp(jnp.where(incl, diff, 0.0)), 0.0)
    kk = jnp.einsum('nbhtd,nbhsd->nbhts', k, k)
    A = jnp.eye(cs, dtype=jnp.float32) + jnp.where(strict, beta[..., :, None] * kk * gam, 0.0)
    eG = jnp.exp(G)
    rhs = jnp.concatenate([beta[..., None] * v, (beta * eG)[..., None] * k], axis=-1)
    sol = lax.linalg.triangular_solve(A, rhs, left_side=True, lower=True, unit_diagonal=True)
    u_v, w = sol[..., :dv], sol[..., dv:]
    p_qk = jnp.einsum('nbhtd,nbhsd->nbhts', q, k) * gam
    k_end = k * jnp.exp(G[..., -1:] - G)[..., None]
    gL = G[..., -1]

    def step(S, xs):
        q_c, u_v_c, w_c, p_c, eG_c, k_end_c, gL_c = xs
        u = u_v_c - jnp.einsum('bhtk,bhkv->bhtv', w_c, S)
        o = eG_c[..., None] * jnp.einsum('bhtk,bhkv->bhtv', q_c, S) + jnp.einsum('bhts,bhsv->bhtv', p_c, u)
        S = jnp.exp(gL_c)[..., None, None] * S + jnp.einsum('bhsk,bhsv->bhkv', k_end_c, u)
        return S, o

    S, o = lax.scan(step, S0, (q, u_v, w, p_qk, eG, k_end, gL))
    return _from_chunks(o), S


def mlstm_chunked(q, k, v, logi, logf, C0, n0, m0):
    T = q.shape[1]
    cs = min(CHUNK, T)
    q, k, v = _to_chunks(q, cs), _to_chunks(k, cs), _to_chunks(v, cs)
    logi, logf = _to_chunks(logi, cs), _to_chunks(logf, cs)
    b = jnp.cumsum(logf, axis=-1)
    idx = jnp.arange(cs)
    incl = idx[:, None] >= idx[None, :]
    dmat = jnp.where(incl, b[..., :, None] - b[..., None, :] + logi[..., None, :], -jnp.inf)
    dmax = jnp.max(dmat, axis=-1)
    s_qk = jnp.einsum('nbhtd,nbhsd->nbhts', q, k)

    def step(carry, xs):
        C, n, m = carry
        q_c, k_c, v_c, b_c, d_c, dmax_c, s_c = xs
        m_t = jnp.maximum(b_c + m[..., None], dmax_c)
        w_inter = jnp.exp(b_c + m[..., None] - m_t)
        w_intra = jnp.exp(d_c - m_t[..., None])
        a = s_c * w_intra
        num = w_inter[..., None] * jnp.einsum('bhtk,bhkv->bhtv', q_c, C) + jnp.einsum('bhts,bhsv->bhtv', a, v_c)
        den = w_inter * jnp.einsum('bhtk,bhk->bht', q_c, n) + jnp.sum(a, axis=-1)
        h = num / jnp.maximum(jnp.abs(den), jnp.exp(-m_t))[..., None]
        decay = w_inter[..., -1]
        kw = k_c * w_intra[..., -1, :][..., None]
        C = decay[..., None, None] * C + jnp.einsum('bhsk,bhsv->bhkv', kw, v_c)
        n = decay[..., None] * n + jnp.sum(kw, axis=-2)
        return (C, n, m_t[..., -1]), h

    (C, n, m), h = lax.scan(step, (C0, n0, m0), (q, k, v, b, dmat, dmax, s_qk))
    return _from_chunks(h), C, n, m


def even_mixer(xn, conv_st, S0, C0, n0, m0, w_in, gdn_conv_w, gdn_a_log, gdn_dt_bias, gdn_norm_w,
               mlstm_b_i, mlstm_b_f, mlstm_norm_w, w_out):
    B, T, _ = xn.shape
    f32 = jnp.float32
    splits = np.cumsum(_even_split_sizes())[:-1].tolist()
    qkv_a, z_a, beta_a, a_a, q_b, k_b, v_b, o_b, i_b, f_b = jnp.split(xn @ w_in, splits, axis=-1)
    hist = jnp.concatenate([conv_st.astype(qkv_a.dtype), qkv_a], axis=1)
    new_conv = hist[:, -(CONV_A - 1):]
    qkv_a = jax.nn.silu(causal_dwconv(hist, gdn_conv_w)).astype(f32)
    q_a, k_a, v_a = jnp.split(qkv_a, [H_A * DK_A, 2 * H_A * DK_A], axis=-1)
    q_a = l2norm(q_a.reshape(B, T, H_A, DK_A)) * (DK_A ** -0.5)
    k_a = l2norm(k_a.reshape(B, T, H_A, DK_A))
    v_a = v_a.reshape(B, T, H_A, DV_A)
    beta = jax.nn.sigmoid(beta_a.astype(f32))
    g = -jnp.exp(gdn_a_log.astype(f32)) * jax.nn.softplus(a_a.astype(f32) + gdn_dt_bias.astype(f32))
    o_a, S = gated_delta_chunked(q_a, k_a, v_a, beta, g, S0.astype(f32))
    o_a = o_a * lax.rsqrt(jnp.mean(o_a * o_a, axis=-1, keepdims=True) + EPS) * gdn_norm_w.astype(f32)
    o_a = (o_a * jax.nn.silu(z_a.astype(f32).reshape(B, T, H_A, DV_A))).reshape(B, T, H_A * DV_A)
    q_b = q_b.astype(f32).reshape(B, T, H_B, DK_B) * (DK_B ** -0.5)
    k_b = k_b.astype(f32).reshape(B, T, H_B, DK_B)
    v_b = v_b.astype(f32).reshape(B, T, H_B, DV_B)
    logi = i_b.astype(f32) + mlstm_b_i.astype(f32)
    logf = jax.nn.log_sigmoid(f_b.astype(f32) + mlstm_b_f.astype(f32))
    h_b, C, n, m = mlstm_chunked(q_b, k_b, v_b, logi, logf, C0.astype(f32), n0.astype(f32), m0.astype(f32))
    h_b = h_b * lax.rsqrt(jnp.mean(h_b * h_b, axis=-1, keepdims=True) + EPS)
    h_b = h_b.reshape(B, T, H_B * DV_B) * mlstm_norm_w.astype(f32) * jax.nn.sigmoid(o_b.astype(f32))
    out = jnp.concatenate([o_a, h_b], axis=-1).astype(xn.dtype) @ w_out
    dt = xn.dtype
    return out, (new_conv.astype(dt), S.astype(dt), C.astype(dt), n.astype(dt), m.astype(dt))


def odd_mixer(xn, conv_st, w_in, conv_w, w_out):
    h, bg, cg = jnp.split(xn @ w_in, 3, axis=-1)
    u = cg * h
    hist = jnp.concatenate([conv_st.astype(u.dtype), u], axis=1)
    new_st = hist[:, -(CONV_C - 1):]
    y = bg * causal_dwconv(hist, conv_w)
    return y @ w_out, new_st


def run_trunk(x, p, init_even, init_odd, norm_g, final_norm, ffn_w_gu, ffn_w_d, w_ple, w_ple_gate,
              w_in_even, gdn_conv_w, gdn_a_log, gdn_dt_bias, gdn_norm_w, mlstm_b_i, mlstm_b_f,
              mlstm_norm_w, w_out_even, w_in_odd, conv_c_w, w_out_odd):
    new_even, new_odd = [], []
    for i in range(DEPTH):
        j = i // 2
        x = x + 0.5 * swiglu(rmsnorm(x, norm_g[i, 0]), ffn_w_gu[i, 0], ffn_w_d[i, 0])
        xn = rmsnorm(x, norm_g[i, 1])
        if i % 2 == 0:
            y, st = even_mixer(xn, *init_even[j], w_in_even[j], gdn_conv_w[j], gdn_a_log[j], gdn_dt_bias[j],
                               gdn_norm_w[j], mlstm_b_i[j], mlstm_b_f[j], mlstm_norm_w[j], w_out_even[j])
            new_even.append(st)
        else:
            y, st = odd_mixer(xn, init_odd[j], w_in_odd[j], conv_c_w[j], w_out_odd[j])
            new_odd.append(st)
        x = x + y
        x = x + 0.5 * swiglu(rmsnorm(x, norm_g[i, 2]), ffn_w_gu[i, 1], ffn_w_d[i, 1])
        xg = rmsnorm(x, norm_g[i, 3])
        x = x + (p[i].astype(x.dtype) @ w_ple[i]) * jax.nn.sigmoid(xg @ w_ple_gate[i])
    y = rmsnorm(x, final_norm)
    st_even = [jnp.stack([s[c] for s in new_even]) for c in range(5)]
    st_odd = jnp.stack(new_odd)
    return y, st_even[0], st_even[1], st_even[2], st_even[3], st_even[4], st_odd


def setup_inputs(seed: int = 0) -> dict:
    key = jax.random.key(seed)
    ks = jax.random.split(key, 32)
    nrm = lambda k, s, sc: jax.random.normal(k, s, jnp.float32) * sc
    dt = jnp.exp(jax.random.uniform(ks[20], (N_EVEN, H_A), jnp.float32, np.log(1e-3), np.log(1e-1)))
    return {
        "x_prompt": nrm(ks[0], (BATCH, SEQ, D_MODEL), 1.0),
        "x_sample": nrm(ks[1], (DEC_BATCH, DEC_SEQ, D_MODEL), 1.0),
        "state_gdn_conv": nrm(ks[2], (N_EVEN, DEC_BATCH, CONV_A - 1, QKV_A), 1.0),
        "state_gdn_S": nrm(ks[3], (N_EVEN, DEC_BATCH, H_A, DK_A, DV_A), 0.1),
        "state_mlstm_C": nrm(ks[4], (N_EVEN, DEC_BATCH, H_B, DK_B, DV_B), 0.1),
        "state_mlstm_n": nrm(ks[5], (N_EVEN, DEC_BATCH, H_B, DK_B), 0.1),
        "state_mlstm_m": nrm(ks[6], (N_EVEN, DEC_BATCH, H_B), 1.0),
        "state_conv": nrm(ks[7], (N_ODD, DEC_BATCH, CONV_C - 1, D_MODEL), 1.0),
        "p_prompt": nrm(ks[8], (DEPTH, BATCH, SEQ, D_PLE), 1.0),
        "p_sample": nrm(ks[9], (DEPTH, DEC_BATCH, DEC_SEQ, D_PLE), 1.0),
        "norm_g": 1.0 + nrm(ks[10], (DEPTH, 4, D_MODEL), 0.02),
        "final_norm": 1.0 + nrm(ks[11], (D_MODEL,), 0.02),
        "ffn_w_gu": nrm(ks[12], (DEPTH, 2, D_MODEL, 2 * D_FF), D_MODEL ** -0.5),
        "ffn_w_d": nrm(ks[13], (DEPTH, 2, D_FF, D_MODEL), D_FF ** -0.5),
        "w_ple": nrm(ks[14], (DEPTH, D_PLE, D_MODEL), D_PLE ** -0.5),
        "w_ple_gate": nrm(ks[15], (DEPTH, D_MODEL, D_MODEL), D_MODEL ** -0.5),
        "w_in_even": nrm(ks[16], (N_EVEN, D_MODEL, sum(_even_split_sizes())), D_MODEL ** -0.5),
        "gdn_conv_w": nrm(ks[17], (N_EVEN, CONV_A, QKV_A), CONV_A ** -0.5),
        "gdn_a_log": jnp.log(jax.random.uniform(ks[18], (N_EVEN, H_A), jnp.float32, 1.0, 16.0)),
        "gdn_dt_bias": dt + jnp.log(-jnp.expm1(-dt)),
        "gdn_norm_w": 1.0 + nrm(ks[19], (N_EVEN, DV_A), 0.02),
        "mlstm_b_i": nrm(ks[21], (N_EVEN, H_B), 0.1),
        "mlstm_b_f": jnp.linspace(3.0, 6.0, H_B, dtype=jnp.float32)[None, :] + nrm(ks[22], (N_EVEN, H_B), 0.1),
        "mlstm_norm_w": 1.0 + nrm(ks[23], (N_EVEN, H_B * DV_B), 0.02),
        "w_out_even": nrm(ks[24], (N_EVEN, MIX_EVEN, D_MODEL), MIX_EVEN ** -0.5),
        "w_in_odd": nrm(ks[25], (N_ODD, D_MODEL, 3 * D_MODEL), D_MODEL ** -0.5),
        "conv_c_w": nrm(ks[26], (N_ODD, CONV_C, D_MODEL), CONV_C ** -0.5),
        "w_out_odd": nrm(ks[27], (N_ODD, D_MODEL, D_MODEL), D_MODEL ** -0.5),
    }


def reference(x_prompt, x_sample, state_gdn_conv, state_gdn_S, state_mlstm_C, state_mlstm_n, state_mlstm_m,
              state_conv, p_prompt, p_sample, norm_g, final_norm, ffn_w_gu, ffn_w_d, w_ple, w_ple_gate,
              w_in_even, gdn_conv_w, gdn_a_log, gdn_dt_bias, gdn_norm_w, mlstm_b_i, mlstm_b_f, mlstm_norm_w,
              w_out_even, w_in_odd, conv_c_w, w_out_odd):
    dt = x_prompt.dtype
    Bp = x_prompt.shape[0]
    zero_even = (jnp.zeros((Bp, CONV_A - 1, QKV_A), dt), jnp.zeros((Bp, H_A, DK_A, DV_A), dt),
                 jnp.zeros((Bp, H_B, DK_B, DV_B), dt), jnp.zeros((Bp, H_B, DK_B), dt), jnp.zeros((Bp, H_B), dt))
    init_even_p = [zero_even for _ in range(N_EVEN)]
    init_odd_p = [jnp.zeros((Bp, CONV_C - 1, D_MODEL), dt) for _ in range(N_ODD)]
    init_even_s = [(state_gdn_conv[j], state_gdn_S[j], state_mlstm_C[j], state_mlstm_n[j], state_mlstm_m[j])
                   for j in range(N_EVEN)]
    init_odd_s = [state_conv[j] for j in range(N_ODD)]
    weights = (norm_g, final_norm, ffn_w_gu, ffn_w_d, w_ple, w_ple_gate, w_in_even, gdn_conv_w, gdn_a_log,
               gdn_dt_bias, gdn_norm_w, mlstm_b_i, mlstm_b_f, mlstm_norm_w, w_out_even, w_in_odd, conv_c_w,
               w_out_odd)
    y_p, gconv_p, gS_p, mC_p, mn_p, mm_p, conv_p = run_trunk(x_prompt, p_prompt, init_even_p, init_odd_p, *weights)
    y_s, gconv_s, gS_s, mC_s, mn_s, mm_s, conv_s = run_trunk(x_sample, p_sample, init_even_s, init_odd_s, *weights)
    return (y_p, y_s, gconv_p, gS_p, mC_p, mn_p, mm_p, conv_p, gconv_s, gS_s, mC_s, mn_s, mm_s, conv_s)
```

```python
import functools

import jax
import jax.numpy as jnp
import numpy as np
from jax import lax
from jax.experimental import pallas as pl
from jax.experimental.pallas import tpu as pltpu

F32 = jnp.float32
BF16 = jnp.bfloat16

EPS = 1e-6
CHUNK = 64
H_A, DK_A, DV_A, CONV_A = 4, 128, 128, 4
H_B, DK_B, DV_B = 4, 128, 128
CONV_C = 3
QKV_A = H_A * (2 * DK_A + DV_A)
FF_CHUNK = 256
GATE_PAD = 128
VMEM_LIMIT = 56 * 1024 * 1024


def _rms(x, g):
    ms = jnp.mean(x * x, axis=-1, keepdims=True)
    return x * lax.rsqrt(ms + EPS) * g


def _dot(a, b):
    return jnp.dot(a, b, preferred_element_type=F32)


def _const_spec(shape):
    n = len(shape)
    return pl.BlockSpec(shape, lambda *_: (0,) * n, pipeline_mode=pl.Buffered(1))


def _params(sem):
    return pltpu.CompilerParams(dimension_semantics=sem, vmem_limit_bytes=VMEM_LIMIT)


def _row_tile(n, want):
    t = min(n, want)
    assert n % t == 0, (n, t)
    return t


def _ffn_kernel(x_ref, g_ref, wgu_ref, wd_ref, o_ref, act_ref):
    x = x_ref[...]
    xn = _rms(x, g_ref[...]).astype(BF16)
    for c in range(wgu_ref.shape[0]):
        h = _dot(xn, wgu_ref[c])
        a = h[:, :FF_CHUNK]
        b = h[:, FF_CHUNK:]
        act_ref[:, c * FF_CHUNK:(c + 1) * FF_CHUNK] = (a * jax.nn.sigmoid(a) * b).astype(BF16)
    o_ref[...] = x + 0.5 * _dot(act_ref[...], wd_ref[...])


def _ffn(x, g, wgu, wd):
    n, d = x.shape
    nch, _, two_fc = wgu.shape
    dff = wd.shape[0]
    tm = _row_tile(n, 512)
    return pl.pallas_call(
        _ffn_kernel,
        out_shape=jax.ShapeDtypeStruct((n, d), F32),
        grid=(n // tm,),
        in_specs=[pl.BlockSpec((tm, d), lambda i: (i, 0)),
                  _const_spec((1, d)),
                  _const_spec((nch, d, two_fc)),
                  _const_spec((dff, d))],
        out_specs=pl.BlockSpec((tm, d), lambda i: (i, 0)),
        scratch_shapes=[pltpu.VMEM((tm, dff), BF16)],
        compiler_params=_params(("parallel",)),
        name="ffn",
    )(x, g, wgu, wd)


def _ple_kernel(x_ref, p_ref, g_ref, gf_ref, wple_ref, wgate_ref, o_ref, *, final):
    x = x_ref[...]
    xg = _rms(x, g_ref[...]).astype(BF16)
    gate = jax.nn.sigmoid(_dot(xg, wgate_ref[...]))
    y = x + _dot(p_ref[...].astype(BF16), wple_ref[...]) * gate
    if final:
        y = _rms(y, gf_ref[...])
    o_ref[...] = y


def _ple(x, p, g, gf, wple, wgate, final):
    n, d = x.shape
    dp = p.shape[1]
    tm = _row_tile(n, 1024)
    return pl.pallas_call(
        functools.partial(_ple_kernel, final=final),
        out_shape=jax.ShapeDtypeStruct((n, d), F32),
        grid=(n // tm,),
        in_specs=[pl.BlockSpec((tm, d), lambda i: (i, 0)),
                  pl.BlockSpec((tm, dp), lambda i: (i, 0)),
                  _const_spec((1, d)), _const_spec((1, d)),
                  _const_spec((dp, d)), _const_spec((d, d))],
        out_specs=pl.BlockSpec((tm, d), lambda i: (i, 0)),
        compiler_params=_params(("parallel",)),
        name="ple",
    )(x, p, g, gf, wple, wgate)


def _odd_kernel(x_ref, g_ref, st_ref, win_ref, cw_ref, wout_ref, o_ref, newst_ref, hist_ref):
    sb, tt, d = x_ref.shape
    x = x_ref[...].reshape(sb * tt, d)
    xn = _rms(x, g_ref[...]).astype(BF16)
    proj = _dot(xn, win_ref[...])
    u = proj[:, 2 * d:] * proj[:, :d]
    bg = proj[:, d:2 * d]

    @pl.when(pl.program_id(1) == 0)
    def _():
        hist_ref[:, 6:8, :] = st_ref[...]

    hist_ref[:, 8:8 + tt, :] = u.reshape(sb, tt, d)
    cw = cw_ref[...]
    conv = (hist_ref[:, 6:6 + tt, :] * cw[0:1] + hist_ref[:, 7:7 + tt, :] * cw[1:2]
            + hist_ref[:, 8:8 + tt, :] * cw[2:3])
    y = (bg * conv.reshape(sb * tt, d)).astype(BF16)
    o_ref[...] = (x + _dot(y, wout_ref[...])).reshape(sb, tt, d)
    last = hist_ref[:, tt + 6:tt + 8, :]
    hist_ref[:, 6:8, :] = last
    newst_ref[...] = last


def _odd_mixer(x, g, st, win, cw, wout):
    b, t, d = x.shape
    tt = _row_tile(t, 512)
    sb = b if tt < 128 else 1
    return pl.pallas_call(
        _odd_kernel,
        out_shape=(jax.ShapeDtypeStruct((b, t, d), F32),
                   jax.ShapeDtypeStruct((b, CONV_C - 1, d), F32)),
        grid=(b // sb, t // tt),
        in_specs=[pl.BlockSpec((sb, tt, d), lambda i, j: (i, j, 0)),
                  _const_spec((1, d)),
                  pl.BlockSpec((sb, CONV_C - 1, d), lambda i, j: (i, 0, 0)),
                  _const_spec((d, 3 * d)), _const_spec((CONV_C, d)), _const_spec((d, d))],
        out_specs=(pl.BlockSpec((sb, tt, d), lambda i, j: (i, j, 0)),
                   pl.BlockSpec((sb, CONV_C - 1, d), lambda i, j: (i, 0, 0))),
        scratch_shapes=[pltpu.VMEM((sb, tt + 8, d), F32)],
        compiler_params=_params(("parallel", "arbitrary")),
        name="odd_mixer",
    )(x, g, st, win, cw, wout)


def _split_bf16(a):
    hi = a.astype(BF16)
    lo = (a - hi.astype(F32)).astype(BF16)
    return hi, lo


def _even_in_kernel(x_ref, g_ref, wbig_ref, wsm_ref, big_ref, sm_ref):
    xn = _rms(x_ref[...], g_ref[...])
    xh, xl = _split_bf16(xn)
    big_ref[...] = _dot(xh, wbig_ref[...])
    wh, wl = _split_bf16(wsm_ref[...])
    sm_ref[...] = _dot(xh, wh) + (_dot(xl, wh) + _dot(xh, wl))


def _even_in(x, g, wbig, wsm):
    n, d = x.shape
    nb = wbig.shape[1]
    tm = _row_tile(n, 512)
    return pl.pallas_call(
        _even_in_kernel,
        out_shape=(jax.ShapeDtypeStruct((n, nb), F32), jax.ShapeDtypeStruct((n, GATE_PAD), F32)),
        grid=(n // tm,),
        in_specs=[pl.BlockSpec((tm, d), lambda i: (i, 0)), _const_spec((1, d)),
                  _const_spec((d, nb)), _const_spec((d, GATE_PAD))],
        out_specs=(pl.BlockSpec((tm, nb), lambda i: (i, 0)), pl.BlockSpec((tm, GATE_PAD), lambda i: (i, 0))),
        compiler_params=_params(("parallel",)),
        name="even_in",
    )(x, g, wbig, wsm)


def _even_out_kernel(x_ref, m_ref, w_ref, o_ref):
    o_ref[...] = x_ref[...] + _dot(m_ref[...].astype(BF16), w_ref[...])


def _even_out(x, mixed, w):
    n, d = x.shape
    k = mixed.shape[1]
    tm = _row_tile(n, 1024)
    return pl.pallas_call(
        _even_out_kernel,
        out_shape=jax.ShapeDtypeStruct((n, d), F32),
        grid=(n // tm,),
        in_specs=[pl.BlockSpec((tm, d), lambda i: (i, 0)), pl.BlockSpec((tm, k), lambda i: (i, 0)),
                  _const_spec((k, d))],
        out_specs=pl.BlockSpec((tm, d), lambda i: (i, 0)),
        compiler_params=_params(("parallel",)),
        name="even_out",
    )(x, mixed, w)


def _to_chunks(a, cs):
    B, T, H = a.shape[:3]
    a = a.reshape((B, T // cs, cs, H) + a.shape[3:])
    perm = (1, 0, 3, 2) + tuple(range(4, a.ndim))
    return a.transpose(perm)


def _from_chunks(a):
    N, B, H, cs, d = a.shape
    return a.transpose(1, 0, 3, 2, 4).reshape(B, N * cs, H, d)


def _gated_delta_chunked(q, k, v, beta, g, S0):
    T = q.shape[1]
    cs = min(CHUNK, T)
    dv = v.shape[-1]
    q, k, v = _to_chunks(q, cs), _to_chunks(k, cs), _to_chunks(v, cs)
    beta, g = _to_chunks(beta, cs), _to_chunks(g, cs)
    G = jnp.cumsum(g, axis=-1)
    idx = jnp.arange(cs)
    incl = idx[:, None] >= idx[None, :]
    strict = idx[:, None] > idx[None, :]
    diff = G[..., :, None] - G[..., None, :]
    gam = jnp.where(incl, jnp.exp(jnp.where(incl, diff, 0.0)), 0.0)
    kk = jnp.einsum('nbhtd,nbhsd->nbhts', k, k)
    A = jnp.eye(cs, dtype=jnp.float32) + jnp.where(strict, beta[..., :, None] * kk * gam, 0.0)
    eG = jnp.exp(G)
    rhs = jnp.concatenate([beta[..., None] * v, (beta * eG)[..., None] * k], axis=-1)
    sol = lax.linalg.triangular_solve(A, rhs, left_side=True, lower=True, unit_diagonal=True)
    u_v, w = sol[..., :dv], sol[..., dv:]
    p_qk = jnp.einsum('nbhtd,nbhsd->nbhts', q, k) * gam
    k_end = k * jnp.exp(G[..., -1:] - G)[..., None]
    gL = G[..., -1]

    def step(S, xs):
        q_c, u_v_c, w_c, p_c, eG_c, k_end_c, gL_c = xs
        u = u_v_c - jnp.einsum('bhtk,bhkv->bhtv', w_c, S)
        o = eG_c[..., None] * jnp.einsum('bhtk,bhkv->bhtv', q_c, S) + jnp.einsum('bhts,bhsv->bhtv', p_c, u)
        S = jnp.exp(gL_c)[..., None, None] * S + jnp.einsum('bhsk,bhsv->bhkv', k_end_c, u)
        return S, o

    S, o = lax.scan(step, S0, (q, u_v, w, p_qk, eG, k_end, gL))
    return _from_chunks(o), S


def _mlstm_chunked(q, k, v, logi, logf, C0, n0, m0):
    T = q.shape[1]
    cs = min(CHUNK, T)
    q, k, v = _to_chunks(q, cs), _to_chunks(k, cs), _to_chunks(v, cs)
    logi, logf = _to_chunks(logi, cs), _to_chunks(logf, cs)
    b = jnp.cumsum(logf, axis=-1)
    idx = jnp.arange(cs)
    incl = idx[:, None] >= idx[None, :]
    dmat = jnp.where(incl, b[..., :, None] - b[..., None, :] + logi[..., None, :], -jnp.inf)
    dmax = jnp.max(dmat, axis=-1)
    s_qk = jnp.einsum('nbhtd,nbhsd->nbhts', q, k)

    def step(carry, xs):
        C, n, m = carry
        q_c, k_c, v_c, b_c, d_c, dmax_c, s_c = xs
        m_t = jnp.maximum(b_c + m[..., None], dmax_c)
        w_inter = jnp.exp(b_c + m[..., None] - m_t)
        w_intra = jnp.exp(d_c - m_t[..., None])
        a = s_c * w_intra
        num = w_inter[..., None] * jnp.einsum('bhtk,bhkv->bhtv', q_c, C) + jnp.einsum('bhts,bhsv->bhtv', a, v_c)
        den = w_inter * jnp.einsum('bhtk,bhk->bht', q_c, n) + jnp.sum(a, axis=-1)
        h = num / jnp.maximum(jnp.abs(den), jnp.exp(-m_t))[..., None]
        decay = w_inter[..., -1]
        kw = k_c * w_intra[..., -1, :][..., None]
        C = decay[..., None, None] * C + jnp.einsum('bhsk,bhsv->bhkv', kw, v_c)
        n = decay[..., None] * n + jnp.sum(kw, axis=-2)
        return (C, n, m_t[..., -1]), h

    (C, n, m), h = lax.scan(step, (C0, n0, m0), (q, k, v, b, dmat, dmax, s_qk))
    return _from_chunks(h), C, n, m


def _causal_dwconv(hist, w):
    K = w.shape[0]
    T = hist.shape[1] - K + 1
    out = hist[:, 0:T] * w[0]
    for j in range(1, K):
        out = out + hist[:, j:j + T] * w[j]
    return out


def _l2norm(x):
    return x * lax.rsqrt(jnp.sum(x * x, axis=-1, keepdims=True) + EPS)


def _even_core(big, sm, conv_st, S0, C0, n0, m0, conv_w, a_log, dt_bias, gdn_norm_w, b_i, b_f, mlstm_norm_w):
    B, T, _ = big.shape
    hd = H_A * DV_A
    qkv_a = big[..., :QKV_A]
    z_a = big[..., QKV_A:QKV_A + hd]
    o0 = QKV_A + hd
    q_b, k_b, v_b, o_b = [big[..., o0 + i * 512:o0 + (i + 1) * 512] for i in range(4)]
    beta_a, a_a, i_b, f_b = [sm[..., 4 * i:4 * i + 4] for i in range(4)]
    hist = jnp.concatenate([conv_st, qkv_a], axis=1)
    new_conv = hist[:, -(CONV_A - 1):]
    qkv = jax.nn.silu(_causal_dwconv(hist, conv_w))
    q_a, k_a, v_a = jnp.split(qkv, [H_A * DK_A, 2 * H_A * DK_A], axis=-1)
    q_a = _l2norm(q_a.reshape(B, T, H_A, DK_A)) * (DK_A ** -0.5)
    k_a = _l2norm(k_a.reshape(B, T, H_A, DK_A))
    v_a = v_a.reshape(B, T, H_A, DV_A)
    beta = jax.nn.sigmoid(beta_a)
    g = -jnp.exp(a_log) * jax.nn.softplus(a_a + dt_bias)
    o_a, S = _gated_delta_chunked(q_a, k_a, v_a, beta, g, S0)
    o_a = o_a * lax.rsqrt(jnp.mean(o_a * o_a, axis=-1, keepdims=True) + EPS) * gdn_norm_w
    o_a = (o_a * jax.nn.silu(z_a.reshape(B, T, H_A, DV_A))).reshape(B, T, H_A * DV_A)
    q_b = q_b.reshape(B, T, H_B, DK_B) * (DK_B ** -0.5)
    k_b = k_b.reshape(B, T, H_B, DK_B)
    v_b = v_b.reshape(B, T, H_B, DV_B)
    logi = i_b + b_i
    logf = jax.nn.log_sigmoid(f_b + b_f)
    h_b, C, n, m = _mlstm_chunked(q_b, k_b, v_b, logi, logf, C0, n0, m0)
    h_b = h_b * lax.rsqrt(jnp.mean(h_b * h_b, axis=-1, keepdims=True) + EPS)
    h_b = h_b.reshape(B, T, H_B * DV_B) * mlstm_norm_w * jax.nn.sigmoid(o_b)
    return jnp.concatenate([o_a, h_b], axis=-1), (new_conv, S, C, n, m)


def _prep_weights(ffn_w_gu, ffn_w_d, w_ple, w_ple_gate, w_in_even, w_out_even, w_in_odd, w_out_odd):
    depth, _, d, two_ff = ffn_w_gu.shape
    dff = two_ff // 2
    nch = dff // FF_CHUNK
    wgu = ffn_w_gu.astype(BF16).reshape(depth, 2, d, 2, nch, FF_CHUNK)
    wgu = wgu.transpose(0, 1, 4, 2, 3, 5).reshape(depth, 2, nch, d, 2 * FF_CHUNK)
    nbig = QKV_A + H_A * DV_A
    nmid = nbig + 2 * H_A
    nb_end = nmid + 4 * H_B * DK_B
    w_big = jnp.concatenate([w_in_even[:, :, :nbig], w_in_even[:, :, nmid:nb_end]], axis=-1).astype(BF16)
    w_sm = jnp.concatenate([w_in_even[:, :, nbig:nmid], w_in_even[:, :, nb_end:]], axis=-1)
    w_sm = jnp.pad(w_sm, ((0, 0), (0, 0), (0, GATE_PAD - w_sm.shape[-1])))
    return dict(wgu=wgu, wd=ffn_w_d.astype(BF16), wple=w_ple.astype(BF16), wgate=w_ple_gate.astype(BF16),
                w_big=w_big, w_sm=w_sm, w_out_even=w_out_even.astype(BF16),
                w_in_odd=w_in_odd.astype(BF16), w_out_odd=w_out_odd.astype(BF16))


def _run_trunk(x, p, init_even, init_odd, W, norm_g, final_norm, gdn_conv_w, gdn_a_log, gdn_dt_bias,
               gdn_norm_w, mlstm_b_i, mlstm_b_f, mlstm_norm_w, conv_c_w):
    B, T, D = x.shape
    depth = norm_g.shape[0]
    n = B * T
    x = x.reshape(n, D)
    new_even, new_odd = [], []
    gf = final_norm.reshape(1, D)
    for i in range(depth):
        j = i // 2
        g = lambda k: norm_g[i, k].reshape(1, D)
        x = _ffn(x, g(0), W["wgu"][i, 0], W["wd"][i, 0])
        if i % 2 == 0:
            big, sm = _even_in(x, g(1), W["w_big"][j], W["w_sm"][j])
            mixed, st = _even_core(big.reshape(B, T, -1), sm.reshape(B, T, -1), *init_even[j], gdn_conv_w[j],
                                   gdn_a_log[j], gdn_dt_bias[j], gdn_norm_w[j], mlstm_b_i[j], mlstm_b_f[j],
                                   mlstm_norm_w[j])
            x = _even_out(x, mixed.reshape(n, -1), W["w_out_even"][j])
            new_even.append(st)
        else:
            y, st = _odd_mixer(x.reshape(B, T, D), g(1), init_odd[j], W["w_in_odd"][j], conv_c_w[j],
                               W["w_out_odd"][j])
            x = y.reshape(n, D)
            new_odd.append(st)
        x = _ffn(x, g(2), W["wgu"][i, 1], W["wd"][i, 1])
        x = _ple(x, p[i].reshape(n, -1), g(3), gf, W["wple"][i], W["wgate"][i], final=(i == depth - 1))
    st_even = [jnp.stack([s[c] for s in new_even]) for c in range(5)]
    return (x.reshape(B, T, D), *st_even, jnp.stack(new_odd))


def kernel(x_prompt, x_sample, state_gdn_conv, state_gdn_S, state_mlstm_C, state_mlstm_n, state_mlstm_m,
           state_conv, p_prompt, p_sample, norm_g, final_norm, ffn_w_gu, ffn_w_d, w_ple, w_ple_gate,
           w_in_even, gdn_conv_w, gdn_a_log, gdn_dt_bias, gdn_norm_w, mlstm_b_i, mlstm_b_f, mlstm_norm_w,
           w_out_even, w_in_odd, conv_c_w, w_out_odd):
    Bp = x_prompt.shape[0]
    n_even, n_odd = state_gdn_S.shape[0], state_conv.shape[0]
    D = x_prompt.shape[-1]
    W = _prep_weights(ffn_w_gu, ffn_w_d, w_ple, w_ple_gate, w_in_even, w_out_even, w_in_odd, w_out_odd)
    zero_even = (jnp.zeros((Bp, CONV_A - 1, QKV_A), F32), jnp.zeros((Bp, H_A, DK_A, DV_A), F32),
                 jnp.zeros((Bp, H_B, DK_B, DV_B), F32), jnp.zeros((Bp, H_B, DK_B), F32),
                 jnp.zeros((Bp, H_B), F32))
    init_even_p = [zero_even] * n_even
    init_odd_p = [jnp.zeros((Bp, CONV_C - 1, D), F32)] * n_odd
    init_even_s = [(state_gdn_conv[j], state_gdn_S[j], state_mlstm_C[j], state_mlstm_n[j], state_mlstm_m[j])
                   for j in range(n_even)]
    init_odd_s = [state_conv[j] for j in range(n_odd)]
    rest = (norm_g, final_norm, gdn_conv_w, gdn_a_log, gdn_dt_bias, gdn_norm_w, mlstm_b_i, mlstm_b_f,
            mlstm_norm_w, conv_c_w)
    outs_p = _run_trunk(x_prompt, p_prompt, init_even_p, init_odd_p, W, *rest)
    outs_s = _run_trunk(x_sample, p_sample, init_even_s, init_odd_s, W, *rest)
    return (outs_p[0], outs_s[0]) + tuple(outs_p[1:]) + tuple(outs_s[1:])
```

```python
import functools

import jax
import jax.numpy as jnp
import numpy as np
from jax import lax
from jax.experimental import pallas as pl
from jax.experimental.pallas import tpu as pltpu

F32 = jnp.float32
BF16 = jnp.bfloat16

EPS = 1e-6
H_A, DK_A, DV_A, CONV_A = 4, 128, 128, 4
H_B, DK_B, DV_B = 4, 128, 128
CONV_C = 3
QKV_A = H_A * (2 * DK_A + DV_A)
FF_CHUNK = 256
GATE_PAD = 128
SEQ_CHUNK = 128
N_LEVELS = 7
VMEM_LIMIT = 56 * 1024 * 1024

OFF_Z = QKV_A
OFF_QB = OFF_Z + H_A * DV_A
OFF_KB = OFF_QB + H_B * DK_B
OFF_VB = OFF_KB + H_B * DK_B
OFF_OB = OFF_VB + H_B * DV_B
N_BIG = OFF_OB + H_B * DV_B
COL_BETA, COL_G, COL_LOGI, COL_LOGF = 0, H_A, 2 * H_A, 2 * H_A + H_B


def _rms(x, g):
    ms = jnp.mean(x * x, axis=-1, keepdims=True)
    return x * lax.rsqrt(ms + EPS) * g


def _dot(a, b):
    return jnp.dot(a, b, preferred_element_type=F32)


def _bdot(a, b):
    return jnp.dot(a.astype(BF16), b.astype(BF16), preferred_element_type=F32)


def _const_spec(shape):
    n = len(shape)
    return pl.BlockSpec(shape, lambda *_: (0,) * n, pipeline_mode=pl.Buffered(1))


def _params(sem):
    return pltpu.CompilerParams(dimension_semantics=sem, vmem_limit_bytes=VMEM_LIMIT)


def _row_tile(n, want):
    t = min(n, want)
    assert n % t == 0, (n, t)
    return t


def _ffn_kernel(x_ref, g_ref, wgu_ref, wd_ref, o_ref, act_ref):
    x = x_ref[...]
    xn = _rms(x, g_ref[...]).astype(BF16)
    for c in range(wgu_ref.shape[0]):
        h = _dot(xn, wgu_ref[c])
        a = h[:, :FF_CHUNK]
        b = h[:, FF_CHUNK:]
        act_ref[:, c * FF_CHUNK:(c + 1) * FF_CHUNK] = (a * jax.nn.sigmoid(a) * b).astype(BF16)
    o_ref[...] = x + 0.5 * _dot(act_ref[...], wd_ref[...])


def _ffn(x, g, wgu, wd):
    n, d = x.shape
    nch, _, two_fc = wgu.shape
    dff = wd.shape[0]
    tm = _row_tile(n, 512)
    return pl.pallas_call(
        _ffn_kernel,
        out_shape=jax.ShapeDtypeStruct((n, d), F32),
        grid=(n // tm,),
        in_specs=[pl.BlockSpec((tm, d), lambda i: (i, 0)),
                  _const_spec((1, d)),
                  _const_spec((nch, d, two_fc)),
                  _const_spec((dff, d))],
        out_specs=pl.BlockSpec((tm, d), lambda i: (i, 0)),
        scratch_shapes=[pltpu.VMEM((tm, dff), BF16)],
        compiler_params=_params(("parallel",)),
        name="ffn",
    )(x, g, wgu, wd)


def _ple_kernel(x_ref, p_ref, g_ref, gf_ref, wple_ref, wgate_ref, o_ref, *, final):
    x = x_ref[...]
    xg = _rms(x, g_ref[...]).astype(BF16)
    gate = jax.nn.sigmoid(_dot(xg, wgate_ref[...]))
    y = x + _dot(p_ref[...].astype(BF16), wple_ref[...]) * gate
    if final:
        y = _rms(y, gf_ref[...])
    o_ref[...] = y


def _ple(x, p, g, gf, wple, wgate, final):
    n, d = x.shape
    dp = p.shape[1]
    tm = _row_tile(n, 1024)
    return pl.pallas_call(
        functools.partial(_ple_kernel, final=final),
        out_shape=jax.ShapeDtypeStruct((n, d), F32),
        grid=(n // tm,),
        in_specs=[pl.BlockSpec((tm, d), lambda i: (i, 0)),
                  pl.BlockSpec((tm, dp), lambda i: (i, 0)),
                  _const_spec((1, d)), _const_spec((1, d)),
                  _const_spec((dp, d)), _const_spec((d, d))],
        out_specs=pl.BlockSpec((tm, d), lambda i: (i, 0)),
        compiler_params=_params(("parallel",)),
        name="ple",
    )(x, p, g, gf, wple, wgate)


def _odd_kernel(x_ref, g_ref, st_ref, win_ref, cw_ref, wout_ref, o_ref, newst_ref, hist_ref):
    sb, tt, d = x_ref.shape
    x = x_ref[...].reshape(sb * tt, d)
    xn = _rms(x, g_ref[...]).astype(BF16)
    proj = _dot(xn, win_ref[...])
    u = proj[:, 2 * d:] * proj[:, :d]
    bg = proj[:, d:2 * d]

    @pl.when(pl.program_id(1) == 0)
    def _():
        hist_ref[:, 6:8, :] = st_ref[...]

    hist_ref[:, 8:8 + tt, :] = u.reshape(sb, tt, d)
    cw = cw_ref[...]
    conv = (hist_ref[:, 6:6 + tt, :] * cw[0:1] + hist_ref[:, 7:7 + tt, :] * cw[1:2]
            + hist_ref[:, 8:8 + tt, :] * cw[2:3])
    y = (bg * conv.reshape(sb * tt, d)).astype(BF16)
    o_ref[...] = (x + _dot(y, wout_ref[...])).reshape(sb, tt, d)
    last = hist_ref[:, tt + 6:tt + 8, :]
    hist_ref[:, 6:8, :] = last
    newst_ref[...] = last


def _odd_mixer(x, g, st, win, cw, wout):
    b, t, d = x.shape
    tt = _row_tile(t, 512)
    sb = b if tt < 128 else 1
    return pl.pallas_call(
        _odd_kernel,
        out_shape=(jax.ShapeDtypeStruct((b, t, d), F32),
                   jax.ShapeDtypeStruct((b, CONV_C - 1, d), F32)),
        grid=(b // sb, t // tt),
        in_specs=[pl.BlockSpec((sb, tt, d), lambda i, j: (i, j, 0)),
                  _const_spec((1, d)),
                  pl.BlockSpec((sb, CONV_C - 1, d), lambda i, j: (i, 0, 0)),
                  _const_spec((d, 3 * d)), _const_spec((CONV_C, d)), _const_spec((d, d))],
        out_specs=(pl.BlockSpec((sb, tt, d), lambda i, j: (i, j, 0)),
                   pl.BlockSpec((sb, CONV_C - 1, d), lambda i, j: (i, 0, 0))),
        scratch_shapes=[pltpu.VMEM((sb, tt + 8, d), F32)],
        compiler_params=_params(("parallel", "arbitrary")),
        name="odd_mixer",
    )(x, g, st, win, cw, wout)


def _split_bf16(a):
    hi = a.astype(BF16)
    lo = (a - hi.astype(F32)).astype(BF16)
    return hi, lo


def _even_in_kernel(x_ref, g_ref, wbig_ref, wsm_ref, big_ref, sm_ref):
    xn = _rms(x_ref[...], g_ref[...])
    xh, xl = _split_bf16(xn)
    big_ref[...] = _dot(xh, wbig_ref[...])
    wh, wl = _split_bf16(wsm_ref[...])
    sm_ref[...] = _dot(xh, wh) + (_dot(xl, wh) + _dot(xh, wl))


def _even_in(x, g, wbig, wsm):
    n, d = x.shape
    nb = wbig.shape[1]
    tm = _row_tile(n, 512)
    return pl.pallas_call(
        _even_in_kernel,
        out_shape=(jax.ShapeDtypeStruct((n, nb), F32), jax.ShapeDtypeStruct((n, GATE_PAD), F32)),
        grid=(n // tm,),
        in_specs=[pl.BlockSpec((tm, d), lambda i: (i, 0)), _const_spec((1, d)),
                  _const_spec((d, nb)), _const_spec((d, GATE_PAD))],
        out_specs=(pl.BlockSpec((tm, nb), lambda i: (i, 0)), pl.BlockSpec((tm, GATE_PAD), lambda i: (i, 0))),
        compiler_params=_params(("parallel",)),
        name="even_in",
    )(x, g, wbig, wsm)


def _softplus_parts(y):
    t = jnp.log1p(jnp.exp(-jnp.abs(y)))
    return jnp.maximum(y, 0.0) + t, jnp.maximum(-y, 0.0) + t


def _even_seq_kernel(x_ref, big_ref, sm_ref, cst_ref, s0_ref, c0_ref, m0_ref, convw_ref, gp_ref, gnw_ref, mnw_ref,
                     lvl_ref, tri_ref, wout_ref,
                     o_ref, ncst_ref, s_ref, c_ref, m_ref, hist_ref, mix_ref, *, t_valid):
    L = SEQ_CHUNK

    @pl.when(pl.program_id(1) == 0)
    def _():
        hist_ref[5:8, :] = cst_ref[...]
        s_ref[...] = s0_ref[...]
        c_ref[...] = c0_ref[...]
        m_ref[...] = m0_ref[...]

    row = lax.broadcasted_iota(jnp.int32, (L, L), 0)
    col = lax.broadcasted_iota(jnp.int32, (L, L), 1)
    incl = row >= col
    col_ok = incl if t_valid == L else (incl & (col < t_valid))

    y = sm_ref[...] + gp_ref[0:1, :]
    sp_pos, sp_neg = _softplus_parts(y)
    gt = jnp.where(col < COL_G, jax.nn.sigmoid(y),
                   jnp.where(col < COL_LOGI, -jnp.exp(gp_ref[1:2, :]) * sp_pos,
                             jnp.where(col < COL_LOGF, y, -sp_neg)))
    gt = jnp.where(col < COL_LOGF + H_B, gt, 0.0)
    if t_valid < L:
        gt = jnp.where(row < t_valid, gt, 0.0)
    g1 = gt.astype(BF16)
    r1 = gt - g1.astype(F32)
    g2 = r1.astype(BF16)
    g3 = (r1 - g2.astype(F32)).astype(BF16)
    tri = tri_ref[...]
    cum = _dot(tri, g1) + (_dot(tri, g2) + _dot(tri, g3))
    gt_t = gt.T
    cum_t = cum.T

    hist_ref[8:8 + L, :] = big_ref[:, :QKV_A]

    def conv_silu(c0):
        acc = hist_ref[5:5 + L, c0:c0 + 128] * convw_ref[0:1, c0:c0 + 128]
        for j in range(1, CONV_A):
            acc = acc + hist_ref[5 + j:5 + j + L, c0:c0 + 128] * convw_ref[j:j + 1, c0:c0 + 128]
        return acc * jax.nn.sigmoid(acc)

    eye = jnp.where(row == col, 1.0, 0.0)
    one_col = jnp.where(col == 0, 1.0, 0.0)

    for h in range(H_A):
        q = conv_silu(h * DK_A)
        k = conv_silu(H_A * DK_A + h * DK_A)
        v = conv_silu(2 * H_A * DK_A + h * DV_A)
        q = q * lax.rsqrt(jnp.sum(q * q, axis=-1, keepdims=True) + EPS) * (DK_A ** -0.5)
        k = k * lax.rsqrt(jnp.sum(k * k, axis=-1, keepdims=True) + EPS)
        beta = gt[:, COL_BETA + h:COL_BETA + h + 1]
        g_c = cum[:, COL_G + h:COL_G + h + 1]
        g_r = cum_t[COL_G + h:COL_G + h + 1, :]
        gam = jnp.where(incl, jnp.exp(jnp.where(incl, g_c - g_r, 0.0)), 0.0)
        k_t = k.T
        qk_kk = _bdot(jnp.concatenate([q, k], axis=0), k_t)
        qk = qk_kk[:L]
        a_s = beta * qk_kk[L:] * gam
        d_inv = eye - a_s * lvl_ref[0]
        for lv in range(1, N_LEVELS):
            de = _bdot(d_inv, a_s * lvl_ref[lv])
            d_inv = d_inv - _bdot(de, d_inv)
        e_g = jnp.exp(g_c)
        sol = _bdot(d_inv, jnp.concatenate([beta * v, (beta * e_g) * k], axis=1))
        s_old = s_ref[h]
        wq = _bdot(jnp.concatenate([sol[:, DV_A:], q], axis=0), s_old)
        u = sol[:, :DV_A] - wq[:L]
        o = e_g * wq[L:] + _bdot(qk * gam, u)
        g_last = g_r[:, L - 1:L]
        s_ref[h] = jnp.exp(g_last) * s_old + _bdot(k_t * jnp.exp(g_last - g_r), u)
        o = o * lax.rsqrt(jnp.mean(o * o, axis=-1, keepdims=True) + EPS) * gnw_ref[...]
        z = big_ref[:, OFF_Z + h * DV_A:OFF_Z + (h + 1) * DV_A]
        mix_ref[:, h * DV_A:(h + 1) * DV_A] = (o * (z * jax.nn.sigmoid(z))).astype(BF16)

    for h in range(H_B):
        q = big_ref[:, OFF_QB + h * DK_B:OFF_QB + (h + 1) * DK_B] * (DK_B ** -0.5)
        k = big_ref[:, OFF_KB + h * DK_B:OFF_KB + (h + 1) * DK_B]
        v = big_ref[:, OFF_VB + h * DV_B:OFF_VB + (h + 1) * DV_B]
        b_c = cum[:, COL_LOGF + h:COL_LOGF + h + 1]
        b_r = cum_t[COL_LOGF + h:COL_LOGF + h + 1, :]
        li_r = gt_t[COL_LOGI + h:COL_LOGI + h + 1, :]
        dmat = jnp.where(col_ok, b_c - b_r + li_r, -jnp.inf)
        m_prev = m_ref[h, 0:1, 0:1]
        m_t = jnp.maximum(b_c + m_prev, jnp.max(dmat, axis=-1, keepdims=True))
        w_inter = jnp.exp(b_c + m_prev - m_t)
        w_intra = jnp.exp(dmat - m_t)
        k_t = k.T
        a = _bdot(q, k_t) * w_intra
        c_old = c_ref[h]
        v_ext = jnp.concatenate([v, one_col], axis=1)
        tot = w_inter * _bdot(q, c_old) + _bdot(a, v_ext)
        den = tot[:, DV_B:DV_B + 1]
        hh = tot[:, :DV_B] / jnp.maximum(jnp.abs(den), jnp.exp(-m_t))
        c_ref[h] = w_inter[L - 1:L, :] * c_old + _bdot(k_t * w_intra[L - 1:L, :], v_ext)
        m_ref[h] = jnp.broadcast_to(m_t[L - 1:L, :], m_ref.shape[1:])
        hh = hh * lax.rsqrt(jnp.mean(hh * hh, axis=-1, keepdims=True) + EPS)
        ob = big_ref[:, OFF_OB + h * DV_B:OFF_OB + (h + 1) * DV_B]
        hh = hh * mnw_ref[:, h * DV_B:(h + 1) * DV_B] * jax.nn.sigmoid(ob)
        mix_ref[:, H_A * DV_A + h * DV_B:H_A * DV_A + (h + 1) * DV_B] = hh.astype(BF16)

    o_ref[...] = x_ref[...] + _dot(mix_ref[...], wout_ref[...])

    last = hist_ref[5 + t_valid:8 + t_valid, :]
    hist_ref[5:8, :] = last
    ncst_ref[...] = last


def _level_masks():
    t = np.arange(SEQ_CHUNK)
    out = []
    for lv in range(N_LEVELS):
        b = 1 << lv
        tb, sb = t[:, None] // b, t[None, :] // b
        out.append(((tb % 2 == 1) & (sb == tb - 1)).astype(np.float32))
    return np.stack(out)


def _even_seq(x, big, sm, conv_st, S0, C0, n0, m0, conv_w, a_log, dt_bias, gdn_norm_w, b_i, b_f, mlstm_norm_w, wout):
    B, T, D = x.shape
    L = SEQ_CHUNK
    t_valid = L
    if T % L:
        assert T < L
        t_valid = T
        pad = lambda a: jnp.pad(a, ((0, 0), (0, L - T), (0, 0)))
        x, big, sm = pad(x), pad(big), pad(sm)
    Tp = x.shape[1]
    gp = jnp.zeros((8, GATE_PAD), F32)
    gp = gp.at[0, COL_G:COL_G + H_A].set(dt_bias).at[0, COL_LOGI:COL_LOGI + H_B].set(b_i)
    gp = gp.at[0, COL_LOGF:COL_LOGF + H_B].set(b_f).at[1, COL_G:COL_G + H_A].set(a_log)
    c_ext = jnp.concatenate([C0, n0[..., None], jnp.zeros(C0.shape[:-1] + (DV_B - 1,), F32)], axis=-1)
    m_b = jnp.broadcast_to(m0[:, :, None, None], (B, H_B, 8, 128))
    lvl = jnp.asarray(_level_masks())
    tri = jnp.asarray(np.tril(np.ones((L, L), np.float32)), BF16)
    tile = lambda w: pl.BlockSpec((None, L, w), lambda b, c: (b, c, 0))
    state = lambda *s: pl.BlockSpec((None,) + s, lambda b, c: (b,) + (0,) * len(s))
    out, ncst, S, c_new, m_new = pl.pallas_call(
        functools.partial(_even_seq_kernel, t_valid=t_valid),
        out_shape=(jax.ShapeDtypeStruct((B, Tp, D), F32),
                   jax.ShapeDtypeStruct((B, CONV_A - 1, QKV_A), F32),
                   jax.ShapeDtypeStruct((B, H_A, DK_A, DV_A), F32),
                   jax.ShapeDtypeStruct((B, H_B, DK_B, 2 * DV_B), F32),
                   jax.ShapeDtypeStruct((B, H_B, 8, 128), F32)),
        grid=(B, Tp // L),
        in_specs=[tile(D), tile(N_BIG), tile(GATE_PAD),
                  state(CONV_A - 1, QKV_A), state(H_A, DK_A, DV_A), state(H_B, DK_B, 2 * DV_B), state(H_B, 8, 128),
                  _const_spec((CONV_A, QKV_A)), _const_spec((8, GATE_PAD)), _const_spec((1, DV_A)),
                  _const_spec((1, H_B * DV_B)), _const_spec((N_LEVELS, L, L)), _const_spec((L, L)),
                  _const_spec((H_A * DV_A + H_B * DV_B, D))],
        out_specs=(tile(D), state(CONV_A - 1, QKV_A), state(H_A, DK_A, DV_A), state(H_B, DK_B, 2 * DV_B),
                   state(H_B, 8, 128)),
        scratch_shapes=[pltpu.VMEM((L + 8, QKV_A), F32), pltpu.VMEM((L, H_A * DV_A + H_B * DV_B), BF16)],
        compiler_params=_params(("parallel", "arbitrary")),
        name="even_seq",
    )(x, big, sm, conv_st, S0, c_ext, m_b, conv_w, gp, gdn_norm_w.reshape(1, DV_A),
      mlstm_norm_w.reshape(1, H_B * DV_B), lvl, tri, wout)
    return out[:, :T], (ncst, S, c_new[..., :DV_B], c_new[..., DV_B], m_new[:, :, 0, 0])


def _prep_weights(ffn_w_gu, ffn_w_d, w_ple, w_ple_gate, w_in_even, w_out_even, w_in_odd, w_out_odd):
    depth, _, d, two_ff = ffn_w_gu.shape
    dff = two_ff // 2
    nch = dff // FF_CHUNK
    wgu = ffn_w_gu.astype(BF16).reshape(depth, 2, d, 2, nch, FF_CHUNK)
    wgu = wgu.transpose(0, 1, 4, 2, 3, 5).reshape(depth, 2, nch, d, 2 * FF_CHUNK)
    nbig = QKV_A + H_A * DV_A
    nmid = nbig + 2 * H_A
    nb_end = nmid + 4 * H_B * DK_B
    w_big = jnp.concatenate([w_in_even[:, :, :nbig], w_in_even[:, :, nmid:nb_end]], axis=-1).astype(BF16)
    w_sm = jnp.concatenate([w_in_even[:, :, nbig:nmid], w_in_even[:, :, nb_end:]], axis=-1)
    w_sm = jnp.pad(w_sm, ((0, 0), (0, 0), (0, GATE_PAD - w_sm.shape[-1])))
    return dict(wgu=wgu, wd=ffn_w_d.astype(BF16), wple=w_ple.astype(BF16), wgate=w_ple_gate.astype(BF16),
                w_big=w_big, w_sm=w_sm, w_out_even=w_out_even.astype(BF16),
                w_in_odd=w_in_odd.astype(BF16), w_out_odd=w_out_odd.astype(BF16))


def _run_trunk(x, p, init_even, init_odd, W, norm_g, final_norm, gdn_conv_w, gdn_a_log, gdn_dt_bias,
               gdn_norm_w, mlstm_b_i, mlstm_b_f, mlstm_norm_w, conv_c_w):
    B, T, D = x.shape
    depth = norm_g.shape[0]
    n = B * T
    x = x.reshape(n, D)
    new_even, new_odd = [], []
    gf = final_norm.reshape(1, D)
    for i in range(depth):
        j = i // 2
        g = lambda k: norm_g[i, k].reshape(1, D)
        x = _ffn(x, g(0), W["wgu"][i, 0], W["wd"][i, 0])
        if i % 2 == 0:
            big, sm = _even_in(x, g(1), W["w_big"][j], W["w_sm"][j])
            y, st = _even_seq(x.reshape(B, T, D), big.reshape(B, T, -1), sm.reshape(B, T, -1), *init_even[j],
                              gdn_conv_w[j], gdn_a_log[j], gdn_dt_bias[j], gdn_norm_w[j], mlstm_b_i[j],
                              mlstm_b_f[j], mlstm_norm_w[j], W["w_out_even"][j])
            new_even.append(st)
        else:
            y, st = _odd_mixer(x.reshape(B, T, D), g(1), init_odd[j], W["w_in_odd"][j], conv_c_w[j],
                               W["w_out_odd"][j])
            new_odd.append(st)
        x = y.reshape(n, D)
        x = _ffn(x, g(2), W["wgu"][i, 1], W["wd"][i, 1])
        x = _ple(x, p[i].reshape(n, -1), g(3), gf, W["wple"][i], W["wgate"][i], final=(i == depth - 1))
    st_even = [jnp.stack([s[c] for s in new_even]) for c in range(5)]
    return (x.reshape(B, T, D), *st_even, jnp.stack(new_odd))


def kernel(x_prompt, x_sample, state_gdn_conv, state_gdn_S, state_mlstm_C, state_mlstm_n, state_mlstm_m,
           state_conv, p_prompt, p_sample, norm_g, final_norm, ffn_w_gu, ffn_w_d, w_ple, w_ple_gate,
           w_in_even, gdn_conv_w, gdn_a_log, gdn_dt_bias, gdn_norm_w, mlstm_b_i, mlstm_b_f, mlstm_norm_w,
           w_out_even, w_in_odd, conv_c_w, w_out_odd):
    Bp = x_prompt.shape[0]
    n_even, n_odd = state_gdn_S.shape[0], state_conv.shape[0]
    D = x_prompt.shape[-1]
    W = _prep_weights(ffn_w_gu, ffn_w_d, w_ple, w_ple_gate, w_in_even, w_out_even, w_in_odd, w_out_odd)
    zero_even = (jnp.zeros((Bp, CONV_A - 1, QKV_A), F32), jnp.zeros((Bp, H_A, DK_A, DV_A), F32),
                 jnp.zeros((Bp, H_B, DK_B, DV_B), F32), jnp.zeros((Bp, H_B, DK_B), F32),
                 jnp.zeros((Bp, H_B), F32))
    init_even_p = [zero_even] * n_even
    init_odd_p = [jnp.zeros((Bp, CONV_C - 1, D), F32)] * n_odd
    init_even_s = [(state_gdn_conv[j], state_gdn_S[j], state_mlstm_C[j], state_mlstm_n[j], state_mlstm_m[j])
                   for j in range(n_even)]
    init_odd_s = [state_conv[j] for j in range(n_odd)]
    rest = (norm_g, final_norm, gdn_conv_w, gdn_a_log, gdn_dt_bias, gdn_norm_w, mlstm_b_i, mlstm_b_f,
            mlstm_norm_w, conv_c_w)
    outs_p = _run_trunk(x_prompt, p_prompt, init_even_p, init_odd_p, W, *rest)
    outs_s = _run_trunk(x_sample, p_sample, init_even_s, init_odd_s, W, *rest)
    return (outs_p[0], outs_s[0]) + tuple(outs_p[1:]) + tuple(outs_s[1:])
```

```python
import functools

import jax
import jax.numpy as jnp
import numpy as np
from jax import lax
from jax.experimental import pallas as pl
from jax.experimental.pallas import tpu as pltpu

F32 = jnp.float32
BF16 = jnp.bfloat16

EPS = 1e-6
H_A, DK_A, DV_A, CONV_A = 4, 128, 128, 4
H_B, DK_B, DV_B = 4, 128, 128
CONV_C = 3
QKV_A = H_A * (2 * DK_A + DV_A)
FF_CHUNK = 256
GATE_PAD = 128
SEQ_CHUNK = 128
N_LEVELS = 7
VMEM_LIMIT = 56 * 1024 * 1024

OFF_Z = QKV_A
OFF_QB = OFF_Z + H_A * DV_A
OFF_KB = OFF_QB + H_B * DK_B
OFF_VB = OFF_KB + H_B * DK_B
OFF_OB = OFF_VB + H_B * DV_B
N_BIG = OFF_OB + H_B * DV_B
COL_BETA, COL_G, COL_LOGI, COL_LOGF = 0, H_A, 2 * H_A, 2 * H_A + H_B


def _rms(x, g):
    ms = jnp.mean(x * x, axis=-1, keepdims=True)
    return x * lax.rsqrt(ms + EPS) * g


def _dot(a, b):
    return jnp.dot(a, b, preferred_element_type=F32)


def _bdot(a, b):
    return jnp.dot(a.astype(BF16), b.astype(BF16), preferred_element_type=F32)


def _const_spec(shape):
    n = len(shape)
    return pl.BlockSpec(shape, lambda *_: (0,) * n, pipeline_mode=pl.Buffered(1))


def _params(sem):
    return pltpu.CompilerParams(dimension_semantics=sem, vmem_limit_bytes=VMEM_LIMIT)


def _row_tile(n, want):
    t = min(n, want)
    assert n % t == 0, (n, t)
    return t


def _ffn_kernel(x_ref, g_ref, wgu_ref, wd_ref, o_ref, act_ref):
    x = x_ref[...]
    xn = _rms(x, g_ref[...]).astype(BF16)
    for c in range(wgu_ref.shape[0]):
        h = _dot(xn, wgu_ref[c])
        a = h[:, :FF_CHUNK]
        b = h[:, FF_CHUNK:]
        act_ref[:, c * FF_CHUNK:(c + 1) * FF_CHUNK] = (a * jax.nn.sigmoid(a) * b).astype(BF16)
    o_ref[...] = x + 0.5 * _dot(act_ref[...], wd_ref[...])


def _ffn(x, g, wgu, wd):
    n, d = x.shape
    nch, _, two_fc = wgu.shape
    dff = wd.shape[0]
    tm = _row_tile(n, 512)
    return pl.pallas_call(
        _ffn_kernel,
        out_shape=jax.ShapeDtypeStruct((n, d), F32),
        grid=(n // tm,),
        in_specs=[pl.BlockSpec((tm, d), lambda i: (i, 0)),
                  _const_spec((1, d)),
                  _const_spec((nch, d, two_fc)),
                  _const_spec((dff, d))],
        out_specs=pl.BlockSpec((tm, d), lambda i: (i, 0)),
        scratch_shapes=[pltpu.VMEM((tm, dff), BF16)],
        compiler_params=_params(("parallel",)),
        name="ffn",
    )(x, g, wgu, wd)


def _ple_kernel(x_ref, p_ref, g_ref, gf_ref, wple_ref, wgate_ref, o_ref, *, final):
    x = x_ref[...]
    xg = _rms(x, g_ref[...]).astype(BF16)
    gate = jax.nn.sigmoid(_dot(xg, wgate_ref[...]))
    y = x + _dot(p_ref[...].astype(BF16), wple_ref[...]) * gate
    if final:
        y = _rms(y, gf_ref[...])
    o_ref[...] = y


def _ple(x, p, g, gf, wple, wgate, final):
    n, d = x.shape
    dp = p.shape[1]
    tm = _row_tile(n, 1024)
    return pl.pallas_call(
        functools.partial(_ple_kernel, final=final),
        out_shape=jax.ShapeDtypeStruct((n, d), F32),
        grid=(n // tm,),
        in_specs=[pl.BlockSpec((tm, d), lambda i: (i, 0)),
                  pl.BlockSpec((tm, dp), lambda i: (i, 0)),
                  _const_spec((1, d)), _const_spec((1, d)),
                  _const_spec((dp, d)), _const_spec((d, d))],
        out_specs=pl.BlockSpec((tm, d), lambda i: (i, 0)),
        compiler_params=_params(("parallel",)),
        name="ple",
    )(x, p, g, gf, wple, wgate)


def _odd_kernel(x_ref, g_ref, st_ref, win_ref, cw_ref, wout_ref, o_ref, newst_ref, hist_ref):
    sb, tt, d = x_ref.shape
    x = x_ref[...].reshape(sb * tt, d)
    xn = _rms(x, g_ref[...]).astype(BF16)
    proj = _dot(xn, win_ref[...])
    u = proj[:, 2 * d:] * proj[:, :d]
    bg = proj[:, d:2 * d]

    @pl.when(pl.program_id(1) == 0)
    def _():
        hist_ref[:, 6:8, :] = st_ref[...]

    hist_ref[:, 8:8 + tt, :] = u.reshape(sb, tt, d)
    cw = cw_ref[...]
    conv = (hist_ref[:, 6:6 + tt, :] * cw[0:1] + hist_ref[:, 7:7 + tt, :] * cw[1:2]
            + hist_ref[:, 8:8 + tt, :] * cw[2:3])
    y = (bg * conv.reshape(sb * tt, d)).astype(BF16)
    o_ref[...] = (x + _dot(y, wout_ref[...])).reshape(sb, tt, d)
    last = hist_ref[:, tt + 6:tt + 8, :]
    hist_ref[:, 6:8, :] = last
    newst_ref[...] = last


def _odd_mixer(x, g, st, win, cw, wout):
    b, t, d = x.shape
    tt = _row_tile(t, 512)
    sb = b if tt < 128 else 1
    return pl.pallas_call(
        _odd_kernel,
        out_shape=(jax.ShapeDtypeStruct((b, t, d), F32),
                   jax.ShapeDtypeStruct((b, CONV_C - 1, d), F32)),
        grid=(b // sb, t // tt),
        in_specs=[pl.BlockSpec((sb, tt, d), lambda i, j: (i, j, 0)),
                  _const_spec((1, d)),
                  pl.BlockSpec((sb, CONV_C - 1, d), lambda i, j: (i, 0, 0)),
                  _const_spec((d, 3 * d)), _const_spec((CONV_C, d)), _const_spec((d, d))],
        out_specs=(pl.BlockSpec((sb, tt, d), lambda i, j: (i, j, 0)),
                   pl.BlockSpec((sb, CONV_C - 1, d), lambda i, j: (i, 0, 0))),
        scratch_shapes=[pltpu.VMEM((sb, tt + 8, d), F32)],
        compiler_params=_params(("parallel", "arbitrary")),
        name="odd_mixer",
    )(x, g, st, win, cw, wout)


def _split_bf16(a):
    hi = a.astype(BF16)
    lo = (a - hi.astype(F32)).astype(BF16)
    return hi, lo


def _even_in_kernel(x_ref, g_ref, wbig_ref, wsm_ref, big_ref, sm_ref):
    xn = _rms(x_ref[...], g_ref[...])
    xh, xl = _split_bf16(xn)
    big_ref[...] = _dot(xh, wbig_ref[...])
    wh, wl = _split_bf16(wsm_ref[...])
    sm_ref[...] = _dot(xh, wh) + (_dot(xl, wh) + _dot(xh, wl))


def _even_in(x, g, wbig, wsm):
    n, d = x.shape
    nb = wbig.shape[1]
    tm = _row_tile(n, 512)
    return pl.pallas_call(
        _even_in_kernel,
        out_shape=(jax.ShapeDtypeStruct((n, nb), F32), jax.ShapeDtypeStruct((n, GATE_PAD), F32)),
        grid=(n // tm,),
        in_specs=[pl.BlockSpec((tm, d), lambda i: (i, 0)), _const_spec((1, d)),
                  _const_spec((d, nb)), _const_spec((d, GATE_PAD))],
        out_specs=(pl.BlockSpec((tm, nb), lambda i: (i, 0)), pl.BlockSpec((tm, GATE_PAD), lambda i: (i, 0))),
        compiler_params=_params(("parallel",)),
        name="even_in",
    )(x, g, wbig, wsm)


def _softplus_parts(y):
    t = jnp.log1p(jnp.exp(-jnp.abs(y)))
    return jnp.maximum(y, 0.0) + t, jnp.maximum(-y, 0.0) + t


def _even_seq_kernel(x_ref, big_ref, sm_ref, cst_ref, s0_ref, c0_ref, m0_ref, convw_ref, gp_ref, gnw_ref, mnw_ref,
                     lvl_ref, tri_ref, wout_ref,
                     o_ref, ncst_ref, s_ref, c_ref, m_ref, hist_ref, mix_ref, *, t_valid):
    L = SEQ_CHUNK

    @pl.when(pl.program_id(1) == 0)
    def _():
        hist_ref[5:8, :] = cst_ref[...]
        s_ref[...] = s0_ref[...]
        c_ref[...] = c0_ref[...]
        m_ref[...] = m0_ref[...]

    row = lax.broadcasted_iota(jnp.int32, (L, L), 0)
    col = lax.broadcasted_iota(jnp.int32, (L, L), 1)
    incl = row >= col
    col_ok = incl if t_valid == L else (incl & (col < t_valid))

    y = sm_ref[...] + gp_ref[0:1, :]
    sp_pos, sp_neg = _softplus_parts(y)
    gt = jnp.where(col < COL_G, jax.nn.sigmoid(y),
                   jnp.where(col < COL_LOGI, -jnp.exp(gp_ref[1:2, :]) * sp_pos,
                             jnp.where(col < COL_LOGF, y, -sp_neg)))
    gt = jnp.where(col < COL_LOGF + H_B, gt, 0.0)
    if t_valid < L:
        gt = jnp.where(row < t_valid, gt, 0.0)
    g1 = gt.astype(BF16)
    r1 = gt - g1.astype(F32)
    g2 = r1.astype(BF16)
    g3 = (r1 - g2.astype(F32)).astype(BF16)
    tri = tri_ref[...]
    cum = _dot(tri, g1) + (_dot(tri, g2) + _dot(tri, g3))
    gt_t = gt.T
    cum_t = cum.T

    hist_ref[8:8 + L, :] = big_ref[:, :QKV_A]

    def conv_silu(c0):
        acc = hist_ref[5:5 + L, c0:c0 + 128] * convw_ref[0:1, c0:c0 + 128]
        for j in range(1, CONV_A):
            acc = acc + hist_ref[5 + j:5 + j + L, c0:c0 + 128] * convw_ref[j:j + 1, c0:c0 + 128]
        return acc * jax.nn.sigmoid(acc)

    eye = jnp.where(row == col, 1.0, 0.0)
    one_col = jnp.where(col == 0, 1.0, 0.0)

    HA, HB = range(H_A), range(H_B)

    mq = [big_ref[:, OFF_QB + h * DK_B:OFF_QB + (h + 1) * DK_B] * (DK_B ** -0.5) for h in HB]
    mk_t = [big_ref[:, OFF_KB + h * DK_B:OFF_KB + (h + 1) * DK_B].T for h in HB]
    v_ext = [jnp.concatenate([big_ref[:, OFF_VB + h * DV_B:OFF_VB + (h + 1) * DV_B], one_col], axis=1) for h in HB]
    b_c = [cum[:, COL_LOGF + h:COL_LOGF + h + 1] for h in HB]
    dmat = [jnp.where(col_ok, b_c[h] - cum_t[COL_LOGF + h:COL_LOGF + h + 1, :]
                      + gt_t[COL_LOGI + h:COL_LOGI + h + 1, :], -jnp.inf) for h in HB]
    m_prev = [m_ref[h, 0:1, 0:1] for h in HB]
    m_t = [jnp.maximum(b_c[h] + m_prev[h], jnp.max(dmat[h], axis=-1, keepdims=True)) for h in HB]
    w_inter = [jnp.exp(b_c[h] + m_prev[h] - m_t[h]) for h in HB]
    w_intra = [jnp.exp(dmat[h] - m_t[h]) for h in HB]
    c_old = [c_ref[h] for h in HB]
    m_s = [_bdot(mq[h], mk_t[h]) for h in HB]
    m_qc = [_bdot(mq[h], c_old[h]) for h in HB]

    q = [conv_silu(h * DK_A) for h in HA]
    k = [conv_silu(H_A * DK_A + h * DK_A) for h in HA]
    v = [conv_silu(2 * H_A * DK_A + h * DV_A) for h in HA]
    q = [q[h] * lax.rsqrt(jnp.sum(q[h] * q[h], axis=-1, keepdims=True) + EPS) * (DK_A ** -0.5) for h in HA]
    k = [k[h] * lax.rsqrt(jnp.sum(k[h] * k[h], axis=-1, keepdims=True) + EPS) for h in HA]
    beta = [gt[:, COL_BETA + h:COL_BETA + h + 1] for h in HA]
    g_c = [cum[:, COL_G + h:COL_G + h + 1] for h in HA]
    g_r = [cum_t[COL_G + h:COL_G + h + 1, :] for h in HA]
    gam = [jnp.where(incl, jnp.exp(jnp.where(incl, g_c[h] - g_r[h], 0.0)), 0.0) for h in HA]
    k_t = [k[h].T for h in HA]
    qk_kk = [_bdot(jnp.concatenate([q[h], k[h]], axis=0), k_t[h]) for h in HA]
    a_s = [beta[h] * qk_kk[h][L:] * gam[h] for h in HA]
    d_inv = [eye - a_s[h] * lvl_ref[0] for h in HA]
    for lv in range(1, N_LEVELS):
        de = [_bdot(d_inv[h], a_s[h] * lvl_ref[lv]) for h in HA]
        if lv == 1:
            m_a = [m_s[h] * w_intra[h] for h in HB]
            m_av = [_bdot(m_a[h], v_ext[h]) for h in HB]
        if lv == 2:
            m_cu = [_bdot(mk_t[h] * w_intra[h][L - 1:L, :], v_ext[h]) for h in HB]
        d_inv = [d_inv[h] - _bdot(de[h], d_inv[h]) for h in HA]
    e_g = [jnp.exp(g_c[h]) for h in HA]
    sol = [_bdot(d_inv[h], jnp.concatenate([beta[h] * v[h], (beta[h] * e_g[h]) * k[h]], axis=1)) for h in HA]

    s_old = [s_ref[h] for h in HA]
    wq = [_bdot(jnp.concatenate([sol[h][:, DV_A:], q[h]], axis=0), s_old[h]) for h in HA]
    u = [sol[h][:, :DV_A] - wq[h][:L] for h in HA]
    o = [e_g[h] * wq[h][L:] + _bdot(qk_kk[h][:L] * gam[h], u[h]) for h in HA]
    g_last = [g_r[h][:, L - 1:L] for h in HA]
    s_new = [jnp.exp(g_last[h]) * s_old[h] + _bdot(k_t[h] * jnp.exp(g_last[h] - g_r[h]), u[h]) for h in HA]
    for h in HA:
        s_ref[h] = s_new[h]
        oh = o[h] * lax.rsqrt(jnp.mean(o[h] * o[h], axis=-1, keepdims=True) + EPS) * gnw_ref[...]
        z = big_ref[:, OFF_Z + h * DV_A:OFF_Z + (h + 1) * DV_A]
        mix_ref[:, h * DV_A:(h + 1) * DV_A] = (oh * (z * jax.nn.sigmoid(z))).astype(BF16)

    for h in HB:
        tot = w_inter[h] * m_qc[h] + m_av[h]
        den = tot[:, DV_B:DV_B + 1]
        hh = tot[:, :DV_B] / jnp.maximum(jnp.abs(den), jnp.exp(-m_t[h]))
        c_ref[h] = w_inter[h][L - 1:L, :] * c_old[h] + m_cu[h]
        m_ref[h] = jnp.broadcast_to(m_t[h][L - 1:L, :], m_ref.shape[1:])
        hh = hh * lax.rsqrt(jnp.mean(hh * hh, axis=-1, keepdims=True) + EPS)
        ob = big_ref[:, OFF_OB + h * DV_B:OFF_OB + (h + 1) * DV_B]
        hh = hh * mnw_ref[:, h * DV_B:(h + 1) * DV_B] * jax.nn.sigmoid(ob)
        mix_ref[:, H_A * DV_A + h * DV_B:H_A * DV_A + (h + 1) * DV_B] = hh.astype(BF16)

    o_ref[...] = x_ref[...] + _dot(mix_ref[...], wout_ref[...])

    last = hist_ref[5 + t_valid:8 + t_valid, :]
    hist_ref[5:8, :] = last
    ncst_ref[...] = last


def _level_masks():
    t = np.arange(SEQ_CHUNK)
    out = []
    for lv in range(N_LEVELS):
        b = 1 << lv
        tb, sb = t[:, None] // b, t[None, :] // b
        out.append(((tb % 2 == 1) & (sb == tb - 1)).astype(np.float32))
    return np.stack(out)


def _even_seq(x, big, sm, conv_st, S0, C0, n0, m0, conv_w, a_log, dt_bias, gdn_norm_w, b_i, b_f, mlstm_norm_w, wout):
    B, T, D = x.shape
    L = SEQ_CHUNK
    t_valid = L
    if T % L:
        assert T < L
        t_valid = T
        pad = lambda a: jnp.pad(a, ((0, 0), (0, L - T), (0, 0)))
        x, big, sm = pad(x), pad(big), pad(sm)
    Tp = x.shape[1]
    gp = jnp.zeros((8, GATE_PAD), F32)
    gp = gp.at[0, COL_G:COL_G + H_A].set(dt_bias).at[0, COL_LOGI:COL_LOGI + H_B].set(b_i)
    gp = gp.at[0, COL_LOGF:COL_LOGF + H_B].set(b_f).at[1, COL_G:COL_G + H_A].set(a_log)
    c_ext = jnp.concatenate([C0, n0[..., None], jnp.zeros(C0.shape[:-1] + (DV_B - 1,), F32)], axis=-1)
    m_b = jnp.broadcast_to(m0[:, :, None, None], (B, H_B, 8, 128))
    lvl = jnp.asarray(_level_masks())
    tri = jnp.asarray(np.tril(np.ones((L, L), np.float32)), BF16)
    tile = lambda w: pl.BlockSpec((None, L, w), lambda b, c: (b, c, 0))
    state = lambda *s: pl.BlockSpec((None,) + s, lambda b, c: (b,) + (0,) * len(s))
    out, ncst, S, c_new, m_new = pl.pallas_call(
        functools.partial(_even_seq_kernel, t_valid=t_valid),
        out_shape=(jax.ShapeDtypeStruct((B, Tp, D), F32),
                   jax.ShapeDtypeStruct((B, CONV_A - 1, QKV_A), F32),
                   jax.ShapeDtypeStruct((B, H_A, DK_A, DV_A), F32),
                   jax.ShapeDtypeStruct((B, H_B, DK_B, 2 * DV_B), F32),
                   jax.ShapeDtypeStruct((B, H_B, 8, 128), F32)),
        grid=(B, Tp // L),
        in_specs=[tile(D), tile(N_BIG), tile(GATE_PAD),
                  state(CONV_A - 1, QKV_A), state(H_A, DK_A, DV_A), state(H_B, DK_B, 2 * DV_B), state(H_B, 8, 128),
                  _const_spec((CONV_A, QKV_A)), _const_spec((8, GATE_PAD)), _const_spec((1, DV_A)),
                  _const_spec((1, H_B * DV_B)), _const_spec((N_LEVELS, L, L)), _const_spec((L, L)),
                  _const_spec((H_A * DV_A + H_B * DV_B, D))],
        out_specs=(tile(D), state(CONV_A - 1, QKV_A), state(H_A, DK_A, DV_A), state(H_B, DK_B, 2 * DV_B),
                   state(H_B, 8, 128)),
        scratch_shapes=[pltpu.VMEM((L + 8, QKV_A), F32), pltpu.VMEM((L, H_A * DV_A + H_B * DV_B), BF16)],
        compiler_params=_params(("parallel", "arbitrary")),
        name="even_seq",
    )(x, big, sm, conv_st, S0, c_ext, m_b, conv_w, gp, gdn_norm_w.reshape(1, DV_A),
      mlstm_norm_w.reshape(1, H_B * DV_B), lvl, tri, wout)
    return out[:, :T], (ncst, S, c_new[..., :DV_B], c_new[..., DV_B], m_new[:, :, 0, 0])


def _prep_weights(ffn_w_gu, ffn_w_d, w_ple, w_ple_gate, w_in_even, w_out_even, w_in_odd, w_out_odd):
    depth, _, d, two_ff = ffn_w_gu.shape
    dff = two_ff // 2
    nch = dff // FF_CHUNK
    wgu = ffn_w_gu.astype(BF16).reshape(depth, 2, d, 2, nch, FF_CHUNK)
    wgu = wgu.transpose(0, 1, 4, 2, 3, 5).reshape(depth, 2, nch, d, 2 * FF_CHUNK)
    nbig = QKV_A + H_A * DV_A
    nmid = nbig + 2 * H_A
    nb_end = nmid + 4 * H_B * DK_B
    w_big = jnp.concatenate([w_in_even[:, :, :nbig], w_in_even[:, :, nmid:nb_end]], axis=-1).astype(BF16)
    w_sm = jnp.concatenate([w_in_even[:, :, nbig:nmid], w_in_even[:, :, nb_end:]], axis=-1)
    w_sm = jnp.pad(w_sm, ((0, 0), (0, 0), (0, GATE_PAD - w_sm.shape[-1])))
    return dict(wgu=wgu, wd=ffn_w_d.astype(BF16), wple=w_ple.astype(BF16), wgate=w_ple_gate.astype(BF16),
                w_big=w_big, w_sm=w_sm, w_out_even=w_out_even.astype(BF16),
                w_in_odd=w_in_odd.astype(BF16), w_out_odd=w_out_odd.astype(BF16))


def _run_trunk(x, p, init_even, init_odd, W, norm_g, final_norm, gdn_conv_w, gdn_a_log, gdn_dt_bias,
               gdn_norm_w, mlstm_b_i, mlstm_b_f, mlstm_norm_w, conv_c_w):
    B, T, D = x.shape
    depth = norm_g.shape[0]
    n = B * T
    x = x.reshape(n, D)
    new_even, new_odd = [], []
    gf = final_norm.reshape(1, D)
    for i in range(depth):
        j = i // 2
        g = lambda k: norm_g[i, k].reshape(1, D)
        x = _ffn(x, g(0), W["wgu"][i, 0], W["wd"][i, 0])
        if i % 2 == 0:
            big, sm = _even_in(x, g(1), W["w_big"][j], W["w_sm"][j])
            y, st = _even_seq(x.reshape(B, T, D), big.reshape(B, T, -1), sm.reshape(B, T, -1), *init_even[j],
                              gdn_conv_w[j], gdn_a_log[j], gdn_dt_bias[j], gdn_norm_w[j], mlstm_b_i[j],
                              mlstm_b_f[j], mlstm_norm_w[j], W["w_out_even"][j])
            new_even.append(st)
        else:
            y, st = _odd_mixer(x.reshape(B, T, D), g(1), init_odd[j], W["w_in_odd"][j], conv_c_w[j],
                               W["w_out_odd"][j])
            new_odd.append(st)
        x = y.reshape(n, D)
        x = _ffn(x, g(2), W["wgu"][i, 1], W["wd"][i, 1])
        x = _ple(x, p[i].reshape(n, -1), g(3), gf, W["wple"][i], W["wgate"][i], final=(i == depth - 1))
    st_even = [jnp.stack([s[c] for s in new_even]) for c in range(5)]
    return (x.reshape(B, T, D), *st_even, jnp.stack(new_odd))


def kernel(x_prompt, x_sample, state_gdn_conv, state_gdn_S, state_mlstm_C, state_mlstm_n, state_mlstm_m,
           state_conv, p_prompt, p_sample, norm_g, final_norm, ffn_w_gu, ffn_w_d, w_ple, w_ple_gate,
           w_in_even, gdn_conv_w, gdn_a_log, gdn_dt_bias, gdn_norm_w, mlstm_b_i, mlstm_b_f, mlstm_norm_w,
           w_out_even, w_in_odd, conv_c_w, w_out_odd):
    Bp = x_prompt.shape[0]
    n_even, n_odd = state_gdn_S.shape[0], state_conv.shape[0]
    D = x_prompt.shape[-1]
    W = _prep_weights(ffn_w_gu, ffn_w_d, w_ple, w_ple_gate, w_in_even, w_out_even, w_in_odd, w_out_odd)
    zero_even = (jnp.zeros((Bp, CONV_A - 1, QKV_A), F32), jnp.zeros((Bp, H_A, DK_A, DV_A), F32),
                 jnp.zeros((Bp, H_B, DK_B, DV_B), F32), jnp.zeros((Bp, H_B, DK_B), F32),
                 jnp.zeros((Bp, H_B), F32))
    init_even_p = [zero_even] * n_even
    init_odd_p = [jnp.zeros((Bp, CONV_C - 1, D), F32)] * n_odd
    init_even_s = [(state_gdn_conv[j], state_gdn_S[j], state_mlstm_C[j], state_mlstm_n[j], state_mlstm_m[j])
                   for j in range(n_even)]
    init_odd_s = [state_conv[j] for j in range(n_odd)]
    rest = (norm_g, final_norm, gdn_conv_w, gdn_a_log, gdn_dt_bias, gdn_norm_w, mlstm_b_i, mlstm_b_f,
            mlstm_norm_w, conv_c_w)
    outs_p = _run_trunk(x_prompt, p_prompt, init_even_p, init_odd_p, W, *rest)
    outs_s = _run_trunk(x_sample, p_sample, init_even_s, init_odd_s, W, *rest)
    return (outs_p[0], outs_s[0]) + tuple(outs_p[1:]) + tuple(outs_s[1:])
```

```python
import functools

import jax
import jax.numpy as jnp
import numpy as np
from jax import lax
from jax.experimental import pallas as pl
from jax.experimental.pallas import tpu as pltpu

F32 = jnp.float32
BF16 = jnp.bfloat16

EPS = 1e-6
H_A, DK_A, DV_A, CONV_A = 4, 128, 128, 4
H_B, DK_B, DV_B = 4, 128, 128
CONV_C = 3
QKV_A = H_A * (2 * DK_A + DV_A)
FF_CHUNK = 256
GATE_PAD = 128
SEQ_CHUNK = 128
N_LEVELS = 7
VMEM_LIMIT = 56 * 1024 * 1024

OFF_Z = QKV_A
OFF_QB = OFF_Z + H_A * DV_A
OFF_KB = OFF_QB + H_B * DK_B
OFF_VB = OFF_KB + H_B * DK_B
OFF_OB = OFF_VB + H_B * DV_B
N_BIG = OFF_OB + H_B * DV_B
COL_BETA, COL_G, COL_LOGI, COL_LOGF = 0, H_A, 2 * H_A, 2 * H_A + H_B


def _rms(x, g):
    ms = jnp.mean(x * x, axis=-1, keepdims=True)
    return x * lax.rsqrt(ms + EPS) * g


def _dot(a, b):
    return jnp.dot(a, b, preferred_element_type=F32)


def _bdot(a, b):
    return jnp.dot(a.astype(BF16), b.astype(BF16), preferred_element_type=F32)


def _const_spec(shape):
    n = len(shape)
    return pl.BlockSpec(shape, lambda *_: (0,) * n, pipeline_mode=pl.Buffered(1))


def _params(sem):
    return pltpu.CompilerParams(dimension_semantics=sem, vmem_limit_bytes=VMEM_LIMIT)


def _row_tile(n, want):
    t = min(n, want)
    assert n % t == 0, (n, t)
    return t


def _ffn_kernel(x_ref, g_ref, wgu_ref, wd_ref, o_ref, act_ref):
    dff = wd_ref.shape[0]
    x = x_ref[...]
    xn = _rms(x, g_ref[...]).astype(BF16)
    for c in range(dff // FF_CHUNK):
        lo = c * FF_CHUNK
        a = _dot(xn, wgu_ref[:, lo:lo + FF_CHUNK])
        b = _dot(xn, wgu_ref[:, dff + lo:dff + lo + FF_CHUNK])
        act_ref[:, lo:lo + FF_CHUNK] = (a * jax.nn.sigmoid(a) * b).astype(BF16)
    o_ref[...] = x + 0.5 * _dot(act_ref[...], wd_ref[...])


def _ffn(x, g, wgu, wd):
    n, d = x.shape
    dff = wd.shape[0]
    tm = _row_tile(n, 512)
    return pl.pallas_call(
        _ffn_kernel,
        out_shape=jax.ShapeDtypeStruct((n, d), F32),
        grid=(n // tm,),
        in_specs=[pl.BlockSpec((tm, d), lambda i: (i, 0)),
                  _const_spec((1, d)),
                  _const_spec((d, 2 * dff)),
                  _const_spec((dff, d))],
        out_specs=pl.BlockSpec((tm, d), lambda i: (i, 0)),
        scratch_shapes=[pltpu.VMEM((tm, dff), BF16)],
        compiler_params=_params(("parallel",)),
        name="ffn",
    )(x, g, wgu, wd)


def _ple_kernel(x_ref, p_ref, g_ref, gf_ref, wple_ref, wgate_ref, o_ref, *, final):
    x = x_ref[...]
    xg = _rms(x, g_ref[...]).astype(BF16)
    gate = jax.nn.sigmoid(_dot(xg, wgate_ref[...]))
    y = x + _dot(p_ref[...].astype(BF16), wple_ref[...]) * gate
    if final:
        y = _rms(y, gf_ref[...])
    o_ref[...] = y


def _ple(x, p, g, gf, wple, wgate, final):
    n, d = x.shape
    dp = p.shape[1]
    tm = _row_tile(n, 1024)
    return pl.pallas_call(
        functools.partial(_ple_kernel, final=final),
        out_shape=jax.ShapeDtypeStruct((n, d), F32),
        grid=(n // tm,),
        in_specs=[pl.BlockSpec((tm, d), lambda i: (i, 0)),
                  pl.BlockSpec((tm, dp), lambda i: (i, 0)),
                  _const_spec((1, d)), _const_spec((1, d)),
                  _const_spec((dp, d)), _const_spec((d, d))],
        out_specs=pl.BlockSpec((tm, d), lambda i: (i, 0)),
        compiler_params=_params(("parallel",)),
        name="ple",
    )(x, p, g, gf, wple, wgate)


def _odd_kernel(x_ref, g_ref, st_ref, win_ref, cw_ref, wout_ref, o_ref, newst_ref, hist_ref):
    sb, tt, d = x_ref.shape
    x = x_ref[...].reshape(sb * tt, d)
    xn = _rms(x, g_ref[...]).astype(BF16)
    proj = _dot(xn, win_ref[...])
    u = proj[:, 2 * d:] * proj[:, :d]
    bg = proj[:, d:2 * d]

    @pl.when(pl.program_id(1) == 0)
    def _():
        hist_ref[:, 6:8, :] = st_ref[...]

    hist_ref[:, 8:8 + tt, :] = u.reshape(sb, tt, d)
    cw = cw_ref[...]
    conv = (hist_ref[:, 6:6 + tt, :] * cw[0:1] + hist_ref[:, 7:7 + tt, :] * cw[1:2]
            + hist_ref[:, 8:8 + tt, :] * cw[2:3])
    y = (bg * conv.reshape(sb * tt, d)).astype(BF16)
    o_ref[...] = (x + _dot(y, wout_ref[...])).reshape(sb, tt, d)
    last = hist_ref[:, tt + 6:tt + 8, :]
    hist_ref[:, 6:8, :] = last
    newst_ref[...] = last


def _odd_mixer(x, g, st, win, cw, wout):
    b, t, d = x.shape
    tt = _row_tile(t, 512)
    sb = b if tt < 128 else 1
    return pl.pallas_call(
        _odd_kernel,
        out_shape=(jax.ShapeDtypeStruct((b, t, d), F32),
                   jax.ShapeDtypeStruct((b, CONV_C - 1, d), F32)),
        grid=(b // sb, t // tt),
        in_specs=[pl.BlockSpec((sb, tt, d), lambda i, j: (i, j, 0)),
                  _const_spec((1, d)),
                  pl.BlockSpec((sb, CONV_C - 1, d), lambda i, j: (i, 0, 0)),
                  _const_spec((d, 3 * d)), _const_spec((CONV_C, d)), _const_spec((d, d))],
        out_specs=(pl.BlockSpec((sb, tt, d), lambda i, j: (i, j, 0)),
                   pl.BlockSpec((sb, CONV_C - 1, d), lambda i, j: (i, 0, 0))),
        scratch_shapes=[pltpu.VMEM((sb, tt + 8, d), F32)],
        compiler_params=_params(("parallel", "arbitrary")),
        name="odd_mixer",
    )(x, g, st, win, cw, wout)


def _split_bf16(a):
    hi = a.astype(BF16)
    lo = (a - hi.astype(F32)).astype(BF16)
    return hi, lo


def _softplus_parts(y):
    t = jnp.log1p(jnp.exp(-jnp.abs(y)))
    return jnp.maximum(y, 0.0) + t, jnp.maximum(-y, 0.0) + t


def _even_in_kernel(x_ref, g_ref, cst_ref, wbig_ref, wsm_ref, convw_ref, gp_ref, tri_ref,
                    big_ref, gt_ref, cum_ref, ncst_ref, hist_ref):
    tm = x_ref.shape[0]

    @pl.when(pl.program_id(1) == 0)
    def _():
        hist_ref[5:8, :] = cst_ref[...]

    xn = _rms(x_ref[...], g_ref[...])
    xh, xl = _split_bf16(xn)
    hist_ref[8:8 + tm, :] = _dot(xh, wbig_ref[:, :OFF_Z])
    big_ref[:, OFF_Z:OFF_QB] = _dot(xh, wbig_ref[:, OFF_Z:OFF_QB])
    big_ref[:, OFF_QB:OFF_KB] = _dot(xh, wbig_ref[:, OFF_QB:OFF_KB]) * (DK_B ** -0.5)
    big_ref[:, OFF_KB:] = _dot(xh, wbig_ref[:, OFF_KB:])

    wh, wl = _split_bf16(wsm_ref[...])
    y = _dot(xh, wh) + (_dot(xl, wh) + _dot(xh, wl)) + gp_ref[0:1, :]
    col = lax.broadcasted_iota(jnp.int32, y.shape, 1)
    sp_pos, sp_neg = _softplus_parts(y)
    gt = jnp.where(col < COL_G, jax.nn.sigmoid(y),
                   jnp.where(col < COL_LOGI, -jnp.exp(gp_ref[1:2, :]) * sp_pos,
                             jnp.where(col < COL_LOGF, y, -sp_neg)))
    gt = jnp.where(col < COL_LOGF + H_B, gt, 0.0)
    gt_ref[...] = gt
    g1 = gt.astype(BF16)
    r1 = gt - g1.astype(F32)
    g2 = r1.astype(BF16)
    g3 = (r1 - g2.astype(F32)).astype(BF16)
    tri = tri_ref[...]
    cum_ref[...] = _dot(tri, g1) + (_dot(tri, g2) + _dot(tri, g3))

    for blk in range(QKV_A // 128):
        c0 = blk * 128
        acc = hist_ref[5:5 + tm, c0:c0 + 128] * convw_ref[0:1, c0:c0 + 128]
        for j in range(1, CONV_A):
            acc = acc + hist_ref[5 + j:5 + j + tm, c0:c0 + 128] * convw_ref[j:j + 1, c0:c0 + 128]
        v = acc * jax.nn.sigmoid(acc)
        if blk < 2 * H_A:
            v = v * lax.rsqrt(jnp.sum(v * v, axis=-1, keepdims=True) + EPS)
        if blk < H_A:
            v = v * (DK_A ** -0.5)
        big_ref[:, c0:c0 + 128] = v

    last = hist_ref[tm + 5:tm + 8, :]
    hist_ref[5:8, :] = last
    ncst_ref[...] = last


def _chunk_tri(tm):
    t = np.arange(tm)
    same = (t[:, None] // SEQ_CHUNK) == (t[None, :] // SEQ_CHUNK)
    return (same & (t[:, None] >= t[None, :])).astype(np.float32)


def _even_in(x, g, conv_st, wbig, wsm, conv_w, gp):
    b, t, d = x.shape
    tm = _row_tile(t, 512)
    assert tm % SEQ_CHUNK == 0 or tm == t < SEQ_CHUNK
    tri = jnp.asarray(_chunk_tri(tm), BF16)
    tile = lambda w: pl.BlockSpec((None, tm, w), lambda i, j: (i, j, 0))
    cst = pl.BlockSpec((None, CONV_A - 1, QKV_A), lambda i, j: (i, 0, 0))
    return pl.pallas_call(
        _even_in_kernel,
        out_shape=(jax.ShapeDtypeStruct((b, t, N_BIG), F32), jax.ShapeDtypeStruct((b, t, GATE_PAD), F32),
                   jax.ShapeDtypeStruct((b, t, GATE_PAD), F32), jax.ShapeDtypeStruct((b, CONV_A - 1, QKV_A), F32)),
        grid=(b, t // tm),
        in_specs=[tile(d), _const_spec((1, d)), cst, _const_spec((d, N_BIG)), _const_spec((d, GATE_PAD)),
                  _const_spec((CONV_A, QKV_A)), _const_spec((8, GATE_PAD)), _const_spec((tm, tm))],
        out_specs=(tile(N_BIG), tile(GATE_PAD), tile(GATE_PAD), cst),
        scratch_shapes=[pltpu.VMEM((tm + 8, QKV_A), F32)],
        compiler_params=_params(("parallel", "arbitrary")),
        name="even_in",
    )(x, g, conv_st, wbig, wsm, conv_w, gp, tri)


def _even_seq_kernel(x_ref, big_ref, gt_ref, cum_ref, s0_ref, c0_ref, m0_ref, gnw_ref, mnw_ref, lvl_ref, wout_ref,
                     o_ref, s_ref, c_ref, m_ref, mix_ref, *, t_valid):
    L = SEQ_CHUNK
    nc = x_ref.shape[0] // L
    HA, HB, CH = range(H_A), range(H_B), range(nc)

    @pl.when(pl.program_id(1) == 0)
    def _():
        s_ref[...] = s0_ref[...]
        c_ref[...] = c0_ref[...]
        m_ref[...] = m0_ref[...]

    row = lax.broadcasted_iota(jnp.int32, (L, L), 0)
    col = lax.broadcasted_iota(jnp.int32, (L, L), 1)
    incl = row >= col
    col_ok = incl if t_valid == L else (incl & (col < t_valid))
    eye = jnp.where(row == col, 1.0, 0.0)
    one_col = jnp.where(col == 0, 1.0, 0.0)

    def blk(c, off, h):
        return big_ref[c * L:(c + 1) * L, off + h * 128:off + (h + 1) * 128]

    def mlstm_local(c):
        cum_c = cum_ref[c * L:(c + 1) * L, :]
        cum_t, gt_t = cum_c.T, gt_ref[c * L:(c + 1) * L, :].T
        p = dict(q=[blk(c, OFF_QB, h) for h in HB], k_t=[blk(c, OFF_KB, h).T for h in HB],
                 v_ext=[jnp.concatenate([blk(c, OFF_VB, h), one_col], axis=1) for h in HB],
                 b_c=[cum_c[:, COL_LOGF + h:COL_LOGF + h + 1] for h in HB])
        p["dmat"] = [jnp.where(col_ok, p["b_c"][h] - cum_t[COL_LOGF + h:COL_LOGF + h + 1, :]
                               + gt_t[COL_LOGI + h:COL_LOGI + h + 1, :], -jnp.inf) for h in HB]
        p["dmax"] = [jnp.max(p["dmat"][h], axis=-1, keepdims=True) for h in HB]
        p["s"] = [_bdot(p["q"][h], p["k_t"][h]) for h in HB]
        return p

    def mlstm_state(p, m_prev, c_cur):
        m_t = [jnp.maximum(p["b_c"][h] + m_prev[h], p["dmax"][h]) for h in HB]
        w_inter = [jnp.exp(p["b_c"][h] + m_prev[h] - m_t[h]) for h in HB]
        w_intra = [jnp.exp(p["dmat"][h] - m_t[h]) for h in HB]
        av = [_bdot(p["s"][h] * w_intra[h], p["v_ext"][h]) for h in HB]
        qc = [_bdot(p["q"][h], c_cur[h]) for h in HB]
        cu = [_bdot(p["k_t"][h] * w_intra[h][L - 1:L, :], p["v_ext"][h]) for h in HB]
        tot = [w_inter[h] * qc[h] + av[h] for h in HB]
        out = [tot[h][:, :DV_B] / jnp.maximum(jnp.abs(tot[h][:, DV_B:DV_B + 1]), jnp.exp(-m_t[h])) for h in HB]
        c_new = [w_inter[h][L - 1:L, :] * c_cur[h] + cu[h] for h in HB]
        return out, [m_t[h][L - 1:L, :] for h in HB], c_new

    def gdn_local(c):
        gt_c, cum_c = gt_ref[c * L:(c + 1) * L, :], cum_ref[c * L:(c + 1) * L, :]
        cum_t = cum_c.T
        p = dict(q=[blk(c, 0, h) for h in HA], k=[blk(c, H_A * DK_A, h) for h in HA],
                 v=[blk(c, 2 * H_A * DK_A, h) for h in HA],
                 beta=[gt_c[:, COL_BETA + h:COL_BETA + h + 1] for h in HA],
                 g_c=[cum_c[:, COL_G + h:COL_G + h + 1] for h in HA],
                 g_r=[cum_t[COL_G + h:COL_G + h + 1, :] for h in HA])
        p["gam"] = [jnp.where(incl, jnp.exp(jnp.where(incl, p["g_c"][h] - p["g_r"][h], 0.0)), 0.0) for h in HA]
        p["k_t"] = [p["k"][h].T for h in HA]
        p["qk_kk"] = [_bdot(jnp.concatenate([p["q"][h], p["k"][h]], axis=0), p["k_t"][h]) for h in HA]
        p["a_s"] = [p["beta"][h] * p["qk_kk"][h][L:] * p["gam"][h] for h in HA]
        p["d"] = [eye - p["a_s"][h] * lvl_ref[0] for h in HA]
        return p

    def gdn_solve(p):
        p["e_g"] = [jnp.exp(p["g_c"][h]) for h in HA]
        p["sol"] = [_bdot(p["d"][h], jnp.concatenate([p["beta"][h] * p["v"][h],
                                                      (p["beta"][h] * p["e_g"][h]) * p["k"][h]], axis=1)) for h in HA]

    def gdn_state_a(p, s_cur):
        wq = [_bdot(jnp.concatenate([p["sol"][h][:, DV_A:], p["q"][h]], axis=0), s_cur[h]) for h in HA]
        p["u"] = [p["sol"][h][:, :DV_A] - wq[h][:L] for h in HA]
        p["qs"] = [wq[h][L:] for h in HA]

    def gdn_state_b(p, s_cur):
        out = [p["e_g"][h] * p["qs"][h] + _bdot(p["qk_kk"][h][:L] * p["gam"][h], p["u"][h]) for h in HA]
        g_last = [p["g_r"][h][:, L - 1:L] for h in HA]
        s_new = [jnp.exp(g_last[h]) * s_cur[h] + _bdot(p["k_t"][h] * jnp.exp(g_last[h] - p["g_r"][h]), p["u"][h])
                 for h in HA]
        return out, s_new

    m_prev = [m_ref[h, 0:1, 0:1] for h in HB]
    c_cur = [c_ref[h] for h in HB]
    s_cur = [s_ref[h] for h in HA]
    m_out, g_out = [None] * nc, [None] * nc
    prev = []
    for g0 in range(0, nc, 2):
        cs = list(range(g0, min(g0 + 2, nc)))
        ml = [mlstm_local(c) for c in cs]
        gd = [gdn_local(c) for c in cs]
        riders = {1 + i: ("mlstm", ml[i], c) for i, c in enumerate(cs)}
        for i, (c, p) in enumerate(prev):
            riders[3 + 2 * i], riders[4 + 2 * i] = ("gdn_a", p, c), ("gdn_b", p, c)
        for lv in range(1, N_LEVELS):
            de = [[_bdot(p["d"][h], p["a_s"][h] * lvl_ref[lv]) for h in HA] for p in gd]
            kind, arg, c = riders.get(lv, (None, None, None))
            if kind == "mlstm":
                m_out[c], m_prev, c_cur = mlstm_state(arg, m_prev, c_cur)
            elif kind == "gdn_a":
                gdn_state_a(arg, s_cur)
            elif kind == "gdn_b":
                g_out[c], s_cur = gdn_state_b(arg, s_cur)
            for p, de_p in zip(gd, de):
                p["d"] = [p["d"][h] - _bdot(de_p[h], p["d"][h]) for h in HA]
        for p in gd:
            gdn_solve(p)
        prev = list(zip(cs, gd))
    for c, p in prev:
        gdn_state_a(p, s_cur)
        g_out[c], s_cur = gdn_state_b(p, s_cur)

    for h in HA:
        s_ref[h] = s_cur[h]
    for h in HB:
        c_ref[h] = c_cur[h]
        m_ref[h] = jnp.broadcast_to(m_prev[h], m_ref.shape[1:])

    for c in CH:
        for h in HA:
            o = g_out[c][h]
            o = o * lax.rsqrt(jnp.mean(o * o, axis=-1, keepdims=True) + EPS) * gnw_ref[...]
            z = blk(c, OFF_Z, h)
            mix_ref[c * L:(c + 1) * L, h * DV_A:(h + 1) * DV_A] = (o * (z * jax.nn.sigmoid(z))).astype(BF16)
        for h in HB:
            hh = m_out[c][h]
            hh = hh * lax.rsqrt(jnp.mean(hh * hh, axis=-1, keepdims=True) + EPS)
            hh = hh * mnw_ref[:, h * DV_B:(h + 1) * DV_B] * jax.nn.sigmoid(blk(c, OFF_OB, h))
            c0 = H_A * DV_A + h * DV_B
            mix_ref[c * L:(c + 1) * L, c0:c0 + DV_B] = hh.astype(BF16)
    o_ref[...] = x_ref[...] + _dot(mix_ref[...], wout_ref[...])


def _level_masks():
    t = np.arange(SEQ_CHUNK)
    out = []
    for lv in range(N_LEVELS):
        b = 1 << lv
        tb, sb = t[:, None] // b, t[None, :] // b
        out.append(((tb % 2 == 1) & (sb == tb - 1)).astype(np.float32))
    return np.stack(out)


def _even_seq(x, big, gt, cum, S0, C0, n0, m0, gdn_norm_w, mlstm_norm_w, wout):
    B, T, D = x.shape
    L = SEQ_CHUNK
    t_valid = L
    if T % L:
        assert T < L
        t_valid = T
        pad = lambda a, mode: jnp.pad(a, ((0, 0), (0, L - T), (0, 0)), mode=mode)
        x, big, gt, cum = pad(x, "constant"), pad(big, "constant"), pad(gt, "constant"), pad(cum, "edge")
    Tp = x.shape[1]
    tt = min(Tp, 4 * L)
    c_ext = jnp.concatenate([C0, n0[..., None], jnp.zeros(C0.shape[:-1] + (DV_B - 1,), F32)], axis=-1)
    m_b = jnp.broadcast_to(m0[:, :, None, None], (B, H_B, 8, 128))
    lvl = jnp.asarray(_level_masks())
    tile = lambda w: pl.BlockSpec((None, tt, w), lambda b, c: (b, c, 0))
    state = lambda *s: pl.BlockSpec((None,) + s, lambda b, c: (b,) + (0,) * len(s))
    dmix = H_A * DV_A + H_B * DV_B
    out, S, c_new, m_new = pl.pallas_call(
        functools.partial(_even_seq_kernel, t_valid=t_valid),
        out_shape=(jax.ShapeDtypeStruct((B, Tp, D), F32),
                   jax.ShapeDtypeStruct((B, H_A, DK_A, DV_A), F32),
                   jax.ShapeDtypeStruct((B, H_B, DK_B, 2 * DV_B), F32),
                   jax.ShapeDtypeStruct((B, H_B, 8, 128), F32)),
        grid=(B, Tp // tt),
        in_specs=[tile(D), tile(N_BIG), tile(GATE_PAD), tile(GATE_PAD),
                  state(H_A, DK_A, DV_A), state(H_B, DK_B, 2 * DV_B), state(H_B, 8, 128),
                  _const_spec((1, DV_A)), _const_spec((1, H_B * DV_B)), _const_spec((N_LEVELS, L, L)),
                  _const_spec((dmix, D))],
        out_specs=(tile(D), state(H_A, DK_A, DV_A), state(H_B, DK_B, 2 * DV_B), state(H_B, 8, 128)),
        scratch_shapes=[pltpu.VMEM((tt, dmix), BF16)],
        compiler_params=_params(("parallel", "arbitrary")),
        name="even_seq",
    )(x, big, gt, cum, S0, c_ext, m_b, gdn_norm_w.reshape(1, DV_A), mlstm_norm_w.reshape(1, H_B * DV_B), lvl, wout)
    return out[:, :T], (S, c_new[..., :DV_B], c_new[..., DV_B], m_new[:, :, 0, 0])


def _prep_weights(ffn_w_gu, ffn_w_d, w_ple, w_ple_gate, w_in_even, w_out_even, w_in_odd, w_out_odd):
    nbig = QKV_A + H_A * DV_A
    nmid = nbig + 2 * H_A
    nb_end = nmid + 4 * H_B * DK_B
    w_big = jnp.concatenate([w_in_even[:, :, :nbig], w_in_even[:, :, nmid:nb_end]], axis=-1).astype(BF16)
    w_sm = jnp.concatenate([w_in_even[:, :, nbig:nmid], w_in_even[:, :, nb_end:]], axis=-1)
    w_sm = jnp.pad(w_sm, ((0, 0), (0, 0), (0, GATE_PAD - w_sm.shape[-1])))
    return dict(wgu=ffn_w_gu.astype(BF16), wd=ffn_w_d.astype(BF16), wple=w_ple.astype(BF16),
                wgate=w_ple_gate.astype(BF16), w_big=w_big, w_sm=w_sm, w_out_even=w_out_even.astype(BF16),
                w_in_odd=w_in_odd.astype(BF16), w_out_odd=w_out_odd.astype(BF16))


def _gate_params(a_log, dt_bias, b_i, b_f):
    gp = jnp.zeros((8, GATE_PAD), F32)
    gp = gp.at[0, COL_G:COL_G + H_A].set(dt_bias).at[0, COL_LOGI:COL_LOGI + H_B].set(b_i)
    return gp.at[0, COL_LOGF:COL_LOGF + H_B].set(b_f).at[1, COL_G:COL_G + H_A].set(a_log)


def _run_trunk(x, p, init_even, init_odd, W, norm_g, final_norm, gdn_conv_w, gdn_a_log, gdn_dt_bias,
               gdn_norm_w, mlstm_b_i, mlstm_b_f, mlstm_norm_w, conv_c_w):
    B, T, D = x.shape
    depth = norm_g.shape[0]
    n = B * T
    x = x.reshape(n, D)
    new_even, new_odd = [], []
    gf = final_norm.reshape(1, D)
    for i in range(depth):
        j = i // 2
        g = lambda k: norm_g[i, k].reshape(1, D)
        x = _ffn(x, g(0), W["wgu"][i, 0], W["wd"][i, 0])
        if i % 2 == 0:
            conv_st, S0, C0, n0, m0 = init_even[j]
            gp = _gate_params(gdn_a_log[j], gdn_dt_bias[j], mlstm_b_i[j], mlstm_b_f[j])
            big, gt, cum, ncst = _even_in(x.reshape(B, T, D), g(1), conv_st, W["w_big"][j], W["w_sm"][j],
                                          gdn_conv_w[j], gp)
            y, st = _even_seq(x.reshape(B, T, D), big, gt, cum, S0, C0, n0, m0, gdn_norm_w[j], mlstm_norm_w[j],
                              W["w_out_even"][j])
            new_even.append((ncst,) + st)
        else:
            y, st = _odd_mixer(x.reshape(B, T, D), g(1), init_odd[j], W["w_in_odd"][j], conv_c_w[j],
                               W["w_out_odd"][j])
            new_odd.append(st)
        x = y.reshape(n, D)
        x = _ffn(x, g(2), W["wgu"][i, 1], W["wd"][i, 1])
        x = _ple(x, p[i].reshape(n, -1), g(3), gf, W["wple"][i], W["wgate"][i], final=(i == depth - 1))
    st_even = [jnp.stack([s[c] for s in new_even]) for c in range(5)]
    return (x.reshape(B, T, D), *st_even, jnp.stack(new_odd))


def kernel(x_prompt, x_sample, state_gdn_conv, state_gdn_S, state_mlstm_C, state_mlstm_n, state_mlstm_m,
           state_conv, p_prompt, p_sample, norm_g, final_norm, ffn_w_gu, ffn_w_d, w_ple, w_ple_gate,
           w_in_even, gdn_conv_w, gdn_a_log, gdn_dt_bias, gdn_norm_w, mlstm_b_i, mlstm_b_f, mlstm_norm_w,
           w_out_even, w_in_odd, conv_c_w, w_out_odd):
    Bp = x_prompt.shape[0]
    n_even, n_odd = state_gdn_S.shape[0], state_conv.shape[0]
    D = x_prompt.shape[-1]
    W = _prep_weights(ffn_w_gu, ffn_w_d, w_ple, w_ple_gate, w_in_even, w_out_even, w_in_odd, w_out_odd)
    zero_even = (jnp.zeros((Bp, CONV_A - 1, QKV_A), F32), jnp.zeros((Bp, H_A, DK_A, DV_A), F32),
                 jnp.zeros((Bp, H_B, DK_B, DV_B), F32), jnp.zeros((Bp, H_B, DK_B), F32),
                 jnp.zeros((Bp, H_B), F32))
    init_even_p = [zero_even] * n_even
    init_odd_p = [jnp.zeros((Bp, CONV_C - 1, D), F32)] * n_odd
    init_even_s = [(state_gdn_conv[j], state_gdn_S[j], state_mlstm_C[j], state_mlstm_n[j], state_mlstm_m[j])
                   for j in range(n_even)]
    init_odd_s = [state_conv[j] for j in range(n_odd)]
    rest = (norm_g, final_norm, gdn_conv_w, gdn_a_log, gdn_dt_bias, gdn_norm_w, mlstm_b_i, mlstm_b_f,
            mlstm_norm_w, conv_c_w)
    outs_p = _run_trunk(x_prompt, p_prompt, init_even_p, init_odd_p, W, *rest)
    outs_s = _run_trunk(x_sample, p_sample, init_even_s, init_odd_s, W, *rest)
    return (outs_p[0], outs_s[0]) + tuple(outs_p[1:]) + tuple(outs_s[1:])
```

```python
import functools

import jax
import jax.numpy as jnp
import numpy as np
from jax import lax
from jax.experimental import pallas as pl
from jax.experimental.pallas import tpu as pltpu

F32 = jnp.float32
BF16 = jnp.bfloat16

EPS = 1e-6
H_A, DK_A, DV_A, CONV_A = 4, 128, 128, 4
H_B, DK_B, DV_B = 4, 128, 128
CONV_C = 3
QKV_A = H_A * (2 * DK_A + DV_A)
FF_CHUNK = 256
GATE_PAD = 128
SEQ_CHUNK = 128
N_LEVELS = 7
VMEM_LIMIT = 56 * 1024 * 1024

OFF_Z = QKV_A
OFF_QB = OFF_Z + H_A * DV_A
OFF_KB = OFF_QB + H_B * DK_B
OFF_VB = OFF_KB + H_B * DK_B
OFF_OB = OFF_VB + H_B * DV_B
N_BIG = OFF_OB + H_B * DV_B
COL_BETA, COL_G, COL_LOGI, COL_LOGF = 0, H_A, 2 * H_A, 2 * H_A + H_B


def _rms(x, g):
    ms = jnp.mean(x * x, axis=-1, keepdims=True)
    return x * lax.rsqrt(ms + EPS) * g


def _dot(a, b):
    return jnp.dot(a, b, preferred_element_type=F32)


def _bdot(a, b):
    return jnp.dot(a.astype(BF16), b.astype(BF16), preferred_element_type=F32)


def _const_spec(shape):
    n = len(shape)
    return pl.BlockSpec(shape, lambda *_: (0,) * n, pipeline_mode=pl.Buffered(1))


def _sel_spec(shape, idx):
    n = len(shape)
    return pl.BlockSpec((None,) * len(idx) + tuple(shape), lambda *_: tuple(idx) + (0,) * n,
                        pipeline_mode=pl.Buffered(1))


def _params(sem):
    return pltpu.CompilerParams(dimension_semantics=sem, vmem_limit_bytes=VMEM_LIMIT)


def _row_tile(n, want):
    t = min(n, want)
    assert n % t == 0, (n, t)
    return t


def _swiglu_residual(x, g_ref, wgu_ref, wd_ref, act_ref):
    dff = wd_ref.shape[0]
    xn = _rms(x, g_ref[...]).astype(BF16)
    for c in range(dff // FF_CHUNK):
        lo = c * FF_CHUNK
        a = _dot(xn, wgu_ref[:, lo:lo + FF_CHUNK])
        b = _dot(xn, wgu_ref[:, dff + lo:dff + lo + FF_CHUNK])
        act_ref[:, lo:lo + FF_CHUNK] = (a * jax.nn.sigmoid(a) * b).astype(BF16)
    return x + 0.5 * _dot(act_ref[...], wd_ref[...])


def _ffn_kernel(x_ref, g_ref, wgu_ref, wd_ref, o_ref, act_ref):
    o_ref[...] = _swiglu_residual(x_ref[...], g_ref, wgu_ref, wd_ref, act_ref)


def _ffn_ple_kernel(x_ref, g_ref, wgu_ref, wd_ref, p_ref, gp_ref, gf_ref, wple_ref, wgate_ref, o_ref, act_ref, *,
                    final):
    y = _swiglu_residual(x_ref[...], g_ref, wgu_ref, wd_ref, act_ref)
    gate = jax.nn.sigmoid(_dot(_rms(y, gp_ref[...]).astype(BF16), wgate_ref[...]))
    y = y + _dot(p_ref[...].astype(BF16), wple_ref[...]) * gate
    if final:
        y = _rms(y, gf_ref[...])
    o_ref[...] = y


def _ffn(x, g, wgu_all, wd_all, idx, ple=None):
    n, d = x.shape
    dff = wd_all.shape[-2]
    tm = _row_tile(n, 512)
    row = lambda w: pl.BlockSpec((tm, w), lambda i: (i, 0))
    in_specs = [row(d), _const_spec((1, d)), _sel_spec((d, 2 * dff), idx), _sel_spec((dff, d), idx)]
    args = [x, g, wgu_all, wd_all]
    body = _ffn_kernel
    if ple is not None:
        p_all, layer, gp, gf, wple_all, wgate_all, final = ple
        dp = p_all.shape[-1]
        in_specs += [pl.BlockSpec((None, tm, dp), lambda i: (layer, i, 0)), _const_spec((1, d)), _const_spec((1, d)),
                     _sel_spec((dp, d), (layer,)), _sel_spec((d, d), (layer,))]
        args += [p_all, gp, gf, wple_all, wgate_all]
        body = functools.partial(_ffn_ple_kernel, final=final)
    return pl.pallas_call(
        body,
        out_shape=jax.ShapeDtypeStruct((n, d), F32),
        grid=(n // tm,),
        in_specs=in_specs,
        out_specs=row(d),
        scratch_shapes=[pltpu.VMEM((tm, dff), BF16)],
        compiler_params=_params(("parallel",)),
        name="ffn" if ple is None else "ffn_ple",
    )(*args)


def _odd_kernel(x_ref, g_ref, st_ref, win_ref, cw_ref, wout_ref, o_ref, newst_ref, hist_ref):
    sb, tt, d = x_ref.shape
    x = x_ref[...].reshape(sb * tt, d)
    xn = _rms(x, g_ref[...]).astype(BF16)
    proj = _dot(xn, win_ref[...])
    u = proj[:, 2 * d:] * proj[:, :d]
    bg = proj[:, d:2 * d]

    @pl.when(pl.program_id(1) == 0)
    def _():
        hist_ref[:, 6:8, :] = st_ref[...]

    hist_ref[:, 8:8 + tt, :] = u.reshape(sb, tt, d)
    cw = cw_ref[...]
    conv = (hist_ref[:, 6:6 + tt, :] * cw[0:1] + hist_ref[:, 7:7 + tt, :] * cw[1:2]
            + hist_ref[:, 8:8 + tt, :] * cw[2:3])
    y = (bg * conv.reshape(sb * tt, d)).astype(BF16)
    o_ref[...] = (x + _dot(y, wout_ref[...])).reshape(sb, tt, d)
    last = hist_ref[:, tt + 6:tt + 8, :]
    hist_ref[:, 6:8, :] = last
    newst_ref[...] = last


def _odd_mixer(x, g, st, win_all, cw, wout_all, j):
    b, t, d = x.shape
    tt = _row_tile(t, 512)
    sb = b if tt < 128 else 1
    return pl.pallas_call(
        _odd_kernel,
        out_shape=(jax.ShapeDtypeStruct((b, t, d), F32),
                   jax.ShapeDtypeStruct((b, CONV_C - 1, d), F32)),
        grid=(b // sb, t // tt),
        in_specs=[pl.BlockSpec((sb, tt, d), lambda i, j: (i, j, 0)),
                  _const_spec((1, d)),
                  pl.BlockSpec((sb, CONV_C - 1, d), lambda i, j: (i, 0, 0)),
                  _sel_spec((d, 3 * d), (j,)), _const_spec((CONV_C, d)), _sel_spec((d, d), (j,))],
        out_specs=(pl.BlockSpec((sb, tt, d), lambda i, j: (i, j, 0)),
                   pl.BlockSpec((sb, CONV_C - 1, d), lambda i, j: (i, 0, 0))),
        scratch_shapes=[pltpu.VMEM((sb, tt + 8, d), F32)],
        compiler_params=_params(("parallel", "arbitrary")),
        name="odd_mixer",
    )(x, g, st, win_all, cw, wout_all)


def _split_bf16(a):
    hi = a.astype(BF16)
    lo = (a - hi.astype(F32)).astype(BF16)
    return hi, lo


def _softplus_parts(y):
    t = jnp.log1p(jnp.exp(-jnp.abs(y)))
    return jnp.maximum(y, 0.0) + t, jnp.maximum(-y, 0.0) + t


def _even_in_kernel(x_ref, g_ref, cst_ref, wbig_ref, wsm_ref, convw_ref, gp_ref, tri_ref,
                    big_ref, gt_ref, cum_ref, ncst_ref, hist_ref):
    tm = x_ref.shape[0]

    @pl.when(pl.program_id(1) == 0)
    def _():
        hist_ref[5:8, :] = cst_ref[...]

    xn = _rms(x_ref[...], g_ref[...])
    xh, xl = _split_bf16(xn)
    hist_ref[8:8 + tm, :] = _dot(xh, wbig_ref[:, :OFF_Z])
    big_ref[:, OFF_Z:OFF_QB] = _dot(xh, wbig_ref[:, OFF_Z:OFF_QB])
    big_ref[:, OFF_QB:OFF_KB] = _dot(xh, wbig_ref[:, OFF_QB:OFF_KB]) * (DK_B ** -0.5)
    big_ref[:, OFF_KB:] = _dot(xh, wbig_ref[:, OFF_KB:])

    wh, wl = _split_bf16(wsm_ref[...])
    y = _dot(xh, wh) + (_dot(xl, wh) + _dot(xh, wl)) + gp_ref[0:1, :]
    col = lax.broadcasted_iota(jnp.int32, y.shape, 1)
    sp_pos, sp_neg = _softplus_parts(y)
    gt = jnp.where(col < COL_G, jax.nn.sigmoid(y),
                   jnp.where(col < COL_LOGI, -jnp.exp(gp_ref[1:2, :]) * sp_pos,
                             jnp.where(col < COL_LOGF, y, -sp_neg)))
    gt = jnp.where(col < COL_LOGF + H_B, gt, 0.0)
    gt_ref[...] = gt
    g1 = gt.astype(BF16)
    r1 = gt - g1.astype(F32)
    g2 = r1.astype(BF16)
    g3 = (r1 - g2.astype(F32)).astype(BF16)
    tri = tri_ref[...]
    cum_ref[...] = _dot(tri, g1) + (_dot(tri, g2) + _dot(tri, g3))

    for blk in range(QKV_A // 128):
        c0 = blk * 128
        acc = hist_ref[5:5 + tm, c0:c0 + 128] * convw_ref[0:1, c0:c0 + 128]
        for j in range(1, CONV_A):
            acc = acc + hist_ref[5 + j:5 + j + tm, c0:c0 + 128] * convw_ref[j:j + 1, c0:c0 + 128]
        v = acc * jax.nn.sigmoid(acc)
        if blk < 2 * H_A:
            v = v * lax.rsqrt(jnp.sum(v * v, axis=-1, keepdims=True) + EPS)
        if blk < H_A:
            v = v * (DK_A ** -0.5)
        big_ref[:, c0:c0 + 128] = v

    last = hist_ref[tm + 5:tm + 8, :]
    hist_ref[5:8, :] = last
    ncst_ref[...] = last


def _chunk_tri(tm):
    t = np.arange(tm)
    same = (t[:, None] // SEQ_CHUNK) == (t[None, :] // SEQ_CHUNK)
    return (same & (t[:, None] >= t[None, :])).astype(np.float32)


def _even_in(x, g, conv_st, wbig_all, wsm_all, conv_w, gp, j):
    b, t, d = x.shape
    tm = _row_tile(t, 512)
    assert tm % SEQ_CHUNK == 0 or tm == t < SEQ_CHUNK
    tri = jnp.asarray(_chunk_tri(tm), BF16)
    tile = lambda w: pl.BlockSpec((None, tm, w), lambda i, j: (i, j, 0))
    cst = pl.BlockSpec((None, CONV_A - 1, QKV_A), lambda i, j: (i, 0, 0))
    return pl.pallas_call(
        _even_in_kernel,
        out_shape=(jax.ShapeDtypeStruct((b, t, N_BIG), F32), jax.ShapeDtypeStruct((b, t, GATE_PAD), F32),
                   jax.ShapeDtypeStruct((b, t, GATE_PAD), F32), jax.ShapeDtypeStruct((b, CONV_A - 1, QKV_A), F32)),
        grid=(b, t // tm),
        in_specs=[tile(d), _const_spec((1, d)), cst, _sel_spec((d, N_BIG), (j,)), _sel_spec((d, GATE_PAD), (j,)),
                  _const_spec((CONV_A, QKV_A)), _const_spec((8, GATE_PAD)), _const_spec((tm, tm))],
        out_specs=(tile(N_BIG), tile(GATE_PAD), tile(GATE_PAD), cst),
        scratch_shapes=[pltpu.VMEM((tm + 8, QKV_A), F32)],
        compiler_params=_params(("parallel", "arbitrary")),
        name="even_in",
    )(x, g, conv_st, wbig_all, wsm_all, conv_w, gp, tri)


def _even_seq_kernel(x_ref, big_ref, gt_ref, cum_ref, s0_ref, c0_ref, m0_ref, gnw_ref, mnw_ref, lvl_ref, wout_ref,
                     o_ref, s_ref, c_ref, m_ref, mix_ref, *, t_valid):
    L = SEQ_CHUNK
    nc = x_ref.shape[0] // L
    HA, HB, CH = range(H_A), range(H_B), range(nc)

    @pl.when(pl.program_id(1) == 0)
    def _():
        s_ref[...] = s0_ref[...]
        c_ref[...] = c0_ref[...]
        m_ref[...] = m0_ref[...]

    row = lax.broadcasted_iota(jnp.int32, (L, L), 0)
    col = lax.broadcasted_iota(jnp.int32, (L, L), 1)
    incl = row >= col
    col_ok = incl if t_valid == L else (incl & (col < t_valid))
    eye = jnp.where(row == col, 1.0, 0.0)
    one_col = jnp.where(col == 0, 1.0, 0.0)

    def blk(c, off, h):
        return big_ref[c * L:(c + 1) * L, off + h * 128:off + (h + 1) * 128]

    def mlstm_local(c):
        cum_c = cum_ref[c * L:(c + 1) * L, :]
        cum_t, gt_t = cum_c.T, gt_ref[c * L:(c + 1) * L, :].T
        p = dict(q=[blk(c, OFF_QB, h) for h in HB], k_t=[blk(c, OFF_KB, h).T for h in HB],
                 v_ext=[jnp.concatenate([blk(c, OFF_VB, h), one_col], axis=1) for h in HB],
                 b_c=[cum_c[:, COL_LOGF + h:COL_LOGF + h + 1] for h in HB])
        p["dmat"] = [jnp.where(col_ok, p["b_c"][h] - cum_t[COL_LOGF + h:COL_LOGF + h + 1, :]
                               + gt_t[COL_LOGI + h:COL_LOGI + h + 1, :], -jnp.inf) for h in HB]
        p["dmax"] = [jnp.max(p["dmat"][h], axis=-1, keepdims=True) for h in HB]
        p["s"] = [_bdot(p["q"][h], p["k_t"][h]) for h in HB]
        return p

    def mlstm_state(p, m_prev, c_cur):
        m_t = [jnp.maximum(p["b_c"][h] + m_prev[h], p["dmax"][h]) for h in HB]
        w_inter = [jnp.exp(p["b_c"][h] + m_prev[h] - m_t[h]) for h in HB]
        w_intra = [jnp.exp(p["dmat"][h] - m_t[h]) for h in HB]
        av = [_bdot(p["s"][h] * w_intra[h], p["v_ext"][h]) for h in HB]
        qc = [_bdot(p["q"][h], c_cur[h]) for h in HB]
        cu = [_bdot(p["k_t"][h] * w_intra[h][L - 1:L, :], p["v_ext"][h]) for h in HB]
        tot = [w_inter[h] * qc[h] + av[h] for h in HB]
        out = [tot[h][:, :DV_B] / jnp.maximum(jnp.abs(tot[h][:, DV_B:DV_B + 1]), jnp.exp(-m_t[h])) for h in HB]
        c_new = [w_inter[h][L - 1:L, :] * c_cur[h] + cu[h] for h in HB]
        return out, [m_t[h][L - 1:L, :] for h in HB], c_new

    def gdn_local(c):
        gt_c, cum_c = gt_ref[c * L:(c + 1) * L, :], cum_ref[c * L:(c + 1) * L, :]
        cum_t = cum_c.T
        p = dict(q=[blk(c, 0, h) for h in HA], k=[blk(c, H_A * DK_A, h) for h in HA],
                 v=[blk(c, 2 * H_A * DK_A, h) for h in HA],
                 beta=[gt_c[:, COL_BETA + h:COL_BETA + h + 1] for h in HA],
                 g_c=[cum_c[:, COL_G + h:COL_G + h + 1] for h in HA],
                 g_r=[cum_t[COL_G + h:COL_G + h + 1, :] for h in HA])
        p["gam"] = [jnp.where(incl, jnp.exp(jnp.where(incl, p["g_c"][h] - p["g_r"][h], 0.0)), 0.0) for h in HA]
        p["k_t"] = [p["k"][h].T for h in HA]
        p["qk_kk"] = [_bdot(jnp.concatenate([p["q"][h], p["k"][h]], axis=0), p["k_t"][h]) for h in HA]
        p["a_s"] = [p["beta"][h] * p["qk_kk"][h][L:] * p["gam"][h] for h in HA]
        p["d"] = [eye - p["a_s"][h] * lvl_ref[0] for h in HA]
        return p

    def gdn_solve(p):
        p["e_g"] = [jnp.exp(p["g_c"][h]) for h in HA]
        p["sol"] = [_bdot(p["d"][h], jnp.concatenate([p["beta"][h] * p["v"][h],
                                                      (p["beta"][h] * p["e_g"][h]) * p["k"][h]], axis=1)) for h in HA]

    def gdn_state_a(p, s_cur):
        wq = [_bdot(jnp.concatenate([p["sol"][h][:, DV_A:], p["q"][h]], axis=0), s_cur[h]) for h in HA]
        p["u"] = [p["sol"][h][:, :DV_A] - wq[h][:L] for h in HA]
        p["qs"] = [wq[h][L:] for h in HA]

    def gdn_state_b(p, s_cur):
        out = [p["e_g"][h] * p["qs"][h] + _bdot(p["qk_kk"][h][:L] * p["gam"][h], p["u"][h]) for h in HA]
        g_last = [p["g_r"][h][:, L - 1:L] for h in HA]
        s_new = [jnp.exp(g_last[h]) * s_cur[h] + _bdot(p["k_t"][h] * jnp.exp(g_last[h] - p["g_r"][h]), p["u"][h])
                 for h in HA]
        return out, s_new

    m_prev = [m_ref[h, 0:1, 0:1] for h in HB]
    c_cur = [c_ref[h] for h in HB]
    s_cur = [s_ref[h] for h in HA]
    m_out, g_out = [None] * nc, [None] * nc
    prev = []
    for g0 in range(0, nc, 2):
        cs = list(range(g0, min(g0 + 2, nc)))
        ml = [mlstm_local(c) for c in cs]
        gd = [gdn_local(c) for c in cs]
        riders = {1 + i: ("mlstm", ml[i], c) for i, c in enumerate(cs)}
        for i, (c, p) in enumerate(prev):
            riders[3 + 2 * i], riders[4 + 2 * i] = ("gdn_a", p, c), ("gdn_b", p, c)
        for lv in range(1, N_LEVELS):
            de = [[_bdot(p["d"][h], p["a_s"][h] * lvl_ref[lv]) for h in HA] for p in gd]
            kind, arg, c = riders.get(lv, (None, None, None))
            if kind == "mlstm":
                m_out[c], m_prev, c_cur = mlstm_state(arg, m_prev, c_cur)
            elif kind == "gdn_a":
                gdn_state_a(arg, s_cur)
            elif kind == "gdn_b":
                g_out[c], s_cur = gdn_state_b(arg, s_cur)
            for p, de_p in zip(gd, de):
                p["d"] = [p["d"][h] - _bdot(de_p[h], p["d"][h]) for h in HA]
        for p in gd:
            gdn_solve(p)
        prev = list(zip(cs, gd))
    for c, p in prev:
        gdn_state_a(p, s_cur)
        g_out[c], s_cur = gdn_state_b(p, s_cur)

    for h in HA:
        s_ref[h] = s_cur[h]
    for h in HB:
        c_ref[h] = c_cur[h]
        m_ref[h] = jnp.broadcast_to(m_prev[h], m_ref.shape[1:])

    for c in CH:
        for h in HA:
            o = g_out[c][h]
            o = o * lax.rsqrt(jnp.mean(o * o, axis=-1, keepdims=True) + EPS) * gnw_ref[...]
            z = blk(c, OFF_Z, h)
            mix_ref[c * L:(c + 1) * L, h * DV_A:(h + 1) * DV_A] = (o * (z * jax.nn.sigmoid(z))).astype(BF16)
        for h in HB:
            hh = m_out[c][h]
            hh = hh * lax.rsqrt(jnp.mean(hh * hh, axis=-1, keepdims=True) + EPS)
            hh = hh * mnw_ref[:, h * DV_B:(h + 1) * DV_B] * jax.nn.sigmoid(blk(c, OFF_OB, h))
            c0 = H_A * DV_A + h * DV_B
            mix_ref[c * L:(c + 1) * L, c0:c0 + DV_B] = hh.astype(BF16)
    o_ref[...] = x_ref[...] + _dot(mix_ref[...], wout_ref[...])


def _level_masks():
    t = np.arange(SEQ_CHUNK)
    out = []
    for lv in range(N_LEVELS):
        b = 1 << lv
        tb, sb = t[:, None] // b, t[None, :] // b
        out.append(((tb % 2 == 1) & (sb == tb - 1)).astype(np.float32))
    return np.stack(out)


def _even_seq(x, big, gt, cum, S0, C0, n0, m0, gdn_norm_w, mlstm_norm_w, wout_all, j):
    B, T, D = x.shape
    L = SEQ_CHUNK
    t_valid = L
    if T % L:
        assert T < L
        t_valid = T
        pad = lambda a, mode: jnp.pad(a, ((0, 0), (0, L - T), (0, 0)), mode=mode)
        x, big, gt, cum = pad(x, "constant"), pad(big, "constant"), pad(gt, "constant"), pad(cum, "edge")
    Tp = x.shape[1]
    tt = min(Tp, 4 * L)
    c_ext = jnp.concatenate([C0, n0[..., None], jnp.zeros(C0.shape[:-1] + (DV_B - 1,), F32)], axis=-1)
    m_b = jnp.broadcast_to(m0[:, :, None, None], (B, H_B, 8, 128))
    lvl = jnp.asarray(_level_masks())
    tile = lambda w: pl.BlockSpec((None, tt, w), lambda b, c: (b, c, 0))
    state = lambda *s: pl.BlockSpec((None,) + s, lambda b, c: (b,) + (0,) * len(s))
    dmix = H_A * DV_A + H_B * DV_B
    out, S, c_new, m_new = pl.pallas_call(
        functools.partial(_even_seq_kernel, t_valid=t_valid),
        out_shape=(jax.ShapeDtypeStruct((B, Tp, D), F32),
                   jax.ShapeDtypeStruct((B, H_A, DK_A, DV_A), F32),
                   jax.ShapeDtypeStruct((B, H_B, DK_B, 2 * DV_B), F32),
                   jax.ShapeDtypeStruct((B, H_B, 8, 128), F32)),
        grid=(B, Tp // tt),
        in_specs=[tile(D), tile(N_BIG), tile(GATE_PAD), tile(GATE_PAD),
                  state(H_A, DK_A, DV_A), state(H_B, DK_B, 2 * DV_B), state(H_B, 8, 128),
                  _const_spec((1, DV_A)), _const_spec((1, H_B * DV_B)), _const_spec((N_LEVELS, L, L)),
                  _sel_spec((dmix, D), (j,))],
        out_specs=(tile(D), state(H_A, DK_A, DV_A), state(H_B, DK_B, 2 * DV_B), state(H_B, 8, 128)),
        scratch_shapes=[pltpu.VMEM((tt, dmix), BF16)],
        compiler_params=_params(("parallel", "arbitrary")),
        name="even_seq",
    )(x, big, gt, cum, S0, c_ext, m_b, gdn_norm_w.reshape(1, DV_A), mlstm_norm_w.reshape(1, H_B * DV_B), lvl, wout_all)
    return out[:, :T], (S, c_new[..., :DV_B], c_new[..., DV_B], m_new[:, :, 0, 0])


def _prep_weights(ffn_w_gu, ffn_w_d, w_ple, w_ple_gate, w_in_even, w_out_even, w_in_odd, w_out_odd):
    nbig = QKV_A + H_A * DV_A
    nmid = nbig + 2 * H_A
    nb_end = nmid + 4 * H_B * DK_B
    w_big = jnp.concatenate([w_in_even[:, :, :nbig], w_in_even[:, :, nmid:nb_end]], axis=-1).astype(BF16)
    w_sm = jnp.concatenate([w_in_even[:, :, nbig:nmid], w_in_even[:, :, nb_end:]], axis=-1)
    w_sm = jnp.pad(w_sm, ((0, 0), (0, 0), (0, GATE_PAD - w_sm.shape[-1])))
    return dict(wgu=ffn_w_gu.astype(BF16), wd=ffn_w_d.astype(BF16), wple=w_ple.astype(BF16),
                wgate=w_ple_gate.astype(BF16), w_big=w_big, w_sm=w_sm, w_out_even=w_out_even.astype(BF16),
                w_in_odd=w_in_odd.astype(BF16), w_out_odd=w_out_odd.astype(BF16))


def _gate_params(a_log, dt_bias, b_i, b_f):
    gp = jnp.zeros((8, GATE_PAD), F32)
    gp = gp.at[0, COL_G:COL_G + H_A].set(dt_bias).at[0, COL_LOGI:COL_LOGI + H_B].set(b_i)
    return gp.at[0, COL_LOGF:COL_LOGF + H_B].set(b_f).at[1, COL_G:COL_G + H_A].set(a_log)


def _run_trunk(x, p, init_even, init_odd, W, norm_g, final_norm, gdn_conv_w, gdn_a_log, gdn_dt_bias,
               gdn_norm_w, mlstm_b_i, mlstm_b_f, mlstm_norm_w, conv_c_w):
    B, T, D = x.shape
    depth = norm_g.shape[0]
    n = B * T
    x = x.reshape(n, D)
    new_even, new_odd = [], []
    gf = final_norm.reshape(1, D)
    for i in range(depth):
        j = i // 2
        g = lambda k: norm_g[i, k].reshape(1, D)
        x = _ffn(x, g(0), W["wgu"], W["wd"], (i, 0))
        if i % 2 == 0:
            conv_st, S0, C0, n0, m0 = init_even[j]
            gp = _gate_params(gdn_a_log[j], gdn_dt_bias[j], mlstm_b_i[j], mlstm_b_f[j])
            big, gt, cum, ncst = _even_in(x.reshape(B, T, D), g(1), conv_st, W["w_big"], W["w_sm"],
                                          gdn_conv_w[j], gp, j)
            y, st = _even_seq(x.reshape(B, T, D), big, gt, cum, S0, C0, n0, m0, gdn_norm_w[j], mlstm_norm_w[j],
                              W["w_out_even"], j)
            new_even.append((ncst,) + st)
        else:
            y, st = _odd_mixer(x.reshape(B, T, D), g(1), init_odd[j], W["w_in_odd"], conv_c_w[j],
                               W["w_out_odd"], j)
            new_odd.append(st)
        x = y.reshape(n, D)
        x = _ffn(x, g(2), W["wgu"], W["wd"], (i, 1),
                 ple=(p.reshape(depth, n, -1), i, g(3), gf, W["wple"], W["wgate"], i == depth - 1))
    st_even = [jnp.stack([s[c] for s in new_even]) for c in range(5)]
    return (x.reshape(B, T, D), *st_even, jnp.stack(new_odd))


def kernel(x_prompt, x_sample, state_gdn_conv, state_gdn_S, state_mlstm_C, state_mlstm_n, state_mlstm_m,
           state_conv, p_prompt, p_sample, norm_g, final_norm, ffn_w_gu, ffn_w_d, w_ple, w_ple_gate,
           w_in_even, gdn_conv_w, gdn_a_log, gdn_dt_bias, gdn_norm_w, mlstm_b_i, mlstm_b_f, mlstm_norm_w,
           w_out_even, w_in_odd, conv_c_w, w_out_odd):
    Bp = x_prompt.shape[0]
    n_even, n_odd = state_gdn_S.shape[0], state_conv.shape[0]
    D = x_prompt.shape[-1]
    W = _prep_weights(ffn_w_gu, ffn_w_d, w_ple, w_ple_gate, w_in_even, w_out_even, w_in_odd, w_out_odd)
    zero_even = (jnp.zeros((Bp, CONV_A - 1, QKV_A), F32), jnp.zeros((Bp, H_A, DK_A, DV_A), F32),
                 jnp.zeros((Bp, H_B, DK_B, DV_B), F32), jnp.zeros((Bp, H_B, DK_B), F32),
                 jnp.zeros((Bp, H_B), F32))
    init_even_p = [zero_even] * n_even
    init_odd_p = [jnp.zeros((Bp, CONV_C - 1, D), F32)] * n_odd
    init_even_s = [(state_gdn_conv[j], state_gdn_S[j], state_mlstm_C[j], state_mlstm_n[j], state_mlstm_m[j])
                   for j in range(n_even)]
    init_odd_s = [state_conv[j] for j in range(n_odd)]
    rest = (norm_g, final_norm, gdn_conv_w, gdn_a_log, gdn_dt_bias, gdn_norm_w, mlstm_b_i, mlstm_b_f,
            mlstm_norm_w, conv_c_w)
    outs_p = _run_trunk(x_prompt, p_prompt, init_even_p, init_odd_p, W, *rest)
    outs_s = _run_trunk(x_sample, p_sample, init_even_s, init_odd_s, W, *rest)
    return (outs_p[0], outs_s[0]) + tuple(outs_p[1:]) + tuple(outs_s[1:])
```

```python
import functools

import jax
import jax.numpy as jnp
import numpy as np
from jax import lax
from jax.experimental import pallas as pl
from jax.experimental.pallas import tpu as pltpu

F32 = jnp.float32
BF16 = jnp.bfloat16

EPS = 1e-6
H_A, DK_A, DV_A, CONV_A = 4, 128, 128, 4
H_B, DK_B, DV_B = 4, 128, 128
CONV_C = 3
QKV_A = H_A * (2 * DK_A + DV_A)
FF_CHUNK = 256
GATE_PAD = 128
SEQ_CHUNK = 128
N_LEVELS = 7
EVEN_TILE = 256
A_FRONT = 10
VMEM_LIMIT = 56 * 1024 * 1024

OFF_Z = QKV_A
OFF_QB = OFF_Z + H_A * DV_A
OFF_KB = OFF_QB + H_B * DK_B
OFF_VB = OFF_KB + H_B * DK_B
OFF_OB = OFF_VB + H_B * DV_B
N_BIG = OFF_OB + H_B * DV_B
COL_BETA, COL_G, COL_LOGI, COL_LOGF = 0, H_A, 2 * H_A, 2 * H_A + H_B


def _rms(x, g):
    ms = jnp.mean(x * x, axis=-1, keepdims=True)
    return x * lax.rsqrt(ms + EPS) * g


def _dot(a, b):
    return jnp.dot(a, b, preferred_element_type=F32)


def _bdot(a, b):
    return jnp.dot(a.astype(BF16), b.astype(BF16), preferred_element_type=F32)


def _const_spec(shape):
    n = len(shape)
    return pl.BlockSpec(shape, lambda *_: (0,) * n, pipeline_mode=pl.Buffered(1))


def _sel_spec(shape, idx):
    n = len(shape)
    return pl.BlockSpec((None,) * len(idx) + tuple(shape), lambda *_: tuple(idx) + (0,) * n,
                        pipeline_mode=pl.Buffered(1))


def _params(sem):
    return pltpu.CompilerParams(dimension_semantics=sem, vmem_limit_bytes=VMEM_LIMIT)


def _row_tile(n, want):
    t = min(n, want)
    assert n % t == 0, (n, t)
    return t


def _swiglu_residual(x, g_ref, wgu_ref, wd_ref, act_ref):
    dff = wd_ref.shape[0]
    xn = _rms(x, g_ref[...]).astype(BF16)
    for c in range(dff // FF_CHUNK):
        lo = c * FF_CHUNK
        a = _dot(xn, wgu_ref[:, lo:lo + FF_CHUNK])
        b = _dot(xn, wgu_ref[:, dff + lo:dff + lo + FF_CHUNK])
        act_ref[:, lo:lo + FF_CHUNK] = (a * jax.nn.sigmoid(a) * b).astype(BF16)
    return x + 0.5 * _dot(act_ref[...], wd_ref[...])


def _ffn_kernel(x_ref, g_ref, wgu_ref, wd_ref, o_ref, act_ref):
    o_ref[...] = _swiglu_residual(x_ref[...], g_ref, wgu_ref, wd_ref, act_ref)


def _ffn_ple_kernel(x_ref, g_ref, wgu_ref, wd_ref, p_ref, gp_ref, gf_ref, wple_ref, wgate_ref, o_ref, act_ref, *,
                    final):
    y = _swiglu_residual(x_ref[...], g_ref, wgu_ref, wd_ref, act_ref)
    gate = jax.nn.sigmoid(_dot(_rms(y, gp_ref[...]).astype(BF16), wgate_ref[...]))
    y = y + _dot(p_ref[...].astype(BF16), wple_ref[...]) * gate
    if final:
        y = _rms(y, gf_ref[...])
    o_ref[...] = y


def _ffn(x, g, wgu_all, wd_all, idx, ple=None):
    n, d = x.shape
    dff = wd_all.shape[-2]
    tm = _row_tile(n, 512)
    row = lambda w: pl.BlockSpec((tm, w), lambda i: (i, 0))
    in_specs = [row(d), _const_spec((1, d)), _sel_spec((d, 2 * dff), idx), _sel_spec((dff, d), idx)]
    args = [x, g, wgu_all, wd_all]
    body = _ffn_kernel
    if ple is not None:
        p_all, layer, gp, gf, wple_all, wgate_all, final = ple
        dp = p_all.shape[-1]
        in_specs += [pl.BlockSpec((None, tm, dp), lambda i: (layer, i, 0)), _const_spec((1, d)), _const_spec((1, d)),
                     _sel_spec((dp, d), (layer,)), _sel_spec((d, d), (layer,))]
        args += [p_all, gp, gf, wple_all, wgate_all]
        body = functools.partial(_ffn_ple_kernel, final=final)
    return pl.pallas_call(
        body,
        out_shape=jax.ShapeDtypeStruct((n, d), F32),
        grid=(n // tm,),
        in_specs=in_specs,
        out_specs=row(d),
        scratch_shapes=[pltpu.VMEM((tm, dff), BF16)],
        compiler_params=_params(("parallel",)),
        name="ffn" if ple is None else "ffn_ple",
    )(*args)


def _odd_kernel(x_ref, g_ref, st_ref, win_ref, cw_ref, wout_ref, o_ref, newst_ref, hist_ref):
    sb, tt, d = x_ref.shape
    x = x_ref[...].reshape(sb * tt, d)
    xn = _rms(x, g_ref[...]).astype(BF16)
    proj = _dot(xn, win_ref[...])
    u = proj[:, 2 * d:] * proj[:, :d]
    bg = proj[:, d:2 * d]

    @pl.when(pl.program_id(1) == 0)
    def _():
        hist_ref[:, 6:8, :] = st_ref[...]

    hist_ref[:, 8:8 + tt, :] = u.reshape(sb, tt, d)
    cw = cw_ref[...]
    conv = (hist_ref[:, 6:6 + tt, :] * cw[0:1] + hist_ref[:, 7:7 + tt, :] * cw[1:2]
            + hist_ref[:, 8:8 + tt, :] * cw[2:3])
    y = (bg * conv.reshape(sb * tt, d)).astype(BF16)
    o_ref[...] = (x + _dot(y, wout_ref[...])).reshape(sb, tt, d)
    last = hist_ref[:, tt + 6:tt + 8, :]
    hist_ref[:, 6:8, :] = last
    newst_ref[...] = last


def _odd_mixer(x, g, st, win_all, cw, wout_all, j):
    b, t, d = x.shape
    tt = _row_tile(t, 512)
    sb = b if tt < 128 else 1
    return pl.pallas_call(
        _odd_kernel,
        out_shape=(jax.ShapeDtypeStruct((b, t, d), F32),
                   jax.ShapeDtypeStruct((b, CONV_C - 1, d), F32)),
        grid=(b // sb, t // tt),
        in_specs=[pl.BlockSpec((sb, tt, d), lambda i, j: (i, j, 0)),
                  _const_spec((1, d)),
                  pl.BlockSpec((sb, CONV_C - 1, d), lambda i, j: (i, 0, 0)),
                  _sel_spec((d, 3 * d), (j,)), _const_spec((CONV_C, d)), _sel_spec((d, d), (j,))],
        out_specs=(pl.BlockSpec((sb, tt, d), lambda i, j: (i, j, 0)),
                   pl.BlockSpec((sb, CONV_C - 1, d), lambda i, j: (i, 0, 0))),
        scratch_shapes=[pltpu.VMEM((sb, tt + 8, d), F32)],
        compiler_params=_params(("parallel", "arbitrary")),
        name="odd_mixer",
    )(x, g, st, win_all, cw, wout_all)


def _split_bf16(a):
    hi = a.astype(BF16)
    lo = (a - hi.astype(F32)).astype(BF16)
    return hi, lo


def _softplus_parts(y):
    t = jnp.log1p(jnp.exp(-jnp.abs(y)))
    return jnp.maximum(y, 0.0) + t, jnp.maximum(-y, 0.0) + t


def _even_in_kernel(x_ref, g_ref, cst_ref, wbig_ref, wsm_ref, convw_ref, gp_ref, tri_ref,
                    big_ref, gt_ref, cum_ref, ncst_ref, hist_ref):
    tm = x_ref.shape[0]

    @pl.when(pl.program_id(1) == 0)
    def _():
        hist_ref[5:8, :] = cst_ref[...]

    xn = _rms(x_ref[...], g_ref[...])
    xh, xl = _split_bf16(xn)
    hist_ref[8:8 + tm, :] = _dot(xh, wbig_ref[:, :OFF_Z])
    big_ref[:, OFF_Z:OFF_QB] = _dot(xh, wbig_ref[:, OFF_Z:OFF_QB])
    big_ref[:, OFF_QB:OFF_KB] = _dot(xh, wbig_ref[:, OFF_QB:OFF_KB]) * (DK_B ** -0.5)
    big_ref[:, OFF_KB:] = _dot(xh, wbig_ref[:, OFF_KB:])

    wh, wl = _split_bf16(wsm_ref[...])
    y = _dot(xh, wh) + (_dot(xl, wh) + _dot(xh, wl)) + gp_ref[0:1, :]
    col = lax.broadcasted_iota(jnp.int32, y.shape, 1)
    sp_pos, sp_neg = _softplus_parts(y)
    gt = jnp.where(col < COL_G, jax.nn.sigmoid(y),
                   jnp.where(col < COL_LOGI, -jnp.exp(gp_ref[1:2, :]) * sp_pos,
                             jnp.where(col < COL_LOGF, y, -sp_neg)))
    gt = jnp.where(col < COL_LOGF + H_B, gt, 0.0)
    gt_ref[...] = gt
    g1 = gt.astype(BF16)
    r1 = gt - g1.astype(F32)
    g2 = r1.astype(BF16)
    g3 = (r1 - g2.astype(F32)).astype(BF16)
    tri = tri_ref[...]
    cum_ref[...] = _dot(tri, g1) + (_dot(tri, g2) + _dot(tri, g3))

    for blk in range(QKV_A // 128):
        c0 = blk * 128
        acc = hist_ref[5:5 + tm, c0:c0 + 128] * convw_ref[0:1, c0:c0 + 128]
        for j in range(1, CONV_A):
            acc = acc + hist_ref[5 + j:5 + j + tm, c0:c0 + 128] * convw_ref[j:j + 1, c0:c0 + 128]
        v = acc * jax.nn.sigmoid(acc)
        if blk < 2 * H_A:
            v = v * lax.rsqrt(jnp.sum(v * v, axis=-1, keepdims=True) + EPS)
        if blk < H_A:
            v = v * (DK_A ** -0.5)
        big_ref[:, c0:c0 + 128] = v

    last = hist_ref[tm + 5:tm + 8, :]
    hist_ref[5:8, :] = last
    ncst_ref[...] = last


def _chunk_tri(tm):
    t = np.arange(tm)
    same = (t[:, None] // SEQ_CHUNK) == (t[None, :] // SEQ_CHUNK)
    return (same & (t[:, None] >= t[None, :])).astype(np.float32)


def _even_in(x, g, conv_st, wbig_all, wsm_all, conv_w, gp, j):
    b, t, d = x.shape
    tm = _row_tile(t, 512)
    assert tm % SEQ_CHUNK == 0 or tm == t < SEQ_CHUNK
    tri = jnp.asarray(_chunk_tri(tm), BF16)
    tile = lambda w: pl.BlockSpec((None, tm, w), lambda i, j: (i, j, 0))
    cst = pl.BlockSpec((None, CONV_A - 1, QKV_A), lambda i, j: (i, 0, 0))
    return pl.pallas_call(
        _even_in_kernel,
        out_shape=(jax.ShapeDtypeStruct((b, t, N_BIG), F32), jax.ShapeDtypeStruct((b, t, GATE_PAD), F32),
                   jax.ShapeDtypeStruct((b, t, GATE_PAD), F32), jax.ShapeDtypeStruct((b, CONV_A - 1, QKV_A), F32)),
        grid=(b, t // tm),
        in_specs=[tile(d), _const_spec((1, d)), cst, _sel_spec((d, N_BIG), (j,)), _sel_spec((d, GATE_PAD), (j,)),
                  _const_spec((CONV_A, QKV_A)), _const_spec((8, GATE_PAD)), _const_spec((tm, tm))],
        out_specs=(tile(N_BIG), tile(GATE_PAD), tile(GATE_PAD), cst),
        scratch_shapes=[pltpu.VMEM((tm + 8, QKV_A), F32)],
        compiler_params=_params(("parallel", "arbitrary")),
        name="even_in",
    )(x, g, conv_st, wbig_all, wsm_all, conv_w, gp, tri)


def _even_seq_kernel(x_ref, big_ref, gt_ref, cum_ref, s0_ref, c0_ref, m0_ref, gnw_ref, mnw_ref, lvl_ref, wout_ref,
                     o_ref, s_ref, c_ref, m_ref, mix_ref, *, t_valid):
    L = SEQ_CHUNK
    nc = x_ref.shape[0] // L
    HA, HB, CH = range(H_A), range(H_B), range(nc)

    @pl.when(pl.program_id(1) == 0)
    def _():
        s_ref[...] = s0_ref[...]
        c_ref[...] = c0_ref[...]
        m_ref[...] = m0_ref[...]

    row = lax.broadcasted_iota(jnp.int32, (L, L), 0)
    col = lax.broadcasted_iota(jnp.int32, (L, L), 1)
    incl = row >= col
    col_ok = incl if t_valid == L else (incl & (col < t_valid))
    eye = jnp.where(row == col, 1.0, 0.0)
    one_col = jnp.where(col == 0, 1.0, 0.0)

    def blk(c, off, h):
        return big_ref[c * L:(c + 1) * L, off + h * 128:off + (h + 1) * 128]

    def mlstm_local(c):
        cum_c = cum_ref[c * L:(c + 1) * L, :]
        cum_t, gt_t = cum_c.T, gt_ref[c * L:(c + 1) * L, :].T
        p = dict(q=[blk(c, OFF_QB, h) for h in HB], k_t=[blk(c, OFF_KB, h).T for h in HB],
                 v_ext=[jnp.concatenate([blk(c, OFF_VB, h), one_col], axis=1) for h in HB],
                 b_c=[cum_c[:, COL_LOGF + h:COL_LOGF + h + 1] for h in HB])
        p["dmat"] = [jnp.where(col_ok, p["b_c"][h] - cum_t[COL_LOGF + h:COL_LOGF + h + 1, :]
                               + gt_t[COL_LOGI + h:COL_LOGI + h + 1, :], -jnp.inf) for h in HB]
        p["dmax"] = [jnp.max(p["dmat"][h], axis=-1, keepdims=True) for h in HB]
        p["s"] = [_bdot(p["q"][h], p["k_t"][h]) for h in HB]
        return p

    def mlstm_state(p, m_prev, c_cur):
        m_t = [jnp.maximum(p["b_c"][h] + m_prev[h], p["dmax"][h]) for h in HB]
        w_inter = [jnp.exp(p["b_c"][h] + m_prev[h] - m_t[h]) for h in HB]
        w_intra = [jnp.exp(p["dmat"][h] - m_t[h]) for h in HB]
        av = [_bdot(p["s"][h] * w_intra[h], p["v_ext"][h]) for h in HB]
        qc = [_bdot(p["q"][h], c_cur[h]) for h in HB]
        cu = [_bdot(p["k_t"][h] * w_intra[h][L - 1:L, :], p["v_ext"][h]) for h in HB]
        tot = [w_inter[h] * qc[h] + av[h] for h in HB]
        out = [tot[h][:, :DV_B] / jnp.maximum(jnp.abs(tot[h][:, DV_B:DV_B + 1]), jnp.exp(-m_t[h])) for h in HB]
        c_new = [w_inter[h][L - 1:L, :] * c_cur[h] + cu[h] for h in HB]
        return out, [m_t[h][L - 1:L, :] for h in HB], c_new

    def gdn_local(c):
        gt_c, cum_c = gt_ref[c * L:(c + 1) * L, :], cum_ref[c * L:(c + 1) * L, :]
        cum_t = cum_c.T
        p = dict(q=[blk(c, 0, h) for h in HA], k=[blk(c, H_A * DK_A, h) for h in HA],
                 v=[blk(c, 2 * H_A * DK_A, h) for h in HA],
                 beta=[gt_c[:, COL_BETA + h:COL_BETA + h + 1] for h in HA],
                 g_c=[cum_c[:, COL_G + h:COL_G + h + 1] for h in HA],
                 g_r=[cum_t[COL_G + h:COL_G + h + 1, :] for h in HA])
        p["gam"] = [jnp.where(incl, jnp.exp(jnp.where(incl, p["g_c"][h] - p["g_r"][h], 0.0)), 0.0) for h in HA]
        p["k_t"] = [p["k"][h].T for h in HA]
        p["qk_kk"] = [_bdot(jnp.concatenate([p["q"][h], p["k"][h]], axis=0), p["k_t"][h]) for h in HA]
        p["a_s"] = [p["beta"][h] * p["qk_kk"][h][L:] * p["gam"][h] for h in HA]
        p["d"] = [eye - p["a_s"][h] * lvl_ref[0] for h in HA]
        return p

    def gdn_solve(p):
        p["e_g"] = [jnp.exp(p["g_c"][h]) for h in HA]
        p["sol"] = [_bdot(p["d"][h], jnp.concatenate([p["beta"][h] * p["v"][h],
                                                      (p["beta"][h] * p["e_g"][h]) * p["k"][h]], axis=1)) for h in HA]

    def gdn_state_a(p, s_cur):
        wq = [_bdot(jnp.concatenate([p["sol"][h][:, DV_A:], p["q"][h]], axis=0), s_cur[h]) for h in HA]
        p["u"] = [p["sol"][h][:, :DV_A] - wq[h][:L] for h in HA]
        p["qs"] = [wq[h][L:] for h in HA]

    def gdn_state_b(p, s_cur):
        out = [p["e_g"][h] * p["qs"][h] + _bdot(p["qk_kk"][h][:L] * p["gam"][h], p["u"][h]) for h in HA]
        g_last = [p["g_r"][h][:, L - 1:L] for h in HA]
        s_new = [jnp.exp(g_last[h]) * s_cur[h] + _bdot(p["k_t"][h] * jnp.exp(g_last[h] - p["g_r"][h]), p["u"][h])
                 for h in HA]
        return out, s_new

    m_prev = [m_ref[h, 0:1, 0:1] for h in HB]
    c_cur = [c_ref[h] for h in HB]
    s_cur = [s_ref[h] for h in HA]
    m_out, g_out = [None] * nc, [None] * nc
    prev = []
    for g0 in range(0, nc, 2):
        cs = list(range(g0, min(g0 + 2, nc)))
        ml = [mlstm_local(c) for c in cs]
        gd = [gdn_local(c) for c in cs]
        riders = {1 + i: ("mlstm", ml[i], c) for i, c in enumerate(cs)}
        for i, (c, p) in enumerate(prev):
            riders[3 + 2 * i], riders[4 + 2 * i] = ("gdn_a", p, c), ("gdn_b", p, c)
        for lv in range(1, N_LEVELS):
            de = [[_bdot(p["d"][h], p["a_s"][h] * lvl_ref[lv]) for h in HA] for p in gd]
            kind, arg, c = riders.get(lv, (None, None, None))
            if kind == "mlstm":
                m_out[c], m_prev, c_cur = mlstm_state(arg, m_prev, c_cur)
            elif kind == "gdn_a":
                gdn_state_a(arg, s_cur)
            elif kind == "gdn_b":
                g_out[c], s_cur = gdn_state_b(arg, s_cur)
            for p, de_p in zip(gd, de):
                p["d"] = [p["d"][h] - _bdot(de_p[h], p["d"][h]) for h in HA]
        for p in gd:
            gdn_solve(p)
        prev = list(zip(cs, gd))
    for c, p in prev:
        gdn_state_a(p, s_cur)
        g_out[c], s_cur = gdn_state_b(p, s_cur)

    for h in HA:
        s_ref[h] = s_cur[h]
    for h in HB:
        c_ref[h] = c_cur[h]
        m_ref[h] = jnp.broadcast_to(m_prev[h], m_ref.shape[1:])

    for c in CH:
        for h in HA:
            o = g_out[c][h]
            o = o * lax.rsqrt(jnp.mean(o * o, axis=-1, keepdims=True) + EPS) * gnw_ref[...]
            z = blk(c, OFF_Z, h)
            mix_ref[c * L:(c + 1) * L, h * DV_A:(h + 1) * DV_A] = (o * (z * jax.nn.sigmoid(z))).astype(BF16)
        for h in HB:
            hh = m_out[c][h]
            hh = hh * lax.rsqrt(jnp.mean(hh * hh, axis=-1, keepdims=True) + EPS)
            hh = hh * mnw_ref[:, h * DV_B:(h + 1) * DV_B] * jax.nn.sigmoid(blk(c, OFF_OB, h))
            c0 = H_A * DV_A + h * DV_B
            mix_ref[c * L:(c + 1) * L, c0:c0 + DV_B] = hh.astype(BF16)
    o_ref[...] = x_ref[...] + _dot(mix_ref[...], wout_ref[...])


def _level_masks():
    t = np.arange(SEQ_CHUNK)
    out = []
    for lv in range(N_LEVELS):
        b = 1 << lv
        tb, sb = t[:, None] // b, t[None, :] // b
        out.append(((tb % 2 == 1) & (sb == tb - 1)).astype(np.float32))
    return np.stack(out)


def _even_seq(x, big, gt, cum, S0, C0, n0, m0, gdn_norm_w, mlstm_norm_w, wout_all, j):
    B, T, D = x.shape
    L = SEQ_CHUNK
    t_valid = L
    if T % L:
        assert T < L
        t_valid = T
        pad = lambda a, mode: jnp.pad(a, ((0, 0), (0, L - T), (0, 0)), mode=mode)
        x, big, gt, cum = pad(x, "constant"), pad(big, "constant"), pad(gt, "constant"), pad(cum, "edge")
    Tp = x.shape[1]
    tt = min(Tp, 4 * L)
    c_ext = jnp.concatenate([C0, n0[..., None], jnp.zeros(C0.shape[:-1] + (DV_B - 1,), F32)], axis=-1)
    m_b = jnp.broadcast_to(m0[:, :, None, None], (B, H_B, 8, 128))
    lvl = jnp.asarray(_level_masks())
    tile = lambda w: pl.BlockSpec((None, tt, w), lambda b, c: (b, c, 0))
    state = lambda *s: pl.BlockSpec((None,) + s, lambda b, c: (b,) + (0,) * len(s))
    dmix = H_A * DV_A + H_B * DV_B
    out, S, c_new, m_new = pl.pallas_call(
        functools.partial(_even_seq_kernel, t_valid=t_valid),
        out_shape=(jax.ShapeDtypeStruct((B, Tp, D), F32),
                   jax.ShapeDtypeStruct((B, H_A, DK_A, DV_A), F32),
                   jax.ShapeDtypeStruct((B, H_B, DK_B, 2 * DV_B), F32),
                   jax.ShapeDtypeStruct((B, H_B, 8, 128), F32)),
        grid=(B, Tp // tt),
        in_specs=[tile(D), tile(N_BIG), tile(GATE_PAD), tile(GATE_PAD),
                  state(H_A, DK_A, DV_A), state(H_B, DK_B, 2 * DV_B), state(H_B, 8, 128),
                  _const_spec((1, DV_A)), _const_spec((1, H_B * DV_B)), _const_spec((N_LEVELS, L, L)),
                  _sel_spec((dmix, D), (j,))],
        out_specs=(tile(D), state(H_A, DK_A, DV_A), state(H_B, DK_B, 2 * DV_B), state(H_B, 8, 128)),
        scratch_shapes=[pltpu.VMEM((tt, dmix), BF16)],
        compiler_params=_params(("parallel", "arbitrary")),
        name="even_seq",
    )(x, big, gt, cum, S0, c_ext, m_b, gdn_norm_w.reshape(1, DV_A), mlstm_norm_w.reshape(1, H_B * DV_B), lvl, wout_all)
    return out[:, :T], (S, c_new[..., :DV_B], c_new[..., DV_B], m_new[:, :, 0, 0])


def _even_mixer_kernel(xa_ref, xb_ref, g_ref, cst_ref, s0_ref, c0_ref, m0_ref, wbig_ref, wsm_ref, convw_ref, gp_ref,
                       tri_ref, gnw_ref, mnw_ref, lvl_ref, wout_ref,
                       o_ref, ncst_ref, s_ref, c_ref, m_ref,
                       big_scr, gt_scr, cum_scr, hist_ref, mix_ref, *, nt, ntot):
    L = SEQ_CHUNK
    tm = xa_ref.shape[0]
    nc = big_scr.shape[1] // L
    t_valid = min(tm, L)
    HA, HB = range(H_A), range(H_B)
    s = pl.program_id(0)
    slot_a = s % 2
    slot_b = 1 - slot_a
    ja = jnp.minimum(s, ntot - 1) % nt
    jb = jnp.maximum(s - 1, 0) % nt
    big_a, gt_a, cum_a = big_scr.at[slot_a], gt_scr.at[slot_a], cum_scr.at[slot_a]
    big_b, gt_b, cum_b = big_scr.at[slot_b], gt_scr.at[slot_b], cum_scr.at[slot_b]

    @pl.when(s == 0)
    def _():
        big_scr[...] = jnp.zeros_like(big_scr)
        gt_scr[...] = jnp.zeros_like(gt_scr)
        cum_scr[...] = jnp.zeros_like(cum_scr)

    @pl.when(ja == 0)
    def _():
        hist_ref[5:8, :] = cst_ref[...]

    @pl.when(jb == 0)
    def _():
        s_ref[...] = s0_ref[...]
        c_ref[...] = c0_ref[...]
        m_ref[...] = m0_ref[...]

    xn = _rms(xa_ref[...], g_ref[...])
    xh, xl = _split_bf16(xn)

    def proj_task(c0, width):
        def run():
            r = _dot(xh, wbig_ref[:, c0:c0 + width])
            if c0 < OFF_Z:
                hist_ref[8:8 + tm, c0:c0 + width] = r
            elif OFF_QB <= c0 < OFF_KB:
                big_a[0:tm, c0:c0 + width] = r * (DK_B ** -0.5)
            else:
                big_a[0:tm, c0:c0 + width] = r
        return run

    def gate_task():
        wh, wl = _split_bf16(wsm_ref[...])
        y = _dot(xh, wh) + (_dot(xl, wh) + _dot(xh, wl)) + gp_ref[0:1, :]
        lane = lax.broadcasted_iota(jnp.int32, y.shape, 1)
        sp_pos, sp_neg = _softplus_parts(y)
        gt = jnp.where(lane < COL_G, jax.nn.sigmoid(y),
                       jnp.where(lane < COL_LOGI, -jnp.exp(gp_ref[1:2, :]) * sp_pos,
                                 jnp.where(lane < COL_LOGF, y, -sp_neg)))
        gt = jnp.where(lane < COL_LOGF + H_B, gt, 0.0)
        gt_a[0:tm, :] = gt
        g1 = gt.astype(BF16)
        r1 = gt - g1.astype(F32)
        g2 = r1.astype(BF16)
        g3 = (r1 - g2.astype(F32)).astype(BF16)
        tri = tri_ref[...]
        cum = _dot(tri, g1) + (_dot(tri, g2) + _dot(tri, g3))
        cum_a[0:tm, :] = cum
        if tm < L:
            cum_a[tm:L, :] = jnp.broadcast_to(cum[tm - 1:tm, :], (L - tm, GATE_PAD))

    def conv_block(blk):
        c0 = blk * 128
        acc = hist_ref[5:5 + tm, c0:c0 + 128] * convw_ref[0:1, c0:c0 + 128]
        for j in range(1, CONV_A):
            acc = acc + hist_ref[5 + j:5 + j + tm, c0:c0 + 128] * convw_ref[j:j + 1, c0:c0 + 128]
        v = acc * jax.nn.sigmoid(acc)
        if blk < 2 * H_A:
            v = v * lax.rsqrt(jnp.sum(v * v, axis=-1, keepdims=True) + EPS)
        if blk < H_A:
            v = v * (DK_A ** -0.5)
        big_a[0:tm, c0:c0 + 128] = v

    a_tasks = [proj_task(c0, 256) for c0 in range(0, N_BIG, 256)]
    a_tasks.insert(OFF_Z // 256, gate_task)

    def a_step(n=1):
        for _ in range(n):
            if a_tasks:
                a_tasks.pop(0)()

    row = lax.broadcasted_iota(jnp.int32, (L, L), 0)
    col = lax.broadcasted_iota(jnp.int32, (L, L), 1)
    incl = row >= col
    col_ok = incl if t_valid == L else (incl & (col < t_valid))
    eye = jnp.where(row == col, 1.0, 0.0)
    one_col = jnp.where(col == 0, 1.0, 0.0)

    def blk(c, off, h):
        return big_b[c * L:(c + 1) * L, off + h * 128:off + (h + 1) * 128]

    def mlstm_local(c):
        cum_c = cum_b[c * L:(c + 1) * L, :]
        cum_t, gt_t = cum_c.T, gt_b[c * L:(c + 1) * L, :].T
        p = dict(q=[blk(c, OFF_QB, h) for h in HB], k_t=[blk(c, OFF_KB, h).T for h in HB],
                 v_ext=[jnp.concatenate([blk(c, OFF_VB, h), one_col], axis=1) for h in HB],
                 b_c=[cum_c[:, COL_LOGF + h:COL_LOGF + h + 1] for h in HB])
        p["dmat"] = [jnp.where(col_ok, p["b_c"][h] - cum_t[COL_LOGF + h:COL_LOGF + h + 1, :]
                               + gt_t[COL_LOGI + h:COL_LOGI + h + 1, :], -jnp.inf) for h in HB]
        p["dmax"] = [jnp.max(p["dmat"][h], axis=-1, keepdims=True) for h in HB]
        p["s"] = [_bdot(p["q"][h], p["k_t"][h]) for h in HB]
        return p

    def mlstm_state(p, m_prev, c_cur):
        m_t = [jnp.maximum(p["b_c"][h] + m_prev[h], p["dmax"][h]) for h in HB]
        w_inter = [jnp.exp(p["b_c"][h] + m_prev[h] - m_t[h]) for h in HB]
        w_intra = [jnp.exp(p["dmat"][h] - m_t[h]) for h in HB]
        av = [_bdot(p["s"][h] * w_intra[h], p["v_ext"][h]) for h in HB]
        qc = [_bdot(p["q"][h], c_cur[h]) for h in HB]
        cu = [_bdot(p["k_t"][h] * w_intra[h][L - 1:L, :], p["v_ext"][h]) for h in HB]
        tot = [w_inter[h] * qc[h] + av[h] for h in HB]
        out = [tot[h][:, :DV_B] / jnp.maximum(jnp.abs(tot[h][:, DV_B:DV_B + 1]), jnp.exp(-m_t[h])) for h in HB]
        c_new = [w_inter[h][L - 1:L, :] * c_cur[h] + cu[h] for h in HB]
        return out, [m_t[h][L - 1:L, :] for h in HB], c_new

    def gdn_local(c):
        gt_c, cum_c = gt_b[c * L:(c + 1) * L, :], cum_b[c * L:(c + 1) * L, :]
        cum_t = cum_c.T
        p = dict(q=[blk(c, 0, h) for h in HA], k=[blk(c, H_A * DK_A, h) for h in HA],
                 v=[blk(c, 2 * H_A * DK_A, h) for h in HA],
                 beta=[gt_c[:, COL_BETA + h:COL_BETA + h + 1] for h in HA],
                 g_c=[cum_c[:, COL_G + h:COL_G + h + 1] for h in HA],
                 g_r=[cum_t[COL_G + h:COL_G + h + 1, :] for h in HA])
        p["gam"] = [jnp.where(incl, jnp.exp(jnp.where(incl, p["g_c"][h] - p["g_r"][h], 0.0)), 0.0) for h in HA]
        p["k_t"] = [p["k"][h].T for h in HA]
        p["qk_kk"] = [_bdot(jnp.concatenate([p["q"][h], p["k"][h]], axis=0), p["k_t"][h]) for h in HA]
        p["a_s"] = [p["beta"][h] * p["qk_kk"][h][L:] * p["gam"][h] for h in HA]
        p["d"] = [eye - p["a_s"][h] * lvl_ref[0] for h in HA]
        return p

    def gdn_solve(p):
        p["e_g"] = [jnp.exp(p["g_c"][h]) for h in HA]
        p["sol"] = [_bdot(p["d"][h], jnp.concatenate([p["beta"][h] * p["v"][h],
                                                      (p["beta"][h] * p["e_g"][h]) * p["k"][h]], axis=1)) for h in HA]

    def gdn_state_a(p, s_cur):
        wq = [_bdot(jnp.concatenate([p["sol"][h][:, DV_A:], p["q"][h]], axis=0), s_cur[h]) for h in HA]
        p["u"] = [p["sol"][h][:, :DV_A] - wq[h][:L] for h in HA]
        p["qs"] = [wq[h][L:] for h in HA]

    def gdn_state_b(p, s_cur):
        out = [p["e_g"][h] * p["qs"][h] + _bdot(p["qk_kk"][h][:L] * p["gam"][h], p["u"][h]) for h in HA]
        g_last = [p["g_r"][h][:, L - 1:L] for h in HA]
        s_new = [jnp.exp(g_last[h]) * s_cur[h] + _bdot(p["k_t"][h] * jnp.exp(g_last[h] - p["g_r"][h]), p["u"][h])
                 for h in HA]
        return out, s_new

    m_prev = [m_ref[h, 0:1, 0:1] for h in HB]
    c_cur = [c_ref[h] for h in HB]
    s_cur = [s_ref[h] for h in HA]
    m_out, g_out = [None] * nc, [None] * nc
    a_step(A_FRONT)
    per_slot = 1
    prev = []
    for g0 in range(0, nc, 2):
        cs = list(range(g0, min(g0 + 2, nc)))
        ml = [mlstm_local(c) for c in cs]
        gd = [gdn_local(c) for c in cs]
        riders = {1 + i: ("mlstm", ml[i], c) for i, c in enumerate(cs)}
        for i, (c, p) in enumerate(prev):
            riders[3 + 2 * i], riders[4 + 2 * i] = ("gdn_a", p, c), ("gdn_b", p, c)
        for lv in range(1, N_LEVELS):
            de = [[_bdot(p["d"][h], p["a_s"][h] * lvl_ref[lv]) for h in HA] for p in gd]
            kind, arg, c = riders.get(lv, (None, None, None))
            if kind == "mlstm":
                m_out[c], m_prev, c_cur = mlstm_state(arg, m_prev, c_cur)
            elif kind == "gdn_a":
                gdn_state_a(arg, s_cur)
            elif kind == "gdn_b":
                g_out[c], s_cur = gdn_state_b(arg, s_cur)
            a_step(per_slot)
            for p, de_p in zip(gd, de):
                p["d"] = [p["d"][h] - _bdot(de_p[h], p["d"][h]) for h in HA]
            a_step(per_slot)
        for p in gd:
            gdn_solve(p)
        prev = list(zip(cs, gd))
    a_step(len(a_tasks))
    for c, p in prev:
        gdn_state_a(p, s_cur)
        g_out[c], s_cur = gdn_state_b(p, s_cur)

    for h in HA:
        s_ref[h] = s_cur[h]
    for h in HB:
        c_ref[h] = c_cur[h]
        m_ref[h] = jnp.broadcast_to(m_prev[h], m_ref.shape[1:])

    for b in range(QKV_A // 128):
        conv_block(b)
    last = hist_ref[tm + 5:tm + 8, :]
    hist_ref[5:8, :] = last
    ncst_ref[...] = last

    for c in range(nc):
        for h in HA:
            o = g_out[c][h]
            o = o * lax.rsqrt(jnp.mean(o * o, axis=-1, keepdims=True) + EPS) * gnw_ref[...]
            z = blk(c, OFF_Z, h)
            mix_ref[c * L:(c + 1) * L, h * DV_A:(h + 1) * DV_A] = (o * (z * jax.nn.sigmoid(z))).astype(BF16)
        for h in HB:
            hh = m_out[c][h]
            hh = hh * lax.rsqrt(jnp.mean(hh * hh, axis=-1, keepdims=True) + EPS)
            hh = hh * mnw_ref[:, h * DV_B:(h + 1) * DV_B] * jax.nn.sigmoid(blk(c, OFF_OB, h))
            c0 = H_A * DV_A + h * DV_B
            mix_ref[c * L:(c + 1) * L, c0:c0 + DV_B] = hh.astype(BF16)
    o_ref[...] = xb_ref[...] + _dot(mix_ref[...], wout_ref[...])[0:tm]


def _even_mixer(x, g, conv_st, S0, C0, n0, m0, wbig_all, wsm_all, conv_w, gp, gdn_norm_w, mlstm_norm_w, wout_all, j,
                tile):
    B, T, D = x.shape
    L = SEQ_CHUNK
    tm = _row_tile(T, tile)
    assert tm % L == 0 or tm == T < L
    nt = T // tm
    ntot = B * nt
    rows = max(tm, L)
    dmix = H_A * DV_A + H_B * DV_B
    c_ext = jnp.concatenate([C0, n0[..., None], jnp.zeros(C0.shape[:-1] + (DV_B - 1,), F32)], axis=-1)
    m_b = jnp.broadcast_to(m0[:, :, None, None], (B, H_B, 8, 128))
    lvl = jnp.asarray(_level_masks())
    tri = jnp.asarray(_chunk_tri(tm), BF16)
    xt = x.reshape(ntot, tm, D)
    a_idx = lambda s: jnp.minimum(s, ntot - 1)
    b_idx = lambda s: jnp.maximum(s - 1, 0)
    seq_a = lambda *shape: pl.BlockSpec((None,) + shape, lambda s: (a_idx(s) // nt,) + (0,) * len(shape))
    seq_b = lambda *shape: pl.BlockSpec((None,) + shape, lambda s: (b_idx(s) // nt,) + (0,) * len(shape))
    out, ncst, S, c_new, m_new = pl.pallas_call(
        functools.partial(_even_mixer_kernel, nt=nt, ntot=ntot),
        out_shape=(jax.ShapeDtypeStruct((ntot, tm, D), F32),
                   jax.ShapeDtypeStruct((B, CONV_A - 1, QKV_A), F32),
                   jax.ShapeDtypeStruct((B, H_A, DK_A, DV_A), F32),
                   jax.ShapeDtypeStruct((B, H_B, DK_B, 2 * DV_B), F32),
                   jax.ShapeDtypeStruct((B, H_B, 8, 128), F32)),
        grid=(ntot + 1,),
        in_specs=[pl.BlockSpec((None, tm, D), lambda s: (a_idx(s), 0, 0)),
                  pl.BlockSpec((None, tm, D), lambda s: (b_idx(s), 0, 0)),
                  _const_spec((1, D)), seq_a(CONV_A - 1, QKV_A),
                  seq_b(H_A, DK_A, DV_A), seq_b(H_B, DK_B, 2 * DV_B), seq_b(H_B, 8, 128),
                  _sel_spec((D, N_BIG), (j,)), _sel_spec((D, GATE_PAD), (j,)), _const_spec((CONV_A, QKV_A)),
                  _const_spec((8, GATE_PAD)), _const_spec((tm, tm)), _const_spec((1, DV_A)),
                  _const_spec((1, H_B * DV_B)), _const_spec((N_LEVELS, L, L)), _sel_spec((dmix, D), (j,))],
        out_specs=(pl.BlockSpec((None, tm, D), lambda s: (b_idx(s), 0, 0)), seq_a(CONV_A - 1, QKV_A),
                   seq_b(H_A, DK_A, DV_A), seq_b(H_B, DK_B, 2 * DV_B), seq_b(H_B, 8, 128)),
        scratch_shapes=[pltpu.VMEM((2, rows, N_BIG), F32), pltpu.VMEM((2, rows, GATE_PAD), F32),
                        pltpu.VMEM((2, rows, GATE_PAD), F32), pltpu.VMEM((tm + 8, QKV_A), F32),
                        pltpu.VMEM((rows, dmix), BF16)],
        compiler_params=_params(("arbitrary",)),
        name="even_mixer",
    )(xt, xt, g, conv_st, S0, c_ext, m_b, wbig_all, wsm_all, conv_w, gp, tri, gdn_norm_w.reshape(1, DV_A),
      mlstm_norm_w.reshape(1, H_B * DV_B), lvl, wout_all)
    return out.reshape(B, T, D), (ncst, S, c_new[..., :DV_B], c_new[..., DV_B], m_new[:, :, 0, 0])


def _prep_weights(ffn_w_gu, ffn_w_d, w_ple, w_ple_gate, w_in_even, w_out_even, w_in_odd, w_out_odd):
    nbig = QKV_A + H_A * DV_A
    nmid = nbig + 2 * H_A
    nb_end = nmid + 4 * H_B * DK_B
    w_big = jnp.concatenate([w_in_even[:, :, :nbig], w_in_even[:, :, nmid:nb_end]], axis=-1).astype(BF16)
    w_sm = jnp.concatenate([w_in_even[:, :, nbig:nmid], w_in_even[:, :, nb_end:]], axis=-1)
    w_sm = jnp.pad(w_sm, ((0, 0), (0, 0), (0, GATE_PAD - w_sm.shape[-1])))
    return dict(wgu=ffn_w_gu.astype(BF16), wd=ffn_w_d.astype(BF16), wple=w_ple.astype(BF16),
                wgate=w_ple_gate.astype(BF16), w_big=w_big, w_sm=w_sm, w_out_even=w_out_even.astype(BF16),
                w_in_odd=w_in_odd.astype(BF16), w_out_odd=w_out_odd.astype(BF16))


def _gate_params(a_log, dt_bias, b_i, b_f):
    gp = jnp.zeros((8, GATE_PAD), F32)
    gp = gp.at[0, COL_G:COL_G + H_A].set(dt_bias).at[0, COL_LOGI:COL_LOGI + H_B].set(b_i)
    return gp.at[0, COL_LOGF:COL_LOGF + H_B].set(b_f).at[1, COL_G:COL_G + H_A].set(a_log)


def _run_trunk(x, p, init_even, init_odd, W, norm_g, final_norm, gdn_conv_w, gdn_a_log, gdn_dt_bias,
               gdn_norm_w, mlstm_b_i, mlstm_b_f, mlstm_norm_w, conv_c_w):
    B, T, D = x.shape
    depth = norm_g.shape[0]
    n = B * T
    x = x.reshape(n, D)
    new_even, new_odd = [], []
    gf = final_norm.reshape(1, D)
    for i in range(depth):
        j = i // 2
        g = lambda k: norm_g[i, k].reshape(1, D)
        x = _ffn(x, g(0), W["wgu"], W["wd"], (i, 0))
        if i % 2 == 0:
            conv_st, S0, C0, n0, m0 = init_even[j]
            gp = _gate_params(gdn_a_log[j], gdn_dt_bias[j], mlstm_b_i[j], mlstm_b_f[j])
            y, st = _even_mixer(x.reshape(B, T, D), g(1), conv_st, S0, C0, n0, m0, W["w_big"], W["w_sm"],
                                gdn_conv_w[j], gp, gdn_norm_w[j], mlstm_norm_w[j], W["w_out_even"], j, EVEN_TILE)
            new_even.append(st)
        else:
            y, st = _odd_mixer(x.reshape(B, T, D), g(1), init_odd[j], W["w_in_odd"], conv_c_w[j],
                               W["w_out_odd"], j)
            new_odd.append(st)
        x = y.reshape(n, D)
        x = _ffn(x, g(2), W["wgu"], W["wd"], (i, 1),
                 ple=(p.reshape(depth, n, -1), i, g(3), gf, W["wple"], W["wgate"], i == depth - 1))
    st_even = [jnp.stack([s[c] for s in new_even]) for c in range(5)]
    return (x.reshape(B, T, D), *st_even, jnp.stack(new_odd))


def kernel(x_prompt, x_sample, state_gdn_conv, state_gdn_S, state_mlstm_C, state_mlstm_n, state_mlstm_m,
           state_conv, p_prompt, p_sample, norm_g, final_norm, ffn_w_gu, ffn_w_d, w_ple, w_ple_gate,
           w_in_even, gdn_conv_w, gdn_a_log, gdn_dt_bias, gdn_norm_w, mlstm_b_i, mlstm_b_f, mlstm_norm_w,
           w_out_even, w_in_odd, conv_c_w, w_out_odd):
    Bp = x_prompt.shape[0]
    n_even, n_odd = state_gdn_S.shape[0], state_conv.shape[0]
    D = x_prompt.shape[-1]
    W = _prep_weights(ffn_w_gu, ffn_w_d, w_ple, w_ple_gate, w_in_even, w_out_even, w_in_odd, w_out_odd)
    zero_even = (jnp.zeros((Bp, CONV_A - 1, QKV_A), F32), jnp.zeros((Bp, H_A, DK_A, DV_A), F32),
                 jnp.zeros((Bp, H_B, DK_B, DV_B), F32), jnp.zeros((Bp, H_B, DK_B), F32),
                 jnp.zeros((Bp, H_B), F32))
    init_even_p = [zero_even] * n_even
    init_odd_p = [jnp.zeros((Bp, CONV_C - 1, D), F32)] * n_odd
    init_even_s = [(state_gdn_conv[j], state_gdn_S[j], state_mlstm_C[j], state_mlstm_n[j], state_mlstm_m[j])
                   for j in range(n_even)]
    init_odd_s = [state_conv[j] for j in range(n_odd)]
    rest = (norm_g, final_norm, gdn_conv_w, gdn_a_log, gdn_dt_bias, gdn_norm_w, mlstm_b_i, mlstm_b_f,
            mlstm_norm_w, conv_c_w)
    outs_p = _run_trunk(x_prompt, p_prompt, init_even_p, init_odd_p, W, *rest)
    outs_s = _run_trunk(x_sample, p_sample, init_even_s, init_odd_s, W, *rest)
    return (outs_p[0], outs_s[0]) + tuple(outs_p[1:]) + tuple(outs_s[1:])
```

```python
import functools

import jax
import jax.numpy as jnp
import numpy as np
from jax import lax
from jax.experimental import pallas as pl
from jax.experimental.pallas import tpu as pltpu

F32 = jnp.float32
BF16 = jnp.bfloat16

EPS = 1e-6
H_A, DK_A, DV_A, CONV_A = 4, 128, 128, 4
H_B, DK_B, DV_B = 4, 128, 128
CONV_C = 3
QKV_A = H_A * (2 * DK_A + DV_A)
FF_CHUNK = 256
GATE_PAD = 128
SEQ_CHUNK = 128
N_LEVELS = 7
EVEN_TILE = 256
A_FRONT = 10
VMEM_LIMIT = 56 * 1024 * 1024

OFF_Z = QKV_A
OFF_QB = OFF_Z + H_A * DV_A
OFF_KB = OFF_QB + H_B * DK_B
OFF_VB = OFF_KB + H_B * DK_B
OFF_OB = OFF_VB + H_B * DV_B
N_BIG = OFF_OB + H_B * DV_B
COL_BETA, COL_G, COL_LOGI, COL_LOGF = 0, H_A, 2 * H_A, 2 * H_A + H_B


def _rms(x, g):
    ms = jnp.mean(x * x, axis=-1, keepdims=True)
    return x * lax.rsqrt(ms + EPS) * g


def _dot(a, b):
    return jnp.dot(a, b, preferred_element_type=F32)


def _bdot(a, b):
    return jnp.dot(a.astype(BF16), b.astype(BF16), preferred_element_type=F32)


def _const_spec(shape):
    n = len(shape)
    return pl.BlockSpec(shape, lambda *_: (0,) * n, pipeline_mode=pl.Buffered(1))


def _sel_spec(shape, idx):
    n = len(shape)
    return pl.BlockSpec((None,) * len(idx) + tuple(shape), lambda *_: tuple(idx) + (0,) * n,
                        pipeline_mode=pl.Buffered(1))


def _params(sem):
    return pltpu.CompilerParams(dimension_semantics=sem, vmem_limit_bytes=VMEM_LIMIT)


def _row_tile(n, want):
    t = min(n, want)
    assert n % t == 0, (n, t)
    return t


def _swiglu_residual(x, g_ref, wgu_ref, wd_ref, act_ref):
    dff = wd_ref.shape[0]
    xn = _rms(x, g_ref[...]).astype(BF16)
    for c in range(dff // FF_CHUNK):
        lo = c * FF_CHUNK
        a = _dot(xn, wgu_ref[:, lo:lo + FF_CHUNK])
        b = _dot(xn, wgu_ref[:, dff + lo:dff + lo + FF_CHUNK])
        act_ref[:, lo:lo + FF_CHUNK] = (a * jax.nn.sigmoid(a) * b).astype(BF16)
    return x + 0.5 * _dot(act_ref[...], wd_ref[...])


def _ffn_kernel(x_ref, g_ref, wgu_ref, wd_ref, o_ref, act_ref):
    o_ref[...] = _swiglu_residual(x_ref[...], g_ref, wgu_ref, wd_ref, act_ref)


def _ffn_ple_kernel(x_ref, g_ref, wgu_ref, wd_ref, p_ref, gp_ref, gf_ref, wple_ref, wgate_ref, o_ref, act_ref, *,
                    final):
    y = _swiglu_residual(x_ref[...], g_ref, wgu_ref, wd_ref, act_ref)
    gate = jax.nn.sigmoid(_dot(_rms(y, gp_ref[...]).astype(BF16), wgate_ref[...]))
    y = y + _dot(p_ref[...].astype(BF16), wple_ref[...]) * gate
    if final:
        y = _rms(y, gf_ref[...])
    o_ref[...] = y


def _ffn(x, g, wgu_all, wd_all, idx, ple=None):
    n, d = x.shape
    dff = wd_all.shape[-2]
    tm = _row_tile(n, 512)
    row = lambda w: pl.BlockSpec((tm, w), lambda i: (i, 0))
    in_specs = [row(d), _const_spec((1, d)), _sel_spec((d, 2 * dff), idx), _sel_spec((dff, d), idx)]
    args = [x, g, wgu_all, wd_all]
    body = _ffn_kernel
    if ple is not None:
        p_all, layer, gp, gf, wple_all, wgate_all, final = ple
        dp = p_all.shape[-1]
        in_specs += [pl.BlockSpec((None, tm, dp), lambda i: (layer, i, 0)), _const_spec((1, d)), _const_spec((1, d)),
                     _sel_spec((dp, d), (layer,)), _sel_spec((d, d), (layer,))]
        args += [p_all, gp, gf, wple_all, wgate_all]
        body = functools.partial(_ffn_ple_kernel, final=final)
    return pl.pallas_call(
        body,
        out_shape=jax.ShapeDtypeStruct((n, d), F32),
        grid=(n // tm,),
        in_specs=in_specs,
        out_specs=row(d),
        scratch_shapes=[pltpu.VMEM((tm, dff), BF16)],
        compiler_params=_params(("parallel",)),
        name="ffn" if ple is None else "ffn_ple",
    )(*args)


def _odd_kernel(x_ref, g_ref, st_ref, win_ref, cw_ref, wout_ref, o_ref, newst_ref, hist_ref, y_ref):
    sb, tt, d = x_ref.shape
    x = x_ref[...].reshape(sb * tt, d)
    xn = _rms(x, g_ref[...]).astype(BF16)

    @pl.when(pl.program_id(1) == 0)
    def _():
        hist_ref[:, 6:8, :] = st_ref[...]

    for c0 in range(0, d, FF_CHUNK):
        c1 = c0 + FF_CHUNK
        h = _dot(xn, win_ref[:, c0:c1])
        bg = _dot(xn, win_ref[:, d + c0:d + c1])
        cg = _dot(xn, win_ref[:, 2 * d + c0:2 * d + c1])
        hist_ref[:, 8:8 + tt, c0:c1] = (cg * h).reshape(sb, tt, FF_CHUNK)
        conv = (hist_ref[:, 6:6 + tt, c0:c1] * cw_ref[0:1, c0:c1] + hist_ref[:, 7:7 + tt, c0:c1] * cw_ref[1:2, c0:c1]
                + hist_ref[:, 8:8 + tt, c0:c1] * cw_ref[2:3, c0:c1])
        y_ref[:, c0:c1] = (bg * conv.reshape(sb * tt, FF_CHUNK)).astype(BF16)
    o_ref[...] = (x + _dot(y_ref[...], wout_ref[...])).reshape(sb, tt, d)
    last = hist_ref[:, tt + 6:tt + 8, :]
    hist_ref[:, 6:8, :] = last
    newst_ref[...] = last


def _odd_mixer(x, g, st, win_all, cw, wout_all, j):
    b, t, d = x.shape
    tt = _row_tile(t, 512)
    sb = b if tt < 128 else 1
    return pl.pallas_call(
        _odd_kernel,
        out_shape=(jax.ShapeDtypeStruct((b, t, d), F32),
                   jax.ShapeDtypeStruct((b, CONV_C - 1, d), F32)),
        grid=(b // sb, t // tt),
        in_specs=[pl.BlockSpec((sb, tt, d), lambda i, k: (i, k, 0)),
                  _const_spec((1, d)),
                  pl.BlockSpec((sb, CONV_C - 1, d), lambda i, k: (i, 0, 0)),
                  _sel_spec((d, 3 * d), (j,)), _const_spec((CONV_C, d)), _sel_spec((d, d), (j,))],
        out_specs=(pl.BlockSpec((sb, tt, d), lambda i, k: (i, k, 0)),
                   pl.BlockSpec((sb, CONV_C - 1, d), lambda i, k: (i, 0, 0))),
        scratch_shapes=[pltpu.VMEM((sb, tt + 8, d), F32), pltpu.VMEM((sb * tt, d), BF16)],
        compiler_params=_params(("parallel", "arbitrary")),
        name="odd_mixer",
    )(x, g, st, win_all, cw, wout_all)


def _split_bf16(a):
    hi = a.astype(BF16)
    lo = (a - hi.astype(F32)).astype(BF16)
    return hi, lo


def _softplus_parts(y):
    t = jnp.log1p(jnp.exp(-jnp.abs(y)))
    return jnp.maximum(y, 0.0) + t, jnp.maximum(-y, 0.0) + t


def _chunk_tri(tm):
    t = np.arange(tm)
    same = (t[:, None] // SEQ_CHUNK) == (t[None, :] // SEQ_CHUNK)
    return (same & (t[:, None] >= t[None, :])).astype(np.float32)


def _level_masks():
    t = np.arange(SEQ_CHUNK)
    out = []
    for lv in range(N_LEVELS):
        b = 1 << lv
        tb, sb = t[:, None] // b, t[None, :] // b
        out.append(((tb % 2 == 1) & (sb == tb - 1)).astype(np.float32))
    return np.stack(out)


def _even_mixer_kernel(xa_ref, xb_ref, g_ref, cst_ref, s0_ref, c0_ref, m0_ref, wbig_ref, wsm_ref, convw_ref, gp_ref,
                       tri_ref, gnw_ref, mnw_ref, lvl_ref, wout_ref,
                       o_ref, ncst_ref, s_ref, c_ref, m_ref,
                       big_scr, gt_scr, cum_scr, hist_ref, mix_ref, *, nt, ntot):
    L = SEQ_CHUNK
    tm = xa_ref.shape[0]
    nc = big_scr.shape[1] // L
    t_valid = min(tm, L)
    HA, HB = range(H_A), range(H_B)
    s = pl.program_id(0)
    slot_a = s % 2
    slot_b = 1 - slot_a
    ja = jnp.minimum(s, ntot - 1) % nt
    jb = jnp.maximum(s - 1, 0) % nt
    big_a, gt_a, cum_a = big_scr.at[slot_a], gt_scr.at[slot_a], cum_scr.at[slot_a]
    big_b, gt_b, cum_b = big_scr.at[slot_b], gt_scr.at[slot_b], cum_scr.at[slot_b]

    @pl.when(s == 0)
    def _():
        big_scr[...] = jnp.zeros_like(big_scr)
        gt_scr[...] = jnp.zeros_like(gt_scr)
        cum_scr[...] = jnp.zeros_like(cum_scr)

    @pl.when(ja == 0)
    def _():
        hist_ref[5:8, :] = cst_ref[...]

    @pl.when(jb == 0)
    def _():
        s_ref[...] = s0_ref[...]
        c_ref[...] = c0_ref[...]
        m_ref[...] = m0_ref[...]

    xn = _rms(xa_ref[...], g_ref[...])
    xh, xl = _split_bf16(xn)

    def proj_task(c0, width):
        def run():
            r = _dot(xh, wbig_ref[:, c0:c0 + width])
            if c0 < OFF_Z:
                hist_ref[8:8 + tm, c0:c0 + width] = r
            elif OFF_QB <= c0 < OFF_KB:
                big_a[0:tm, c0:c0 + width] = r * (DK_B ** -0.5)
            else:
                big_a[0:tm, c0:c0 + width] = r
        return run

    def gate_task():
        r = _dot(xh, wsm_ref[...])
        y = r[:, :GATE_PAD] + (_dot(xl, wsm_ref[:, :GATE_PAD]) + r[:, GATE_PAD:]) + gp_ref[0:1, :]
        lane = lax.broadcasted_iota(jnp.int32, y.shape, 1)
        sp_pos, sp_neg = _softplus_parts(y)
        gt = jnp.where(lane < COL_G, jax.nn.sigmoid(y),
                       jnp.where(lane < COL_LOGI, -jnp.exp(gp_ref[1:2, :]) * sp_pos,
                                 jnp.where(lane < COL_LOGF, y, -sp_neg)))
        gt = jnp.where(lane < COL_LOGF + H_B, gt, 0.0)
        gt_a[0:tm, :] = gt
        g1 = gt.astype(BF16)
        r1 = gt - g1.astype(F32)
        g2 = r1.astype(BF16)
        g3 = (r1 - g2.astype(F32)).astype(BF16)
        tri = tri_ref[...]
        cum = _dot(tri, g1) + (_dot(tri, g2) + _dot(tri, g3))
        cum_a[0:tm, :] = cum
        if tm < L:
            cum_a[tm:L, :] = jnp.broadcast_to(cum[tm - 1:tm, :], (L - tm, GATE_PAD))

    def conv_block(blk):
        c0 = blk * 128
        acc = hist_ref[5:5 + tm, c0:c0 + 128] * convw_ref[0:1, c0:c0 + 128]
        for j in range(1, CONV_A):
            acc = acc + hist_ref[5 + j:5 + j + tm, c0:c0 + 128] * convw_ref[j:j + 1, c0:c0 + 128]
        v = acc * jax.nn.sigmoid(acc)
        if blk < 2 * H_A:
            v = v * lax.rsqrt(jnp.sum(v * v, axis=-1, keepdims=True) + EPS)
        if blk < H_A:
            v = v * (DK_A ** -0.5)
        big_a[0:tm, c0:c0 + 128] = v

    a_tasks = [proj_task(c0, 256) for c0 in range(0, N_BIG, 256)]
    a_tasks.insert(OFF_Z // 256, gate_task)

    def a_step(n=1):
        for _ in range(n):
            if a_tasks:
                a_tasks.pop(0)()

    row = lax.broadcasted_iota(jnp.int32, (L, L), 0)
    col = lax.broadcasted_iota(jnp.int32, (L, L), 1)
    incl = row >= col
    col_ok = incl if t_valid == L else (incl & (col < t_valid))
    eye = jnp.where(row == col, 1.0, 0.0)
    one_col = jnp.where(col == 0, 1.0, 0.0)

    def blk(c, off, h):
        return big_b[c * L:(c + 1) * L, off + h * 128:off + (h + 1) * 128]

    def mlstm_local(c):
        cum_c = cum_b[c * L:(c + 1) * L, :]
        cum_t, gt_t = cum_c.T, gt_b[c * L:(c + 1) * L, :].T
        p = dict(q=[blk(c, OFF_QB, h) for h in HB], k_t=[blk(c, OFF_KB, h).T for h in HB],
                 v_ext=[jnp.concatenate([blk(c, OFF_VB, h), one_col], axis=1) for h in HB],
                 b_c=[cum_c[:, COL_LOGF + h:COL_LOGF + h + 1] for h in HB])
        p["dmat"] = [jnp.where(col_ok, p["b_c"][h] - cum_t[COL_LOGF + h:COL_LOGF + h + 1, :]
                               + gt_t[COL_LOGI + h:COL_LOGI + h + 1, :], -jnp.inf) for h in HB]
        p["dmax"] = [jnp.max(p["dmat"][h], axis=-1, keepdims=True) for h in HB]
        p["s"] = [_bdot(p["q"][h], p["k_t"][h]) for h in HB]
        return p

    def mlstm_state(p, m_prev, c_cur):
        m_t = [jnp.maximum(p["b_c"][h] + m_prev[h], p["dmax"][h]) for h in HB]
        w_inter = [jnp.exp(p["b_c"][h] + m_prev[h] - m_t[h]) for h in HB]
        w_intra = [jnp.exp(p["dmat"][h] - m_t[h]) for h in HB]
        av = [_bdot(p["s"][h] * w_intra[h], p["v_ext"][h]) for h in HB]
        qc = [_bdot(p["q"][h], c_cur[h]) for h in HB]
        cu = [_bdot(p["k_t"][h] * w_intra[h][L - 1:L, :], p["v_ext"][h]) for h in HB]
        tot = [w_inter[h] * qc[h] + av[h] for h in HB]
        out = [tot[h][:, :DV_B] / jnp.maximum(jnp.abs(tot[h][:, DV_B:DV_B + 1]), jnp.exp(-m_t[h])) for h in HB]
        c_new = [w_inter[h][L - 1:L, :] * c_cur[h] + cu[h] for h in HB]
        return out, [m_t[h][L - 1:L, :] for h in HB], c_new

    def gdn_local(c):
        gt_c, cum_c = gt_b[c * L:(c + 1) * L, :], cum_b[c * L:(c + 1) * L, :]
        cum_t = cum_c.T
        p = dict(q=[blk(c, 0, h) for h in HA], k=[blk(c, H_A * DK_A, h) for h in HA],
                 v=[blk(c, 2 * H_A * DK_A, h) for h in HA],
                 beta=[gt_c[:, COL_BETA + h:COL_BETA + h + 1] for h in HA],
                 g_c=[cum_c[:, COL_G + h:COL_G + h + 1] for h in HA],
                 g_r=[cum_t[COL_G + h:COL_G + h + 1, :] for h in HA])
        p["gam"] = [jnp.where(incl, jnp.exp(jnp.where(incl, p["g_c"][h] - p["g_r"][h], 0.0)), 0.0) for h in HA]
        p["k_t"] = [p["k"][h].T for h in HA]
        p["qk_kk"] = [_bdot(jnp.concatenate([p["q"][h], p["k"][h]], axis=0), p["k_t"][h]) for h in HA]
        p["a_s"] = [p["beta"][h] * p["qk_kk"][h][L:] * p["gam"][h] for h in HA]
        p["d"] = [eye - p["a_s"][h] * lvl_ref[0] for h in HA]
        return p

    def gdn_solve(p):
        p["e_g"] = [jnp.exp(p["g_c"][h]) for h in HA]
        p["sol"] = [_bdot(p["d"][h], jnp.concatenate([p["beta"][h] * p["v"][h],
                                                      (p["beta"][h] * p["e_g"][h]) * p["k"][h]], axis=1)) for h in HA]

    def gdn_state_a(p, s_cur):
        wq = [_bdot(jnp.concatenate([p["sol"][h][:, DV_A:], p["q"][h]], axis=0), s_cur[h]) for h in HA]
        p["u"] = [p["sol"][h][:, :DV_A] - wq[h][:L] for h in HA]
        p["qs"] = [wq[h][L:] for h in HA]

    def gdn_state_b(p, s_cur):
        out = [p["e_g"][h] * p["qs"][h] + _bdot(p["qk_kk"][h][:L] * p["gam"][h], p["u"][h]) for h in HA]
        g_last = [p["g_r"][h][:, L - 1:L] for h in HA]
        s_new = [jnp.exp(g_last[h]) * s_cur[h] + _bdot(p["k_t"][h] * jnp.exp(g_last[h] - p["g_r"][h]), p["u"][h])
                 for h in HA]
        return out, s_new

    m_prev = [m_ref[h, 0:1, 0:1] for h in HB]
    c_cur = [c_ref[h] for h in HB]
    s_cur = [s_ref[h] for h in HA]
    m_out, g_out = [None] * nc, [None] * nc
    a_step(A_FRONT)
    prev = []
    for g0 in range(0, nc, 2):
        cs = list(range(g0, min(g0 + 2, nc)))
        ml = [mlstm_local(c) for c in cs]
        gd = [gdn_local(c) for c in cs]
        riders = {1 + i: ("mlstm", ml[i], c) for i, c in enumerate(cs)}
        for i, (c, p) in enumerate(prev):
            riders[3 + 2 * i], riders[4 + 2 * i] = ("gdn_a", p, c), ("gdn_b", p, c)
        for lv in range(1, N_LEVELS):
            de = [[_bdot(p["d"][h], p["a_s"][h] * lvl_ref[lv]) for h in HA] for p in gd]
            kind, arg, c = riders.get(lv, (None, None, None))
            if kind == "mlstm":
                m_out[c], m_prev, c_cur = mlstm_state(arg, m_prev, c_cur)
            elif kind == "gdn_a":
                gdn_state_a(arg, s_cur)
            elif kind == "gdn_b":
                g_out[c], s_cur = gdn_state_b(arg, s_cur)
            a_step()
            for p, de_p in zip(gd, de):
                p["d"] = [p["d"][h] - _bdot(de_p[h], p["d"][h]) for h in HA]
            a_step()
        for p in gd:
            gdn_solve(p)
        prev = list(zip(cs, gd))
    a_step(len(a_tasks))
    for c, p in prev:
        gdn_state_a(p, s_cur)
        g_out[c], s_cur = gdn_state_b(p, s_cur)

    for h in HA:
        s_ref[h] = s_cur[h]
    for h in HB:
        c_ref[h] = c_cur[h]
        m_ref[h] = jnp.broadcast_to(m_prev[h], m_ref.shape[1:])

    for b in range(QKV_A // 128):
        conv_block(b)
    last = hist_ref[tm + 5:tm + 8, :]
    hist_ref[5:8, :] = last
    ncst_ref[...] = last

    for c in range(nc):
        for h in HA:
            o = g_out[c][h]
            o = o * lax.rsqrt(jnp.mean(o * o, axis=-1, keepdims=True) + EPS) * gnw_ref[...]
            z = blk(c, OFF_Z, h)
            mix_ref[c * L:(c + 1) * L, h * DV_A:(h + 1) * DV_A] = (o * (z * jax.nn.sigmoid(z))).astype(BF16)
        for h in HB:
            hh = m_out[c][h]
            hh = hh * lax.rsqrt(jnp.mean(hh * hh, axis=-1, keepdims=True) + EPS)
            hh = hh * mnw_ref[:, h * DV_B:(h + 1) * DV_B] * jax.nn.sigmoid(blk(c, OFF_OB, h))
            c0 = H_A * DV_A + h * DV_B
            mix_ref[c * L:(c + 1) * L, c0:c0 + DV_B] = hh.astype(BF16)
    o_ref[...] = xb_ref[...] + _dot(mix_ref[...], wout_ref[...])[0:tm]


def _even_mixer(x, g, conv_st, S0, C0, n0, m0, wbig_all, wsm_all, conv_w, gp, gdn_norm_w, mlstm_norm_w, wout_all, j,
                tile):
    B, T, D = x.shape
    L = SEQ_CHUNK
    tm = _row_tile(T, tile)
    assert tm % L == 0 or tm == T < L
    nt = T // tm
    ntot = B * nt
    rows = max(tm, L)
    dmix = H_A * DV_A + H_B * DV_B
    c_ext = jnp.concatenate([C0, n0[..., None], jnp.zeros(C0.shape[:-1] + (DV_B - 1,), F32)], axis=-1)
    m_b = jnp.broadcast_to(m0[:, :, None, None], (B, H_B, 8, 128))
    lvl = jnp.asarray(_level_masks())
    tri = jnp.asarray(_chunk_tri(tm), BF16)
    xt = x.reshape(ntot, tm, D)
    a_idx = lambda s: jnp.minimum(s, ntot - 1)
    b_idx = lambda s: jnp.maximum(s - 1, 0)
    seq_a = lambda *shape: pl.BlockSpec((None,) + shape, lambda s: (a_idx(s) // nt,) + (0,) * len(shape))
    seq_b = lambda *shape: pl.BlockSpec((None,) + shape, lambda s: (b_idx(s) // nt,) + (0,) * len(shape))
    out, ncst, S, c_new, m_new = pl.pallas_call(
        functools.partial(_even_mixer_kernel, nt=nt, ntot=ntot),
        out_shape=(jax.ShapeDtypeStruct((ntot, tm, D), F32),
                   jax.ShapeDtypeStruct((B, CONV_A - 1, QKV_A), F32),
                   jax.ShapeDtypeStruct((B, H_A, DK_A, DV_A), F32),
                   jax.ShapeDtypeStruct((B, H_B, DK_B, 2 * DV_B), F32),
                   jax.ShapeDtypeStruct((B, H_B, 8, 128), F32)),
        grid=(ntot + 1,),
        in_specs=[pl.BlockSpec((None, tm, D), lambda s: (a_idx(s), 0, 0)),
                  pl.BlockSpec((None, tm, D), lambda s: (b_idx(s), 0, 0)),
                  _const_spec((1, D)), seq_a(CONV_A - 1, QKV_A),
                  seq_b(H_A, DK_A, DV_A), seq_b(H_B, DK_B, 2 * DV_B), seq_b(H_B, 8, 128),
                  _sel_spec((D, N_BIG), (j,)), _sel_spec((D, 2 * GATE_PAD), (j,)), _const_spec((CONV_A, QKV_A)),
                  _const_spec((8, GATE_PAD)), _const_spec((tm, tm)), _const_spec((1, DV_A)),
                  _const_spec((1, H_B * DV_B)), _const_spec((N_LEVELS, L, L)), _sel_spec((dmix, D), (j,))],
        out_specs=(pl.BlockSpec((None, tm, D), lambda s: (b_idx(s), 0, 0)), seq_a(CONV_A - 1, QKV_A),
                   seq_b(H_A, DK_A, DV_A), seq_b(H_B, DK_B, 2 * DV_B), seq_b(H_B, 8, 128)),
        scratch_shapes=[pltpu.VMEM((2, rows, N_BIG), F32), pltpu.VMEM((2, rows, GATE_PAD), F32),
                        pltpu.VMEM((2, rows, GATE_PAD), F32), pltpu.VMEM((tm + 8, QKV_A), F32),
                        pltpu.VMEM((rows, dmix), BF16)],
        compiler_params=_params(("arbitrary",)),
        name="even_mixer",
    )(xt, xt, g, conv_st, S0, c_ext, m_b, wbig_all, wsm_all, conv_w, gp, tri, gdn_norm_w.reshape(1, DV_A),
      mlstm_norm_w.reshape(1, H_B * DV_B), lvl, wout_all)
    return out.reshape(B, T, D), (ncst, S, c_new[..., :DV_B], c_new[..., DV_B], m_new[:, :, 0, 0])


def _prep_weights(ffn_w_gu, ffn_w_d, w_ple, w_ple_gate, w_in_even, w_out_even, w_in_odd, w_out_odd):
    nbig = QKV_A + H_A * DV_A
    nmid = nbig + 2 * H_A
    nb_end = nmid + 4 * H_B * DK_B
    w_big = jnp.concatenate([w_in_even[:, :, :nbig], w_in_even[:, :, nmid:nb_end]], axis=-1).astype(BF16)
    w_sm = jnp.concatenate([w_in_even[:, :, nbig:nmid], w_in_even[:, :, nb_end:]], axis=-1)
    w_sm = jnp.pad(w_sm, ((0, 0), (0, 0), (0, GATE_PAD - w_sm.shape[-1])))
    w_sm = jnp.concatenate(_split_bf16(w_sm), axis=-1)
    return dict(wgu=ffn_w_gu.astype(BF16), wd=ffn_w_d.astype(BF16), wple=w_ple.astype(BF16),
                wgate=w_ple_gate.astype(BF16), w_big=w_big, w_sm=w_sm, w_out_even=w_out_even.astype(BF16),
                w_in_odd=w_in_odd.astype(BF16), w_out_odd=w_out_odd.astype(BF16))


def _gate_params(a_log, dt_bias, b_i, b_f):
    gp = jnp.zeros((8, GATE_PAD), F32)
    gp = gp.at[0, COL_G:COL_G + H_A].set(dt_bias).at[0, COL_LOGI:COL_LOGI + H_B].set(b_i)
    return gp.at[0, COL_LOGF:COL_LOGF + H_B].set(b_f).at[1, COL_G:COL_G + H_A].set(a_log)


def _run_trunk(x, p, init_even, init_odd, W, norm_g, final_norm, gdn_conv_w, gdn_a_log, gdn_dt_bias,
               gdn_norm_w, mlstm_b_i, mlstm_b_f, mlstm_norm_w, conv_c_w):
    B, T, D = x.shape
    depth = norm_g.shape[0]
    n = B * T
    x = x.reshape(n, D)
    new_even, new_odd = [], []
    gf = final_norm.reshape(1, D)
    for i in range(depth):
        j = i // 2
        g = lambda k: norm_g[i, k].reshape(1, D)
        x = _ffn(x, g(0), W["wgu"], W["wd"], (i, 0))
        if i % 2 == 0:
            conv_st, S0, C0, n0, m0 = init_even[j]
            gp = _gate_params(gdn_a_log[j], gdn_dt_bias[j], mlstm_b_i[j], mlstm_b_f[j])
            y, st = _even_mixer(x.reshape(B, T, D), g(1), conv_st, S0, C0, n0, m0, W["w_big"], W["w_sm"],
                                gdn_conv_w[j], gp, gdn_norm_w[j], mlstm_norm_w[j], W["w_out_even"], j, EVEN_TILE)
            new_even.append(st)
        else:
            y, st = _odd_mixer(x.reshape(B, T, D), g(1), init_odd[j], W["w_in_odd"], conv_c_w[j],
                               W["w_out_odd"], j)
            new_odd.append(st)
        x = y.reshape(n, D)
        x = _ffn(x, g(2), W["wgu"], W["wd"], (i, 1),
                 ple=(p.reshape(depth, n, -1), i, g(3), gf, W["wple"], W["wgate"], i == depth - 1))
    st_even = [jnp.stack([s[c] for s in new_even]) for c in range(5)]
    return (x.reshape(B, T, D), *st_even, jnp.stack(new_odd))


def kernel(x_prompt, x_sample, state_gdn_conv, state_gdn_S, state_mlstm_C, state_mlstm_n, state_mlstm_m,
           state_conv, p_prompt, p_sample, norm_g, final_norm, ffn_w_gu, ffn_w_d, w_ple, w_ple_gate,
           w_in_even, gdn_conv_w, gdn_a_log, gdn_dt_bias, gdn_norm_w, mlstm_b_i, mlstm_b_f, mlstm_norm_w,
           w_out_even, w_in_odd, conv_c_w, w_out_odd):
    Bp = x_prompt.shape[0]
    n_even, n_odd = state_gdn_S.shape[0], state_conv.shape[0]
    D = x_prompt.shape[-1]
    W = _prep_weights(ffn_w_gu, ffn_w_d, w_ple, w_ple_gate, w_in_even, w_out_even, w_in_odd, w_out_odd)
    zero_even = (jnp.zeros((Bp, CONV_A - 1, QKV_A), F32), jnp.zeros((Bp, H_A, DK_A, DV_A), F32),
                 jnp.zeros((Bp, H_B, DK_B, DV_B), F32), jnp.zeros((Bp, H_B, DK_B), F32),
                 jnp.zeros((Bp, H_B), F32))
    init_even_p = [zero_even] * n_even
    init_odd_p = [jnp.zeros((Bp, CONV_C - 1, D), F32)] * n_odd
    init_even_s = [(state_gdn_conv[j], state_gdn_S[j], state_mlstm_C[j], state_mlstm_n[j], state_mlstm_m[j])
                   for j in range(n_even)]
    init_odd_s = [state_conv[j] for j in range(n_odd)]
    rest = (norm_g, final_norm, gdn_conv_w, gdn_a_log, gdn_dt_bias, gdn_norm_w, mlstm_b_i, mlstm_b_f,
            mlstm_norm_w, conv_c_w)
    outs_p = _run_trunk(x_prompt, p_prompt, init_even_p, init_odd_p, W, *rest)
    outs_s = _run_trunk(x_sample, p_sample, init_even_s, init_odd_s, W, *rest)
    return (outs_p[0], outs_s[0]) + tuple(outs_p[1:]) + tuple(outs_s[1:])
```

```python
import functools

import jax
import jax.numpy as jnp
import numpy as np
from jax import lax
from jax.experimental import pallas as pl
from jax.experimental.pallas import tpu as pltpu

F32 = jnp.float32
BF16 = jnp.bfloat16

EPS = 1e-6
H_A, DK_A, DV_A, CONV_A = 4, 128, 128, 4
H_B, DK_B, DV_B = 4, 128, 128
CONV_C = 3
QKV_A = H_A * (2 * DK_A + DV_A)
FF_CHUNK = 256
GATE_PAD = 128
SEQ_CHUNK = 128
N_LEVELS = 7
EVEN_TILE = 256
A_FRONT = 10
VMEM_LIMIT = 56 * 1024 * 1024

OFF_Z = QKV_A
OFF_QB = OFF_Z + H_A * DV_A
OFF_KB = OFF_QB + H_B * DK_B
OFF_VB = OFF_KB + H_B * DK_B
OFF_OB = OFF_VB + H_B * DV_B
N_BIG = OFF_OB + H_B * DV_B
COL_BETA, COL_G, COL_LOGI, COL_LOGF = 0, H_A, 2 * H_A, 2 * H_A + H_B


def _rms(x, g):
    ms = jnp.mean(x * x, axis=-1, keepdims=True)
    return x * lax.rsqrt(ms + EPS) * g


def _dot(a, b):
    return jnp.dot(a, b, preferred_element_type=F32)


def _bdot(a, b):
    return jnp.dot(a.astype(BF16), b.astype(BF16), preferred_element_type=F32)


def _const_spec(shape):
    n = len(shape)
    return pl.BlockSpec(shape, lambda *_: (0,) * n, pipeline_mode=pl.Buffered(1))


def _sel_spec(shape, idx):
    n = len(shape)
    return pl.BlockSpec((None,) * len(idx) + tuple(shape), lambda *_: tuple(idx) + (0,) * n,
                        pipeline_mode=pl.Buffered(1))


def _params(sem):
    return pltpu.CompilerParams(dimension_semantics=sem, vmem_limit_bytes=VMEM_LIMIT)


def _row_tile(n, want):
    t = min(n, want)
    assert n % t == 0, (n, t)
    return t


def _swiglu_residual(x_ref, xnext_ref, g_ref, wgu_ref, wd_ref, act_ref, xn_ref):
    i = pl.program_id(0)
    dff = wd_ref.shape[0]

    @pl.when(i == 0)
    def _():
        xn_ref[0] = _rms(x_ref[...], g_ref[...]).astype(BF16)

    xn = xn_ref[i % 2]
    for c in range(dff // FF_CHUNK):
        lo = c * FF_CHUNK
        a = _dot(xn, wgu_ref[:, lo:lo + FF_CHUNK])
        b = _dot(xn, wgu_ref[:, dff + lo:dff + lo + FF_CHUNK])
        act_ref[:, lo:lo + FF_CHUNK] = (a * jax.nn.sigmoid(a) * b).astype(BF16)
    y = x_ref[...] + 0.5 * _dot(act_ref[...], wd_ref[...])
    xn_ref[1 - i % 2] = _rms(xnext_ref[...], g_ref[...]).astype(BF16)
    return y


def _ffn_kernel(x_ref, xnext_ref, g_ref, wgu_ref, wd_ref, o_ref, act_ref, xn_ref):
    o_ref[...] = _swiglu_residual(x_ref, xnext_ref, g_ref, wgu_ref, wd_ref, act_ref, xn_ref)


def _ffn_ple_kernel(x_ref, xnext_ref, g_ref, wgu_ref, wd_ref, p_ref, gp_ref, gf_ref, wple_ref, wgate_ref,
                    o_ref, act_ref, xn_ref, *, final):
    y = _swiglu_residual(x_ref, xnext_ref, g_ref, wgu_ref, wd_ref, act_ref, xn_ref)
    pe = _dot(p_ref[...].astype(BF16), wple_ref[...])
    gate = jax.nn.sigmoid(_dot(_rms(y, gp_ref[...]).astype(BF16), wgate_ref[...]))
    y = y + pe * gate
    if final:
        y = _rms(y, gf_ref[...])
    o_ref[...] = y


def _ffn(x, g, wgu_all, wd_all, idx, ple=None):
    n, d = x.shape
    dff = wd_all.shape[-2]
    tm = _row_tile(n, 512)
    steps = n // tm
    row = lambda w: pl.BlockSpec((tm, w), lambda i: (i, 0))
    nxt = pl.BlockSpec((tm, d), lambda i: (jnp.minimum(i + 1, steps - 1), 0))
    in_specs = [row(d), nxt, _const_spec((1, d)), _sel_spec((d, 2 * dff), idx), _sel_spec((dff, d), idx)]
    args = [x, x, g, wgu_all, wd_all]
    body = _ffn_kernel
    if ple is not None:
        p_all, layer, gp, gf, wple_all, wgate_all, final = ple
        dp = p_all.shape[-1]
        in_specs += [pl.BlockSpec((None, tm, dp), lambda i: (layer, i, 0)), _const_spec((1, d)), _const_spec((1, d)),
                     _sel_spec((dp, d), (layer,)), _sel_spec((d, d), (layer,))]
        args += [p_all, gp, gf, wple_all, wgate_all]
        body = functools.partial(_ffn_ple_kernel, final=final)
    return pl.pallas_call(
        body,
        out_shape=jax.ShapeDtypeStruct((n, d), F32),
        grid=(steps,),
        in_specs=in_specs,
        out_specs=row(d),
        scratch_shapes=[pltpu.VMEM((tm, dff), BF16), pltpu.VMEM((2, tm, d), BF16)],
        compiler_params=_params(("arbitrary",)),
        name="ffn" if ple is None else "ffn_ple",
    )(*args)


def _odd_kernel(x_ref, g_ref, st_ref, win_ref, cw_ref, wout_ref, o_ref, newst_ref, hist_ref, y_ref):
    sb, tt, d = x_ref.shape
    x = x_ref[...].reshape(sb * tt, d)
    xn = _rms(x, g_ref[...]).astype(BF16)

    @pl.when(pl.program_id(1) == 0)
    def _():
        hist_ref[:, 6:8, :] = st_ref[...]

    for c0 in range(0, d, FF_CHUNK):
        c1 = c0 + FF_CHUNK
        h = _dot(xn, win_ref[:, c0:c1])
        bg = _dot(xn, win_ref[:, d + c0:d + c1])
        cg = _dot(xn, win_ref[:, 2 * d + c0:2 * d + c1])
        hist_ref[:, 8:8 + tt, c0:c1] = (cg * h).reshape(sb, tt, FF_CHUNK)
        conv = (hist_ref[:, 6:6 + tt, c0:c1] * cw_ref[0:1, c0:c1] + hist_ref[:, 7:7 + tt, c0:c1] * cw_ref[1:2, c0:c1]
                + hist_ref[:, 8:8 + tt, c0:c1] * cw_ref[2:3, c0:c1])
        y_ref[:, c0:c1] = (bg * conv.reshape(sb * tt, FF_CHUNK)).astype(BF16)
    o_ref[...] = (x + _dot(y_ref[...], wout_ref[...])).reshape(sb, tt, d)
    last = hist_ref[:, tt + 6:tt + 8, :]
    hist_ref[:, 6:8, :] = last
    newst_ref[...] = last


def _odd_mixer(x, g, st, win_all, cw, wout_all, j):
    b, t, d = x.shape
    tt = _row_tile(t, 512)
    sb = b if tt < 128 else 1
    return pl.pallas_call(
        _odd_kernel,
        out_shape=(jax.ShapeDtypeStruct((b, t, d), F32),
                   jax.ShapeDtypeStruct((b, CONV_C - 1, d), F32)),
        grid=(b // sb, t // tt),
        in_specs=[pl.BlockSpec((sb, tt, d), lambda i, k: (i, k, 0)),
                  _const_spec((1, d)),
                  pl.BlockSpec((sb, CONV_C - 1, d), lambda i, k: (i, 0, 0)),
                  _sel_spec((d, 3 * d), (j,)), _const_spec((CONV_C, d)), _sel_spec((d, d), (j,))],
        out_specs=(pl.BlockSpec((sb, tt, d), lambda i, k: (i, k, 0)),
                   pl.BlockSpec((sb, CONV_C - 1, d), lambda i, k: (i, 0, 0))),
        scratch_shapes=[pltpu.VMEM((sb, tt + 8, d), F32), pltpu.VMEM((sb * tt, d), BF16)],
        compiler_params=_params(("parallel", "arbitrary")),
        name="odd_mixer",
    )(x, g, st, win_all, cw, wout_all)


def _split_bf16(a):
    hi = a.astype(BF16)
    lo = (a - hi.astype(F32)).astype(BF16)
    return hi, lo


def _softplus_parts(y):
    t = jnp.log1p(jnp.exp(-jnp.abs(y)))
    return jnp.maximum(y, 0.0) + t, jnp.maximum(-y, 0.0) + t


def _chunk_tri(tm):
    t = np.arange(tm)
    same = (t[:, None] // SEQ_CHUNK) == (t[None, :] // SEQ_CHUNK)
    return (same & (t[:, None] >= t[None, :])).astype(np.float32)


def _level_masks():
    t = np.arange(SEQ_CHUNK)
    out = []
    for lv in range(N_LEVELS):
        b = 1 << lv
        tb, sb = t[:, None] // b, t[None, :] // b
        out.append(((tb % 2 == 1) & (sb == tb - 1)).astype(np.float32))
    return np.stack(out)


def _even_mixer_kernel(xa_ref, xb_ref, g_ref, cst_ref, s0_ref, c0_ref, m0_ref, wbig_ref, wsm_ref, convw_ref, gp_ref,
                       tri_ref, gnw_ref, mnw_ref, lvl_ref, wout_ref,
                       o_ref, ncst_ref, s_ref, c_ref, m_ref,
                       big_scr, gt_scr, cum_scr, hist_ref, mix_ref, *, nt, ntot):
    L = SEQ_CHUNK
    tm = xa_ref.shape[0]
    nc = big_scr.shape[1] // L
    t_valid = min(tm, L)
    HA, HB = range(H_A), range(H_B)
    s = pl.program_id(0)
    slot_a = s % 2
    slot_b = 1 - slot_a
    ja = jnp.minimum(s, ntot - 1) % nt
    jb = jnp.maximum(s - 1, 0) % nt
    big_a, gt_a, cum_a = big_scr.at[slot_a], gt_scr.at[slot_a], cum_scr.at[slot_a]
    big_b, gt_b, cum_b = big_scr.at[slot_b], gt_scr.at[slot_b], cum_scr.at[slot_b]

    @pl.when(s == 0)
    def _():
        big_scr[...] = jnp.zeros_like(big_scr)
        gt_scr[...] = jnp.zeros_like(gt_scr)
        cum_scr[...] = jnp.zeros_like(cum_scr)

    @pl.when(ja == 0)
    def _():
        hist_ref[5:8, :] = cst_ref[...]

    @pl.when(jb == 0)
    def _():
        s_ref[...] = s0_ref[...]
        c_ref[...] = c0_ref[...]
        m_ref[...] = m0_ref[...]

    xn = _rms(xa_ref[...], g_ref[...])
    xh, xl = _split_bf16(xn)

    def proj_task(c0, width):
        def run():
            r = _dot(xh, wbig_ref[:, c0:c0 + width])
            if c0 < OFF_Z:
                hist_ref[8:8 + tm, c0:c0 + width] = r
            elif OFF_QB <= c0 < OFF_KB:
                big_a[0:tm, c0:c0 + width] = r * (DK_B ** -0.5)
            else:
                big_a[0:tm, c0:c0 + width] = r
        return run

    def gate_task():
        r = _dot(xh, wsm_ref[...])
        y = r[:, :GATE_PAD] + (_dot(xl, wsm_ref[:, :GATE_PAD]) + r[:, GATE_PAD:]) + gp_ref[0:1, :]
        lane = lax.broadcasted_iota(jnp.int32, y.shape, 1)
        sp_pos, sp_neg = _softplus_parts(y)
        gt = jnp.where(lane < COL_G, jax.nn.sigmoid(y),
                       jnp.where(lane < COL_LOGI, -jnp.exp(gp_ref[1:2, :]) * sp_pos,
                                 jnp.where(lane < COL_LOGF, y, -sp_neg)))
        gt = jnp.where(lane < COL_LOGF + H_B, gt, 0.0)
        gt_a[0:tm, :] = gt
        g1 = gt.astype(BF16)
        r1 = gt - g1.astype(F32)
        g2 = r1.astype(BF16)
        g3 = (r1 - g2.astype(F32)).astype(BF16)
        tri = tri_ref[...]
        cum = _dot(tri, g1) + (_dot(tri, g2) + _dot(tri, g3))
        cum_a[0:tm, :] = cum
        if tm < L:
            cum_a[tm:L, :] = jnp.broadcast_to(cum[tm - 1:tm, :], (L - tm, GATE_PAD))

    def conv_block(blk):
        c0 = blk * 128
        acc = hist_ref[5:5 + tm, c0:c0 + 128] * convw_ref[0:1, c0:c0 + 128]
        for j in range(1, CONV_A):
            acc = acc + hist_ref[5 + j:5 + j + tm, c0:c0 + 128] * convw_ref[j:j + 1, c0:c0 + 128]
        v = acc * jax.nn.sigmoid(acc)
        if blk < 2 * H_A:
            v = v * lax.rsqrt(jnp.sum(v * v, axis=-1, keepdims=True) + EPS)
        if blk < H_A:
            v = v * (DK_A ** -0.5)
        big_a[0:tm, c0:c0 + 128] = v

    a_tasks = [proj_task(c0, 256) for c0 in range(0, N_BIG, 256)]
    a_tasks.insert(OFF_Z // 256, gate_task)

    def a_step(n=1):
        for _ in range(n):
            if a_tasks:
                a_tasks.pop(0)()

    row = lax.broadcasted_iota(jnp.int32, (L, L), 0)
    col = lax.broadcasted_iota(jnp.int32, (L, L), 1)
    incl = row >= col
    col_ok = incl if t_valid == L else (incl & (col < t_valid))
    eye = jnp.where(row == col, 1.0, 0.0)
    one_col = jnp.where(col == 0, 1.0, 0.0)

    def blk(c, off, h):
        return big_b[c * L:(c + 1) * L, off + h * 128:off + (h + 1) * 128]

    def mlstm_local(c):
        cum_c = cum_b[c * L:(c + 1) * L, :]
        cum_t, gt_t = cum_c.T, gt_b[c * L:(c + 1) * L, :].T
        p = dict(q=[blk(c, OFF_QB, h) for h in HB], k_t=[blk(c, OFF_KB, h).T for h in HB],
                 v_ext=[jnp.concatenate([blk(c, OFF_VB, h), one_col], axis=1) for h in HB],
                 b_c=[cum_c[:, COL_LOGF + h:COL_LOGF + h + 1] for h in HB])
        p["dmat"] = [jnp.where(col_ok, p["b_c"][h] - cum_t[COL_LOGF + h:COL_LOGF + h + 1, :]
                               + gt_t[COL_LOGI + h:COL_LOGI + h + 1, :], -jnp.inf) for h in HB]
        p["dmax"] = [jnp.max(p["dmat"][h], axis=-1, keepdims=True) for h in HB]
        p["s"] = [_bdot(p["q"][h], p["k_t"][h]) for h in HB]
        return p

    def mlstm_state(p, m_prev, c_cur):
        m_t = [jnp.maximum(p["b_c"][h] + m_prev[h], p["dmax"][h]) for h in HB]
        w_inter = [jnp.exp(p["b_c"][h] + m_prev[h] - m_t[h]) for h in HB]
        w_intra = [jnp.exp(p["dmat"][h] - m_t[h]) for h in HB]
        av = [_bdot(p["s"][h] * w_intra[h], p["v_ext"][h]) for h in HB]
        qc = [_bdot(p["q"][h], c_cur[h]) for h in HB]
        cu = [_bdot(p["k_t"][h] * w_intra[h][L - 1:L, :], p["v_ext"][h]) for h in HB]
        tot = [w_inter[h] * qc[h] + av[h] for h in HB]
        out = [tot[h][:, :DV_B] / jnp.maximum(jnp.abs(tot[h][:, DV_B:DV_B + 1]), jnp.exp(-m_t[h])) for h in HB]
        c_new = [w_inter[h][L - 1:L, :] * c_cur[h] + cu[h] for h in HB]
        return out, [m_t[h][L - 1:L, :] for h in HB], c_new

    def gdn_local(c):
        gt_c, cum_c = gt_b[c * L:(c + 1) * L, :], cum_b[c * L:(c + 1) * L, :]
        cum_t = cum_c.T
        p = dict(q=[blk(c, 0, h) for h in HA], k=[blk(c, H_A * DK_A, h) for h in HA],
                 v=[blk(c, 2 * H_A * DK_A, h) for h in HA],
                 beta=[gt_c[:, COL_BETA + h:COL_BETA + h + 1] for h in HA],
                 g_c=[cum_c[:, COL_G + h:COL_G + h + 1] for h in HA],
                 g_r=[cum_t[COL_G + h:COL_G + h + 1, :] for h in HA])
        p["gam"] = [jnp.where(incl, jnp.exp(jnp.where(incl, p["g_c"][h] - p["g_r"][h], 0.0)), 0.0) for h in HA]
        p["k_t"] = [p["k"][h].T for h in HA]
        p["qk_kk"] = [_bdot(jnp.concatenate([p["q"][h], p["k"][h]], axis=0), p["k_t"][h]) for h in HA]
        p["a_s"] = [p["beta"][h] * p["qk_kk"][h][L:] * p["gam"][h] for h in HA]
        p["d"] = [eye - p["a_s"][h] * lvl_ref[0] for h in HA]
        return p

    def gdn_solve(p):
        p["e_g"] = [jnp.exp(p["g_c"][h]) for h in HA]
        p["sol"] = [_bdot(p["d"][h], jnp.concatenate([p["beta"][h] * p["v"][h],
                                                      (p["beta"][h] * p["e_g"][h]) * p["k"][h]], axis=1)) for h in HA]

    def gdn_state_a(p, s_cur):
        wq = [_bdot(jnp.concatenate([p["sol"][h][:, DV_A:], p["q"][h]], axis=0), s_cur[h]) for h in HA]
        p["u"] = [p["sol"][h][:, :DV_A] - wq[h][:L] for h in HA]
        p["qs"] = [wq[h][L:] for h in HA]

    def gdn_state_b(p, s_cur):
        out = [p["e_g"][h] * p["qs"][h] + _bdot(p["qk_kk"][h][:L] * p["gam"][h], p["u"][h]) for h in HA]
        g_last = [p["g_r"][h][:, L - 1:L] for h in HA]
        s_new = [jnp.exp(g_last[h]) * s_cur[h] + _bdot(p["k_t"][h] * jnp.exp(g_last[h] - p["g_r"][h]), p["u"][h])
                 for h in HA]
        return out, s_new

    m_prev = [m_ref[h, 0:1, 0:1] for h in HB]
    c_cur = [c_ref[h] for h in HB]
    s_cur = [s_ref[h] for h in HA]
    m_out, g_out = [None] * nc, [None] * nc
    a_step(A_FRONT)
    prev = []
    for g0 in range(0, nc, 2):
        cs = list(range(g0, min(g0 + 2, nc)))
        ml = [mlstm_local(c) for c in cs]
        gd = [gdn_local(c) for c in cs]
        riders = {1 + i: ("mlstm", ml[i], c) for i, c in enumerate(cs)}
        for i, (c, p) in enumerate(prev):
            riders[3 + 2 * i], riders[4 + 2 * i] = ("gdn_a", p, c), ("gdn_b", p, c)
        for lv in range(1, N_LEVELS):
            de = [[_bdot(p["d"][h], p["a_s"][h] * lvl_ref[lv]) for h in HA] for p in gd]
            kind, arg, c = riders.get(lv, (None, None, None))
            if kind == "mlstm":
                m_out[c], m_prev, c_cur = mlstm_state(arg, m_prev, c_cur)
            elif kind == "gdn_a":
                gdn_state_a(arg, s_cur)
            elif kind == "gdn_b":
                g_out[c], s_cur = gdn_state_b(arg, s_cur)
            a_step()
            for p, de_p in zip(gd, de):
                p["d"] = [p["d"][h] - _bdot(de_p[h], p["d"][h]) for h in HA]
            a_step()
        for p in gd:
            gdn_solve(p)
        prev = list(zip(cs, gd))
    a_step(len(a_tasks))
    for c, p in prev:
        gdn_state_a(p, s_cur)
        g_out[c], s_cur = gdn_state_b(p, s_cur)

    for h in HA:
        s_ref[h] = s_cur[h]
    for h in HB:
        c_ref[h] = c_cur[h]
        m_ref[h] = jnp.broadcast_to(m_prev[h], m_ref.shape[1:])

    for b in range(QKV_A // 128):
        conv_block(b)
    last = hist_ref[tm + 5:tm + 8, :]
    hist_ref[5:8, :] = last
    ncst_ref[...] = last

    for c in range(nc):
        for h in HA:
            o = g_out[c][h]
            o = o * lax.rsqrt(jnp.mean(o * o, axis=-1, keepdims=True) + EPS) * gnw_ref[...]
            z = blk(c, OFF_Z, h)
            mix_ref[c * L:(c + 1) * L, h * DV_A:(h + 1) * DV_A] = (o * (z * jax.nn.sigmoid(z))).astype(BF16)
        for h in HB:
            hh = m_out[c][h]
            hh = hh * lax.rsqrt(jnp.mean(hh * hh, axis=-1, keepdims=True) + EPS)
            hh = hh * mnw_ref[:, h * DV_B:(h + 1) * DV_B] * jax.nn.sigmoid(blk(c, OFF_OB, h))
            c0 = H_A * DV_A + h * DV_B
            mix_ref[c * L:(c + 1) * L, c0:c0 + DV_B] = hh.astype(BF16)
    o_ref[...] = xb_ref[...] + _dot(mix_ref[...], wout_ref[...])[0:tm]


def _even_mixer(x, g, conv_st, S0, C0, n0, m0, wbig_all, wsm_all, conv_w, gp, gdn_norm_w, mlstm_norm_w, wout_all, j,
                tile):
    B, T, D = x.shape
    L = SEQ_CHUNK
    tm = _row_tile(T, tile)
    assert tm % L == 0 or tm == T < L
    nt = T // tm
    ntot = B * nt
    rows = max(tm, L)
    dmix = H_A * DV_A + H_B * DV_B
    c_ext = jnp.concatenate([C0, n0[..., None], jnp.zeros(C0.shape[:-1] + (DV_B - 1,), F32)], axis=-1)
    m_b = jnp.broadcast_to(m0[:, :, None, None], (B, H_B, 8, 128))
    lvl = jnp.asarray(_level_masks())
    tri = jnp.asarray(_chunk_tri(tm), BF16)
    xt = x.reshape(ntot, tm, D)
    a_idx = lambda s: jnp.minimum(s, ntot - 1)
    b_idx = lambda s: jnp.maximum(s - 1, 0)
    seq_a = lambda *shape: pl.BlockSpec((None,) + shape, lambda s: (a_idx(s) // nt,) + (0,) * len(shape))
    seq_b = lambda *shape: pl.BlockSpec((None,) + shape, lambda s: (b_idx(s) // nt,) + (0,) * len(shape))
    out, ncst, S, c_new, m_new = pl.pallas_call(
        functools.partial(_even_mixer_kernel, nt=nt, ntot=ntot),
        out_shape=(jax.ShapeDtypeStruct((ntot, tm, D), F32),
                   jax.ShapeDtypeStruct((B, CONV_A - 1, QKV_A), F32),
                   jax.ShapeDtypeStruct((B, H_A, DK_A, DV_A), F32),
                   jax.ShapeDtypeStruct((B, H_B, DK_B, 2 * DV_B), F32),
                   jax.ShapeDtypeStruct((B, H_B, 8, 128), F32)),
        grid=(ntot + 1,),
        in_specs=[pl.BlockSpec((None, tm, D), lambda s: (a_idx(s), 0, 0)),
                  pl.BlockSpec((None, tm, D), lambda s: (b_idx(s), 0, 0)),
                  _const_spec((1, D)), seq_a(CONV_A - 1, QKV_A),
                  seq_b(H_A, DK_A, DV_A), seq_b(H_B, DK_B, 2 * DV_B), seq_b(H_B, 8, 128),
                  _sel_spec((D, N_BIG), (j,)), _sel_spec((D, 2 * GATE_PAD), (j,)), _const_spec((CONV_A, QKV_A)),
                  _const_spec((8, GATE_PAD)), _const_spec((tm, tm)), _const_spec((1, DV_A)),
                  _const_spec((1, H_B * DV_B)), _const_spec((N_LEVELS, L, L)), _sel_spec((dmix, D), (j,))],
        out_specs=(pl.BlockSpec((None, tm, D), lambda s: (b_idx(s), 0, 0)), seq_a(CONV_A - 1, QKV_A),
                   seq_b(H_A, DK_A, DV_A), seq_b(H_B, DK_B, 2 * DV_B), seq_b(H_B, 8, 128)),
        scratch_shapes=[pltpu.VMEM((2, rows, N_BIG), F32), pltpu.VMEM((2, rows, GATE_PAD), F32),
                        pltpu.VMEM((2, rows, GATE_PAD), F32), pltpu.VMEM((tm + 8, QKV_A), F32),
                        pltpu.VMEM((rows, dmix), BF16)],
        compiler_params=_params(("arbitrary",)),
        name="even_mixer",
    )(xt, xt, g, conv_st, S0, c_ext, m_b, wbig_all, wsm_all, conv_w, gp, tri, gdn_norm_w.reshape(1, DV_A),
      mlstm_norm_w.reshape(1, H_B * DV_B), lvl, wout_all)
    return out.reshape(B, T, D), (ncst, S, c_new[..., :DV_B], c_new[..., DV_B], m_new[:, :, 0, 0])


def _prep_weights(ffn_w_gu, ffn_w_d, w_ple, w_ple_gate, w_in_even, w_out_even, w_in_odd, w_out_odd):
    nbig = QKV_A + H_A * DV_A
    nmid = nbig + 2 * H_A
    nb_end = nmid + 4 * H_B * DK_B
    w_big = jnp.concatenate([w_in_even[:, :, :nbig], w_in_even[:, :, nmid:nb_end]], axis=-1).astype(BF16)
    w_sm = jnp.concatenate([w_in_even[:, :, nbig:nmid], w_in_even[:, :, nb_end:]], axis=-1)
    w_sm = jnp.pad(w_sm, ((0, 0), (0, 0), (0, GATE_PAD - w_sm.shape[-1])))
    w_sm = jnp.concatenate(_split_bf16(w_sm), axis=-1)
    return dict(wgu=ffn_w_gu.astype(BF16), wd=ffn_w_d.astype(BF16), wple=w_ple.astype(BF16),
                wgate=w_ple_gate.astype(BF16), w_big=w_big, w_sm=w_sm, w_out_even=w_out_even.astype(BF16),
                w_in_odd=w_in_odd.astype(BF16), w_out_odd=w_out_odd.astype(BF16))


def _gate_params(a_log, dt_bias, b_i, b_f):
    gp = jnp.zeros((8, GATE_PAD), F32)
    gp = gp.at[0, COL_G:COL_G + H_A].set(dt_bias).at[0, COL_LOGI:COL_LOGI + H_B].set(b_i)
    return gp.at[0, COL_LOGF:COL_LOGF + H_B].set(b_f).at[1, COL_G:COL_G + H_A].set(a_log)


def _run_trunk(x, p, init_even, init_odd, W, norm_g, final_norm, gdn_conv_w, gdn_a_log, gdn_dt_bias,
               gdn_norm_w, mlstm_b_i, mlstm_b_f, mlstm_norm_w, conv_c_w):
    B, T, D = x.shape
    depth = norm_g.shape[0]
    n = B * T
    x = x.reshape(n, D)
    new_even, new_odd = [], []
    gf = final_norm.reshape(1, D)
    for i in range(depth):
        j = i // 2
        g = lambda k: norm_g[i, k].reshape(1, D)
        x = _ffn(x, g(0), W["wgu"], W["wd"], (i, 0))
        if i % 2 == 0:
            conv_st, S0, C0, n0, m0 = init_even[j]
            gp = _gate_params(gdn_a_log[j], gdn_dt_bias[j], mlstm_b_i[j], mlstm_b_f[j])
            y, st = _even_mixer(x.reshape(B, T, D), g(1), conv_st, S0, C0, n0, m0, W["w_big"], W["w_sm"],
                                gdn_conv_w[j], gp, gdn_norm_w[j], mlstm_norm_w[j], W["w_out_even"], j, EVEN_TILE)
            new_even.append(st)
        else:
            y, st = _odd_mixer(x.reshape(B, T, D), g(1), init_odd[j], W["w_in_odd"], conv_c_w[j],
                               W["w_out_odd"], j)
            new_odd.append(st)
        x = y.reshape(n, D)
        x = _ffn(x, g(2), W["wgu"], W["wd"], (i, 1),
                 ple=(p.reshape(depth, n, -1), i, g(3), gf, W["wple"], W["wgate"], i == depth - 1))
    st_even = [jnp.stack([s[c] for s in new_even]) for c in range(5)]
    return (x.reshape(B, T, D), *st_even, jnp.stack(new_odd))


def kernel(x_prompt, x_sample, state_gdn_conv, state_gdn_S, state_mlstm_C, state_mlstm_n, state_mlstm_m,
           state_conv, p_prompt, p_sample, norm_g, final_norm, ffn_w_gu, ffn_w_d, w_ple, w_ple_gate,
           w_in_even, gdn_conv_w, gdn_a_log, gdn_dt_bias, gdn_norm_w, mlstm_b_i, mlstm_b_f, mlstm_norm_w,
           w_out_even, w_in_odd, conv_c_w, w_out_odd):
    Bp = x_prompt.shape[0]
    n_even, n_odd = state_gdn_S.shape[0], state_conv.shape[0]
    D = x_prompt.shape[-1]
    W = _prep_weights(ffn_w_gu, ffn_w_d, w_ple, w_ple_gate, w_in_even, w_out_even, w_in_odd, w_out_odd)
    zero_even = (jnp.zeros((Bp, CONV_A - 1, QKV_A), F32), jnp.zeros((Bp, H_A, DK_A, DV_A), F32),
                 jnp.zeros((Bp, H_B, DK_B, DV_B), F32), jnp.zeros((Bp, H_B, DK_B), F32),
                 jnp.zeros((Bp, H_B), F32))
    init_even_p = [zero_even] * n_even
    init_odd_p = [jnp.zeros((Bp, CONV_C - 1, D), F32)] * n_odd
    init_even_s = [(state_gdn_conv[j], state_gdn_S[j], state_mlstm_C[j], state_mlstm_n[j], state_mlstm_m[j])
                   for j in range(n_even)]
    init_odd_s = [state_conv[j] for j in range(n_odd)]
    rest = (norm_g, final_norm, gdn_conv_w, gdn_a_log, gdn_dt_bias, gdn_norm_w, mlstm_b_i, mlstm_b_f,
            mlstm_norm_w, conv_c_w)
    outs_p = _run_trunk(x_prompt, p_prompt, init_even_p, init_odd_p, W, *rest)
    outs_s = _run_trunk(x_sample, p_sample, init_even_s, init_odd_s, W, *rest)
    return (outs_p[0], outs_s[0]) + tuple(outs_p[1:]) + tuple(outs_s[1:])
```

```python
import functools

import jax
import jax.numpy as jnp
import numpy as np
from jax import lax
from jax.experimental import pallas as pl
from jax.experimental.pallas import tpu as pltpu

F32 = jnp.float32
BF16 = jnp.bfloat16

EPS = 1e-6
H_A, DK_A, DV_A, CONV_A = 4, 128, 128, 4
H_B, DK_B, DV_B = 4, 128, 128
CONV_C = 3
QKV_A = H_A * (2 * DK_A + DV_A)
FF_CHUNK = 256
GATE_PAD = 128
SEQ_CHUNK = 128
N_LEVELS = 7
EVEN_TILE = 256
A_FRONT = 10
VMEM_LIMIT = 56 * 1024 * 1024

OFF_Z = QKV_A
OFF_QB = OFF_Z + H_A * DV_A
OFF_KB = OFF_QB + H_B * DK_B
OFF_VB = OFF_KB + H_B * DK_B
OFF_OB = OFF_VB + H_B * DV_B
N_BIG = OFF_OB + H_B * DV_B
COL_BETA, COL_G, COL_LOGI, COL_LOGF = 0, H_A, 2 * H_A, 2 * H_A + H_B


def _rms(x, g):
    ms = jnp.mean(x * x, axis=-1, keepdims=True)
    return x * lax.rsqrt(ms + EPS) * g


def _dot(a, b):
    return jnp.dot(a, b, preferred_element_type=F32)


def _bdot(a, b):
    return jnp.dot(a.astype(BF16), b.astype(BF16), preferred_element_type=F32)


def _const_spec(shape):
    n = len(shape)
    return pl.BlockSpec(shape, lambda *_: (0,) * n, pipeline_mode=pl.Buffered(1))


def _sel_spec(shape, idx):
    n = len(shape)
    return pl.BlockSpec((None,) * len(idx) + tuple(shape), lambda *_: tuple(idx) + (0,) * n,
                        pipeline_mode=pl.Buffered(1))


def _params(sem):
    return pltpu.CompilerParams(dimension_semantics=sem, vmem_limit_bytes=VMEM_LIMIT)


def _row_tile(n, want):
    t = min(n, want)
    assert n % t == 0, (n, t)
    return t


def _swiglu_residual(x, g_ref, wgu_ref, wd_ref, act_ref):
    dff = wd_ref.shape[0]
    xn = _rms(x, g_ref[...]).astype(BF16)
    for c in range(dff // FF_CHUNK):
        lo = c * FF_CHUNK
        a = _dot(xn, wgu_ref[:, lo:lo + FF_CHUNK])
        b = _dot(xn, wgu_ref[:, dff + lo:dff + lo + FF_CHUNK])
        act_ref[:, lo:lo + FF_CHUNK] = (a * jax.nn.sigmoid(a) * b).astype(BF16)
    return x + 0.5 * _dot(act_ref[...], wd_ref[...])


def _ffn_kernel(x_ref, g_ref, wgu_ref, wd_ref, o_ref, act_ref):
    o_ref[...] = _swiglu_residual(x_ref[...], g_ref, wgu_ref, wd_ref, act_ref)


def _ffn_ple_kernel(x_ref, g_ref, wgu_ref, wd_ref, p_ref, gp_ref, gf_ref, wple_ref, wgate_ref, o_ref, act_ref, *,
                    final):
    y = _swiglu_residual(x_ref[...], g_ref, wgu_ref, wd_ref, act_ref)
    gate = jax.nn.sigmoid(_dot(_rms(y, gp_ref[...]).astype(BF16), wgate_ref[...]))
    y = y + _dot(p_ref[...].astype(BF16), wple_ref[...]) * gate
    if final:
        y = _rms(y, gf_ref[...])
    o_ref[...] = y


def _ffn(x, g, wgu_all, wd_all, idx, ple=None):
    n, d = x.shape
    dff = wd_all.shape[-2]
    tm = _row_tile(n, 512)
    row = lambda w: pl.BlockSpec((tm, w), lambda i: (i, 0))
    in_specs = [row(d), _const_spec((1, d)), _sel_spec((d, 2 * dff), idx), _sel_spec((dff, d), idx)]
    args = [x, g, wgu_all, wd_all]
    body = _ffn_kernel
    if ple is not None:
        p_all, layer, gp, gf, wple_all, wgate_all, final = ple
        dp = p_all.shape[-1]
        in_specs += [pl.BlockSpec((None, tm, dp), lambda i: (layer, i, 0)), _const_spec((1, d)), _const_spec((1, d)),
                     _sel_spec((dp, d), (layer,)), _sel_spec((d, d), (layer,))]
        args += [p_all, gp, gf, wple_all, wgate_all]
        body = functools.partial(_ffn_ple_kernel, final=final)
    return pl.pallas_call(
        body,
        out_shape=jax.ShapeDtypeStruct((n, d), F32),
        grid=(n // tm,),
        in_specs=in_specs,
        out_specs=row(d),
        scratch_shapes=[pltpu.VMEM((tm, dff), BF16)],
        compiler_params=_params(("parallel",)),
        name="ffn" if ple is None else "ffn_ple",
    )(*args)


def _odd_kernel(x_ref, g_ref, st_ref, win_ref, cw_ref, wout_ref, o_ref, newst_ref, hist_ref, y_ref):
    sb, tt, d = x_ref.shape
    x = x_ref[...].reshape(sb * tt, d)
    xn = _rms(x, g_ref[...]).astype(BF16)

    @pl.when(pl.program_id(1) == 0)
    def _():
        hist_ref[:, 6:8, :] = st_ref[...]

    for c0 in range(0, d, FF_CHUNK):
        c1 = c0 + FF_CHUNK
        h = _dot(xn, win_ref[:, c0:c1])
        bg = _dot(xn, win_ref[:, d + c0:d + c1])
        cg = _dot(xn, win_ref[:, 2 * d + c0:2 * d + c1])
        hist_ref[:, 8:8 + tt, c0:c1] = (cg * h).reshape(sb, tt, FF_CHUNK)
        conv = (hist_ref[:, 6:6 + tt, c0:c1] * cw_ref[0:1, c0:c1] + hist_ref[:, 7:7 + tt, c0:c1] * cw_ref[1:2, c0:c1]
                + hist_ref[:, 8:8 + tt, c0:c1] * cw_ref[2:3, c0:c1])
        y_ref[:, c0:c1] = (bg * conv.reshape(sb * tt, FF_CHUNK)).astype(BF16)
    o_ref[...] = (x + _dot(y_ref[...], wout_ref[...])).reshape(sb, tt, d)
    last = hist_ref[:, tt + 6:tt + 8, :]
    hist_ref[:, 6:8, :] = last
    newst_ref[...] = last


def _odd_mixer(x, g, st, win_all, cw, wout_all, j):
    b, t, d = x.shape
    tt = _row_tile(t, 512)
    sb = b if tt < 128 else 1
    return pl.pallas_call(
        _odd_kernel,
        out_shape=(jax.ShapeDtypeStruct((b, t, d), F32),
                   jax.ShapeDtypeStruct((b, CONV_C - 1, d), F32)),
        grid=(b // sb, t // tt),
        in_specs=[pl.BlockSpec((sb, tt, d), lambda i, k: (i, k, 0)),
                  _const_spec((1, d)),
                  pl.BlockSpec((sb, CONV_C - 1, d), lambda i, k: (i, 0, 0)),
                  _sel_spec((d, 3 * d), (j,)), _const_spec((CONV_C, d)), _sel_spec((d, d), (j,))],
        out_specs=(pl.BlockSpec((sb, tt, d), lambda i, k: (i, k, 0)),
                   pl.BlockSpec((sb, CONV_C - 1, d), lambda i, k: (i, 0, 0))),
        scratch_shapes=[pltpu.VMEM((sb, tt + 8, d), F32), pltpu.VMEM((sb * tt, d), BF16)],
        compiler_params=_params(("parallel", "arbitrary")),
        name="odd_mixer",
    )(x, g, st, win_all, cw, wout_all)


def _split_bf16(a):
    hi = a.astype(BF16)
    lo = (a - hi.astype(F32)).astype(BF16)
    return hi, lo


def _softplus_parts(y):
    t = jnp.log1p(jnp.exp(-jnp.abs(y)))
    return jnp.maximum(y, 0.0) + t, jnp.maximum(-y, 0.0) + t


def _chunk_tri(tm):
    t = np.arange(tm)
    same = (t[:, None] // SEQ_CHUNK) == (t[None, :] // SEQ_CHUNK)
    return (same & (t[:, None] >= t[None, :])).astype(np.float32)


def _level_masks():
    t = np.arange(SEQ_CHUNK)
    out = []
    for lv in range(N_LEVELS):
        b = 1 << lv
        tb, sb = t[:, None] // b, t[None, :] // b
        out.append(((tb % 2 == 1) & (sb == tb - 1)).astype(np.float32))
    return np.stack(out)


def _even_mixer_kernel(*refs, nt, ntot):
    for parity in (0, 1):
        @pl.when(pl.program_id(0) % 2 == parity)
        def _():
            _even_mixer_step(*refs, nt=nt, ntot=ntot, slot_a=parity)


def _even_mixer_step(xa_ref, xb_ref, g_ref, cst_ref, s0_ref, c0_ref, m0_ref, wbig_ref, wsm_ref, convw_ref, gp_ref,
                     tri_ref, gnw_ref, mnw_ref, lvl_ref, wout_ref,
                     o_ref, ncst_ref, s_ref, c_ref, m_ref,
                     big_scr, gt_scr, cum_scr, hist_ref, mix_ref, *, nt, ntot, slot_a):
    L = SEQ_CHUNK
    tm = xa_ref.shape[0]
    nc = big_scr.shape[1] // L
    t_valid = min(tm, L)
    HA, HB = range(H_A), range(H_B)
    s = pl.program_id(0)
    slot_b = 1 - slot_a
    ja = jnp.minimum(s, ntot - 1) % nt
    jb = jnp.maximum(s - 1, 0) % nt
    big_a, gt_a, cum_a = big_scr.at[slot_a], gt_scr.at[slot_a], cum_scr.at[slot_a]
    big_b, gt_b, cum_b = big_scr.at[slot_b], gt_scr.at[slot_b], cum_scr.at[slot_b]

    @pl.when(s == 0)
    def _():
        big_scr[...] = jnp.zeros_like(big_scr)
        gt_scr[...] = jnp.zeros_like(gt_scr)
        cum_scr[...] = jnp.zeros_like(cum_scr)

    @pl.when(ja == 0)
    def _():
        hist_ref[5:8, :] = cst_ref[...]

    @pl.when(jb == 0)
    def _():
        s_ref[...] = s0_ref[...]
        c_ref[...] = c0_ref[...]
        m_ref[...] = m0_ref[...]

    xn = _rms(xa_ref[...], g_ref[...])
    xh, xl = _split_bf16(xn)

    def proj_task(c0, width):
        def run():
            r = _dot(xh, wbig_ref[:, c0:c0 + width])
            if c0 < OFF_Z:
                hist_ref[8:8 + tm, c0:c0 + width] = r
            elif OFF_QB <= c0 < OFF_KB:
                big_a[0:tm, c0:c0 + width] = r * (DK_B ** -0.5)
            else:
                big_a[0:tm, c0:c0 + width] = r
        return run

    def gate_task():
        r = _dot(xh, wsm_ref[...])
        y = r[:, :GATE_PAD] + (_dot(xl, wsm_ref[:, :GATE_PAD]) + r[:, GATE_PAD:]) + gp_ref[0:1, :]
        lane = lax.broadcasted_iota(jnp.int32, y.shape, 1)
        sp_pos, sp_neg = _softplus_parts(y)
        gt = jnp.where(lane < COL_G, jax.nn.sigmoid(y),
                       jnp.where(lane < COL_LOGI, -jnp.exp(gp_ref[1:2, :]) * sp_pos,
                                 jnp.where(lane < COL_LOGF, y, -sp_neg)))
        gt = jnp.where(lane < COL_LOGF + H_B, gt, 0.0)
        gt_a[0:tm, :] = gt
        g1 = gt.astype(BF16)
        r1 = gt - g1.astype(F32)
        g2 = r1.astype(BF16)
        g3 = (r1 - g2.astype(F32)).astype(BF16)
        tri = tri_ref[...]
        cum = _dot(tri, g1) + (_dot(tri, g2) + _dot(tri, g3))
        cum_a[0:tm, :] = cum
        if tm < L:
            cum_a[tm:L, :] = jnp.broadcast_to(cum[tm - 1:tm, :], (L - tm, GATE_PAD))

    def conv_block(blk):
        c0 = blk * 128
        acc = hist_ref[5:5 + tm, c0:c0 + 128] * convw_ref[0:1, c0:c0 + 128]
        for j in range(1, CONV_A):
            acc = acc + hist_ref[5 + j:5 + j + tm, c0:c0 + 128] * convw_ref[j:j + 1, c0:c0 + 128]
        v = acc * jax.nn.sigmoid(acc)
        if blk < 2 * H_A:
            v = v * lax.rsqrt(jnp.sum(v * v, axis=-1, keepdims=True) + EPS)
        if blk < H_A:
            v = v * (DK_A ** -0.5)
        big_a[0:tm, c0:c0 + 128] = v

    a_tasks = [proj_task(c0, 256) for c0 in range(0, N_BIG, 256)]
    a_tasks.insert(OFF_Z // 256, gate_task)

    def a_step(n=1):
        for _ in range(n):
            if a_tasks:
                a_tasks.pop(0)()

    row = lax.broadcasted_iota(jnp.int32, (L, L), 0)
    col = lax.broadcasted_iota(jnp.int32, (L, L), 1)
    incl = row >= col
    col_ok = incl if t_valid == L else (incl & (col < t_valid))
    eye = jnp.where(row == col, 1.0, 0.0)
    one_col = jnp.where(col == 0, 1.0, 0.0)

    def blk(c, off, h):
        return big_b[c * L:(c + 1) * L, off + h * 128:off + (h + 1) * 128]

    def mlstm_local(c):
        cum_c = cum_b[c * L:(c + 1) * L, :]
        cum_t, gt_t = cum_c.T, gt_b[c * L:(c + 1) * L, :].T
        p = dict(q=[blk(c, OFF_QB, h) for h in HB], k_t=[blk(c, OFF_KB, h).T for h in HB],
                 v_ext=[jnp.concatenate([blk(c, OFF_VB, h), one_col], axis=1) for h in HB],
                 b_c=[cum_c[:, COL_LOGF + h:COL_LOGF + h + 1] for h in HB])
        p["dmat"] = [jnp.where(col_ok, p["b_c"][h] - cum_t[COL_LOGF + h:COL_LOGF + h + 1, :]
                               + gt_t[COL_LOGI + h:COL_LOGI + h + 1, :], -jnp.inf) for h in HB]
        p["dmax"] = [jnp.max(p["dmat"][h], axis=-1, keepdims=True) for h in HB]
        p["s"] = [_bdot(p["q"][h], p["k_t"][h]) for h in HB]
        return p

    def mlstm_state(p, m_prev, c_cur):
        m_t = [jnp.maximum(p["b_c"][h] + m_prev[h], p["dmax"][h]) for h in HB]
        w_inter = [jnp.exp(p["b_c"][h] + m_prev[h] - m_t[h]) for h in HB]
        w_intra = [jnp.exp(p["dmat"][h] - m_t[h]) for h in HB]
        av = [_bdot(p["s"][h] * w_intra[h], p["v_ext"][h]) for h in HB]
        qc = [_bdot(p["q"][h], c_cur[h]) for h in HB]
        cu = [_bdot(p["k_t"][h] * w_intra[h][L - 1:L, :], p["v_ext"][h]) for h in HB]
        tot = [w_inter[h] * qc[h] + av[h] for h in HB]
        out = [tot[h][:, :DV_B] / jnp.maximum(jnp.abs(tot[h][:, DV_B:DV_B + 1]), jnp.exp(-m_t[h])) for h in HB]
        c_new = [w_inter[h][L - 1:L, :] * c_cur[h] + cu[h] for h in HB]
        return out, [m_t[h][L - 1:L, :] for h in HB], c_new

    def gdn_local(c):
        gt_c, cum_c = gt_b[c * L:(c + 1) * L, :], cum_b[c * L:(c + 1) * L, :]
        cum_t = cum_c.T
        p = dict(q=[blk(c, 0, h) for h in HA], k=[blk(c, H_A * DK_A, h) for h in HA],
                 v=[blk(c, 2 * H_A * DK_A, h) for h in HA],
                 beta=[gt_c[:, COL_BETA + h:COL_BETA + h + 1] for h in HA],
                 g_c=[cum_c[:, COL_G + h:COL_G + h + 1] for h in HA],
                 g_r=[cum_t[COL_G + h:COL_G + h + 1, :] for h in HA])
        p["gam"] = [jnp.where(incl, jnp.exp(jnp.where(incl, p["g_c"][h] - p["g_r"][h], 0.0)), 0.0) for h in HA]
        p["k_t"] = [p["k"][h].T for h in HA]
        p["qk_kk"] = [_bdot(jnp.concatenate([p["q"][h], p["k"][h]], axis=0), p["k_t"][h]) for h in HA]
        p["a_s"] = [p["beta"][h] * p["qk_kk"][h][L:] * p["gam"][h] for h in HA]
        p["d"] = [eye - p["a_s"][h] * lvl_ref[0] for h in HA]
        return p

    def gdn_solve(p):
        p["e_g"] = [jnp.exp(p["g_c"][h]) for h in HA]
        p["sol"] = [_bdot(p["d"][h], jnp.concatenate([p["beta"][h] * p["v"][h],
                                                      (p["beta"][h] * p["e_g"][h]) * p["k"][h]], axis=1)) for h in HA]

    def gdn_state_a(p, s_cur):
        wq = [_bdot(jnp.concatenate([p["sol"][h][:, DV_A:], p["q"][h]], axis=0), s_cur[h]) for h in HA]
        p["u"] = [p["sol"][h][:, :DV_A] - wq[h][:L] for h in HA]
        p["qs"] = [wq[h][L:] for h in HA]

    def gdn_state_b(p, s_cur):
        out = [p["e_g"][h] * p["qs"][h] + _bdot(p["qk_kk"][h][:L] * p["gam"][h], p["u"][h]) for h in HA]
        g_last = [p["g_r"][h][:, L - 1:L] for h in HA]
        s_new = [jnp.exp(g_last[h]) * s_cur[h] + _bdot(p["k_t"][h] * jnp.exp(g_last[h] - p["g_r"][h]), p["u"][h])
                 for h in HA]
        return out, s_new

    m_prev = [m_ref[h, 0:1, 0:1] for h in HB]
    c_cur = [c_ref[h] for h in HB]
    s_cur = [s_ref[h] for h in HA]
    m_out, g_out = [None] * nc, [None] * nc
    a_step(A_FRONT)
    prev = []
    for g0 in range(0, nc, 2):
        cs = list(range(g0, min(g0 + 2, nc)))
        ml = [mlstm_local(c) for c in cs]
        gd = [gdn_local(c) for c in cs]
        riders = {1 + i: ("mlstm", ml[i], c) for i, c in enumerate(cs)}
        for i, (c, p) in enumerate(prev):
            riders[3 + 2 * i], riders[4 + 2 * i] = ("gdn_a", p, c), ("gdn_b", p, c)
        for lv in range(1, N_LEVELS):
            de = [[_bdot(p["d"][h], p["a_s"][h] * lvl_ref[lv]) for h in HA] for p in gd]
            kind, arg, c = riders.get(lv, (None, None, None))
            if kind == "mlstm":
                m_out[c], m_prev, c_cur = mlstm_state(arg, m_prev, c_cur)
            elif kind == "gdn_a":
                gdn_state_a(arg, s_cur)
            elif kind == "gdn_b":
                g_out[c], s_cur = gdn_state_b(arg, s_cur)
            a_step()
            for p, de_p in zip(gd, de):
                p["d"] = [p["d"][h] - _bdot(de_p[h], p["d"][h]) for h in HA]
            a_step()
        for p in gd:
            gdn_solve(p)
        prev = list(zip(cs, gd))
    a_step(len(a_tasks))
    for c, p in prev:
        gdn_state_a(p, s_cur)
        g_out[c], s_cur = gdn_state_b(p, s_cur)

    for h in HA:
        s_ref[h] = s_cur[h]
    for h in HB:
        c_ref[h] = c_cur[h]
        m_ref[h] = jnp.broadcast_to(m_prev[h], m_ref.shape[1:])

    for b in range(QKV_A // 128):
        conv_block(b)
    last = hist_ref[tm + 5:tm + 8, :]
    hist_ref[5:8, :] = last
    ncst_ref[...] = last

    for c in range(nc):
        for h in HA:
            o = g_out[c][h]
            o = o * lax.rsqrt(jnp.mean(o * o, axis=-1, keepdims=True) + EPS) * gnw_ref[...]
            z = blk(c, OFF_Z, h)
            mix_ref[c * L:(c + 1) * L, h * DV_A:(h + 1) * DV_A] = (o * (z * jax.nn.sigmoid(z))).astype(BF16)
        for h in HB:
            hh = m_out[c][h]
            hh = hh * lax.rsqrt(jnp.mean(hh * hh, axis=-1, keepdims=True) + EPS)
            hh = hh * mnw_ref[:, h * DV_B:(h + 1) * DV_B] * jax.nn.sigmoid(blk(c, OFF_OB, h))
            c0 = H_A * DV_A + h * DV_B
            mix_ref[c * L:(c + 1) * L, c0:c0 + DV_B] = hh.astype(BF16)
    o_ref[...] = xb_ref[...] + _dot(mix_ref[...], wout_ref[...])[0:tm]


def _even_mixer(x, g, conv_st, S0, C0, n0, m0, wbig_all, wsm_all, conv_w, gp, gdn_norm_w, mlstm_norm_w, wout_all, j,
                tile):
    B, T, D = x.shape
    L = SEQ_CHUNK
    tm = _row_tile(T, tile)
    assert tm % L == 0 or tm == T < L
    nt = T // tm
    ntot = B * nt
    rows = max(tm, L)
    dmix = H_A * DV_A + H_B * DV_B
    c_ext = jnp.concatenate([C0, n0[..., None], jnp.zeros(C0.shape[:-1] + (DV_B - 1,), F32)], axis=-1)
    m_b = jnp.broadcast_to(m0[:, :, None, None], (B, H_B, 8, 128))
    lvl = jnp.asarray(_level_masks())
    tri = jnp.asarray(_chunk_tri(tm), BF16)
    xt = x.reshape(ntot, tm, D)
    a_idx = lambda s: jnp.minimum(s, ntot - 1)
    b_idx = lambda s: jnp.maximum(s - 1, 0)
    seq_a = lambda *shape: pl.BlockSpec((None,) + shape, lambda s: (a_idx(s) // nt,) + (0,) * len(shape))
    seq_b = lambda *shape: pl.BlockSpec((None,) + shape, lambda s: (b_idx(s) // nt,) + (0,) * len(shape))
    out, ncst, S, c_new, m_new = pl.pallas_call(
        functools.partial(_even_mixer_kernel, nt=nt, ntot=ntot),
        out_shape=(jax.ShapeDtypeStruct((ntot, tm, D), F32),
                   jax.ShapeDtypeStruct((B, CONV_A - 1, QKV_A), F32),
                   jax.ShapeDtypeStruct((B, H_A, DK_A, DV_A), F32),
                   jax.ShapeDtypeStruct((B, H_B, DK_B, 2 * DV_B), F32),
                   jax.ShapeDtypeStruct((B, H_B, 8, 128), F32)),
        grid=(ntot + 1,),
        in_specs=[pl.BlockSpec((None, tm, D), lambda s: (a_idx(s), 0, 0)),
                  pl.BlockSpec((None, tm, D), lambda s: (b_idx(s), 0, 0)),
                  _const_spec((1, D)), seq_a(CONV_A - 1, QKV_A),
                  seq_b(H_A, DK_A, DV_A), seq_b(H_B, DK_B, 2 * DV_B), seq_b(H_B, 8, 128),
                  _sel_spec((D, N_BIG), (j,)), _sel_spec((D, 2 * GATE_PAD), (j,)), _const_spec((CONV_A, QKV_A)),
                  _const_spec((8, GATE_PAD)), _const_spec((tm, tm)), _const_spec((1, DV_A)),
                  _const_spec((1, H_B * DV_B)), _const_spec((N_LEVELS, L, L)), _sel_spec((dmix, D), (j,))],
        out_specs=(pl.BlockSpec((None, tm, D), lambda s: (b_idx(s), 0, 0)), seq_a(CONV_A - 1, QKV_A),
                   seq_b(H_A, DK_A, DV_A), seq_b(H_B, DK_B, 2 * DV_B), seq_b(H_B, 8, 128)),
        scratch_shapes=[pltpu.VMEM((2, rows, N_BIG), F32), pltpu.VMEM((2, rows, GATE_PAD), F32),
                        pltpu.VMEM((2, rows, GATE_PAD), F32), pltpu.VMEM((tm + 8, QKV_A), F32),
                        pltpu.VMEM((rows, dmix), BF16)],
        compiler_params=_params(("arbitrary",)),
        name="even_mixer",
    )(xt, xt, g, conv_st, S0, c_ext, m_b, wbig_all, wsm_all, conv_w, gp, tri, gdn_norm_w.reshape(1, DV_A),
      mlstm_norm_w.reshape(1, H_B * DV_B), lvl, wout_all)
    return out.reshape(B, T, D), (ncst, S, c_new[..., :DV_B], c_new[..., DV_B], m_new[:, :, 0, 0])


def _prep_weights(ffn_w_gu, ffn_w_d, w_ple, w_ple_gate, w_in_even, w_out_even, w_in_odd, w_out_odd):
    nbig = QKV_A + H_A * DV_A
    nmid = nbig + 2 * H_A
    nb_end = nmid + 4 * H_B * DK_B
    w_big = jnp.concatenate([w_in_even[:, :, :nbig], w_in_even[:, :, nmid:nb_end]], axis=-1).astype(BF16)
    w_sm = jnp.concatenate([w_in_even[:, :, nbig:nmid], w_in_even[:, :, nb_end:]], axis=-1)
    w_sm = jnp.pad(w_sm, ((0, 0), (0, 0), (0, GATE_PAD - w_sm.shape[-1])))
    w_sm = jnp.concatenate(_split_bf16(w_sm), axis=-1)
    return dict(wgu=ffn_w_gu.astype(BF16), wd=ffn_w_d.astype(BF16), wple=w_ple.astype(BF16),
                wgate=w_ple_gate.astype(BF16), w_big=w_big, w_sm=w_sm, w_out_even=w_out_even.astype(BF16),
                w_in_odd=w_in_odd.astype(BF16), w_out_odd=w_out_odd.astype(BF16))


def _gate_params(a_log, dt_bias, b_i, b_f):
    gp = jnp.zeros((8, GATE_PAD), F32)
    gp = gp.at[0, COL_G:COL_G + H_A].set(dt_bias).at[0, COL_LOGI:COL_LOGI + H_B].set(b_i)
    return gp.at[0, COL_LOGF:COL_LOGF + H_B].set(b_f).at[1, COL_G:COL_G + H_A].set(a_log)


def _run_trunk(x, p, init_even, init_odd, W, norm_g, final_norm, gdn_conv_w, gdn_a_log, gdn_dt_bias,
               gdn_norm_w, mlstm_b_i, mlstm_b_f, mlstm_norm_w, conv_c_w):
    B, T, D = x.shape
    depth = norm_g.shape[0]
    n = B * T
    x = x.reshape(n, D)
    new_even, new_odd = [], []
    gf = final_norm.reshape(1, D)
    for i in range(depth):
        j = i // 2
        g = lambda k: norm_g[i, k].reshape(1, D)
        x = _ffn(x, g(0), W["wgu"], W["wd"], (i, 0))
        if i % 2 == 0:
            conv_st, S0, C0, n0, m0 = init_even[j]
            gp = _gate_params(gdn_a_log[j], gdn_dt_bias[j], mlstm_b_i[j], mlstm_b_f[j])
            y, st = _even_mixer(x.reshape(B, T, D), g(1), conv_st, S0, C0, n0, m0, W["w_big"], W["w_sm"],
                                gdn_conv_w[j], gp, gdn_norm_w[j], mlstm_norm_w[j], W["w_out_even"], j, EVEN_TILE)
            new_even.append(st)
        else:
            y, st = _odd_mixer(x.reshape(B, T, D), g(1), init_odd[j], W["w_in_odd"], conv_c_w[j],
                               W["w_out_odd"], j)
            new_odd.append(st)
        x = y.reshape(n, D)
        x = _ffn(x, g(2), W["wgu"], W["wd"], (i, 1),
                 ple=(p.reshape(depth, n, -1), i, g(3), gf, W["wple"], W["wgate"], i == depth - 1))
    st_even = [jnp.stack([s[c] for s in new_even]) for c in range(5)]
    return (x.reshape(B, T, D), *st_even, jnp.stack(new_odd))


def kernel(x_prompt, x_sample, state_gdn_conv, state_gdn_S, state_mlstm_C, state_mlstm_n, state_mlstm_m,
           state_conv, p_prompt, p_sample, norm_g, final_norm, ffn_w_gu, ffn_w_d, w_ple, w_ple_gate,
           w_in_even, gdn_conv_w, gdn_a_log, gdn_dt_bias, gdn_norm_w, mlstm_b_i, mlstm_b_f, mlstm_norm_w,
           w_out_even, w_in_odd, conv_c_w, w_out_odd):
    Bp = x_prompt.shape[0]
    n_even, n_odd = state_gdn_S.shape[0], state_conv.shape[0]
    D = x_prompt.shape[-1]
    W = _prep_weights(ffn_w_gu, ffn_w_d, w_ple, w_ple_gate, w_in_even, w_out_even, w_in_odd, w_out_odd)
    zero_even = (jnp.zeros((Bp, CONV_A - 1, QKV_A), F32), jnp.zeros((Bp, H_A, DK_A, DV_A), F32),
                 jnp.zeros((Bp, H_B, DK_B, DV_B), F32), jnp.zeros((Bp, H_B, DK_B), F32),
                 jnp.zeros((Bp, H_B), F32))
    init_even_p = [zero_even] * n_even
    init_odd_p = [jnp.zeros((Bp, CONV_C - 1, D), F32)] * n_odd
    init_even_s = [(state_gdn_conv[j], state_gdn_S[j], state_mlstm_C[j], state_mlstm_n[j], state_mlstm_m[j])
                   for j in range(n_even)]
    init_odd_s = [state_conv[j] for j in range(n_odd)]
    rest = (norm_g, final_norm, gdn_conv_w, gdn_a_log, gdn_dt_bias, gdn_norm_w, mlstm_b_i, mlstm_b_f,
            mlstm_norm_w, conv_c_w)
    outs_p = _run_trunk(x_prompt, p_prompt, init_even_p, init_odd_p, W, *rest)
    outs_s = _run_trunk(x_sample, p_sample, init_even_s, init_odd_s, W, *rest)
    return (outs_p[0], outs_s[0]) + tuple(outs_p[1:]) + tuple(outs_s[1:])
```

```python
import functools

import jax
import jax.numpy as jnp
import numpy as np
from jax import lax
from jax.experimental import pallas as pl
from jax.experimental.pallas import tpu as pltpu

F32 = jnp.float32
BF16 = jnp.bfloat16

EPS = 1e-6
H_A, DK_A, DV_A, CONV_A = 4, 128, 128, 4
H_B, DK_B, DV_B = 4, 128, 128
CONV_C = 3
QKV_A = H_A * (2 * DK_A + DV_A)
FF_CHUNK = 256
GATE_PAD = 128
SEQ_CHUNK = 128
N_LEVELS = 7
EVEN_TILE = 256
A_FRONT = 10
VMEM_LIMIT = 56 * 1024 * 1024

OFF_Z = QKV_A
OFF_QB = OFF_Z + H_A * DV_A
OFF_KB = OFF_QB + H_B * DK_B
OFF_VB = OFF_KB + H_B * DK_B
OFF_OB = OFF_VB + H_B * DV_B
N_BIG = OFF_OB + H_B * DV_B
COL_BETA, COL_G, COL_LOGI, COL_LOGF = 0, H_A, 2 * H_A, 2 * H_A + H_B


def _rms(x, g):
    ms = jnp.mean(x * x, axis=-1, keepdims=True)
    return x * lax.rsqrt(ms + EPS) * g


def _dot(a, b):
    return jnp.dot(a, b, preferred_element_type=F32)


def _bdot(a, b):
    return jnp.dot(a.astype(BF16), b.astype(BF16), preferred_element_type=F32)


def _const_spec(shape):
    n = len(shape)
    return pl.BlockSpec(shape, lambda *_: (0,) * n, pipeline_mode=pl.Buffered(1))


def _sel_spec(shape, idx):
    n = len(shape)
    return pl.BlockSpec((None,) * len(idx) + tuple(shape), lambda *_: tuple(idx) + (0,) * n,
                        pipeline_mode=pl.Buffered(1))


def _params(sem):
    return pltpu.CompilerParams(dimension_semantics=sem, vmem_limit_bytes=VMEM_LIMIT)


def _row_tile(n, want):
    t = min(n, want)
    assert n % t == 0, (n, t)
    return t


def _swiglu_residual(x, g_ref, wgu_ref, wd_ref, act_ref):
    dff = wd_ref.shape[0]
    xn = _rms(x, g_ref[...]).astype(BF16)
    for c in range(dff // FF_CHUNK):
        lo = c * FF_CHUNK
        a = _dot(xn, wgu_ref[:, lo:lo + FF_CHUNK])
        b = _dot(xn, wgu_ref[:, dff + lo:dff + lo + FF_CHUNK])
        act_ref[:, lo:lo + FF_CHUNK] = (a * jax.nn.sigmoid(a) * b).astype(BF16)
    return x + 0.5 * _dot(act_ref[...], wd_ref[...])


def _ffn_kernel(x_ref, g_ref, wgu_ref, wd_ref, o_ref, act_ref):
    o_ref[...] = _swiglu_residual(x_ref[...], g_ref, wgu_ref, wd_ref, act_ref)


def _ffn_ple_kernel(x_ref, g_ref, wgu_ref, wd_ref, p_ref, gp_ref, gf_ref, wple_ref, wgate_ref, o_ref, act_ref, *,
                    final):
    y = _swiglu_residual(x_ref[...], g_ref, wgu_ref, wd_ref, act_ref)
    gate = jax.nn.sigmoid(_dot(_rms(y, gp_ref[...]).astype(BF16), wgate_ref[...]))
    y = y + _dot(p_ref[...].astype(BF16), wple_ref[...]) * gate
    if final:
        y = _rms(y, gf_ref[...])
    o_ref[...] = y


def _ffn(x, g, wgu_all, wd_all, idx, ple=None):
    n, d = x.shape
    dff = wd_all.shape[-2]
    tm = _row_tile(n, 1024)
    row = lambda w: pl.BlockSpec((tm, w), lambda i: (i, 0))
    in_specs = [row(d), _const_spec((1, d)), _sel_spec((d, 2 * dff), idx), _sel_spec((dff, d), idx)]
    args = [x, g, wgu_all, wd_all]
    body = _ffn_kernel
    if ple is not None:
        p_all, layer, gp, gf, wple_all, wgate_all, final = ple
        dp = p_all.shape[-1]
        in_specs += [pl.BlockSpec((None, tm, dp), lambda i: (layer, i, 0)), _const_spec((1, d)), _const_spec((1, d)),
                     _sel_spec((dp, d), (layer,)), _sel_spec((d, d), (layer,))]
        args += [p_all, gp, gf, wple_all, wgate_all]
        body = functools.partial(_ffn_ple_kernel, final=final)
    return pl.pallas_call(
        body,
        out_shape=jax.ShapeDtypeStruct((n, d), F32),
        grid=(n // tm,),
        in_specs=in_specs,
        out_specs=row(d),
        scratch_shapes=[pltpu.VMEM((tm, dff), BF16)],
        compiler_params=_params(("parallel",)),
        name="ffn" if ple is None else "ffn_ple",
    )(*args)


def _odd_kernel(x_ref, g_ref, st_ref, win_ref, cw_ref, wout_ref, o_ref, newst_ref, hist_ref, y_ref):
    sb, tt, d = x_ref.shape
    x = x_ref[...].reshape(sb * tt, d)
    xn = _rms(x, g_ref[...]).astype(BF16)

    @pl.when(pl.program_id(1) == 0)
    def _():
        hist_ref[:, 6:8, :] = st_ref[...]

    for c0 in range(0, d, FF_CHUNK):
        c1 = c0 + FF_CHUNK
        h = _dot(xn, win_ref[:, c0:c1])
        bg = _dot(xn, win_ref[:, d + c0:d + c1])
        cg = _dot(xn, win_ref[:, 2 * d + c0:2 * d + c1])
        hist_ref[:, 8:8 + tt, c0:c1] = (cg * h).reshape(sb, tt, FF_CHUNK)
        conv = (hist_ref[:, 6:6 + tt, c0:c1] * cw_ref[0:1, c0:c1] + hist_ref[:, 7:7 + tt, c0:c1] * cw_ref[1:2, c0:c1]
                + hist_ref[:, 8:8 + tt, c0:c1] * cw_ref[2:3, c0:c1])
        y_ref[:, c0:c1] = (bg * conv.reshape(sb * tt, FF_CHUNK)).astype(BF16)
    o_ref[...] = (x + _dot(y_ref[...], wout_ref[...])).reshape(sb, tt, d)
    last = hist_ref[:, tt + 6:tt + 8, :]
    hist_ref[:, 6:8, :] = last
    newst_ref[...] = last


def _odd_mixer(x, g, st, win_all, cw, wout_all, j):
    b, t, d = x.shape
    tt = _row_tile(t, 1024)
    sb = b if tt < 128 else 1
    return pl.pallas_call(
        _odd_kernel,
        out_shape=(jax.ShapeDtypeStruct((b, t, d), F32),
                   jax.ShapeDtypeStruct((b, CONV_C - 1, d), F32)),
        grid=(b // sb, t // tt),
        in_specs=[pl.BlockSpec((sb, tt, d), lambda i, k: (i, k, 0)),
                  _const_spec((1, d)),
                  pl.BlockSpec((sb, CONV_C - 1, d), lambda i, k: (i, 0, 0)),
                  _sel_spec((d, 3 * d), (j,)), _const_spec((CONV_C, d)), _sel_spec((d, d), (j,))],
        out_specs=(pl.BlockSpec((sb, tt, d), lambda i, k: (i, k, 0)),
                   pl.BlockSpec((sb, CONV_C - 1, d), lambda i, k: (i, 0, 0))),
        scratch_shapes=[pltpu.VMEM((sb, tt + 8, d), F32), pltpu.VMEM((sb * tt, d), BF16)],
        compiler_params=_params(("parallel", "arbitrary")),
        name="odd_mixer",
    )(x, g, st, win_all, cw, wout_all)


def _split_bf16(a):
    hi = a.astype(BF16)
    lo = (a - hi.astype(F32)).astype(BF16)
    return hi, lo


def _softplus_parts(y):
    t = jnp.log1p(jnp.exp(-jnp.abs(y)))
    return jnp.maximum(y, 0.0) + t, jnp.maximum(-y, 0.0) + t


def _chunk_tri(tm):
    t = np.arange(tm)
    same = (t[:, None] // SEQ_CHUNK) == (t[None, :] // SEQ_CHUNK)
    return (same & (t[:, None] >= t[None, :])).astype(np.float32)


def _level_masks():
    t = np.arange(SEQ_CHUNK)
    out = []
    for lv in range(N_LEVELS):
        b = 1 << lv
        tb, sb = t[:, None] // b, t[None, :] // b
        out.append(((tb % 2 == 1) & (sb == tb - 1)).astype(np.float32))
    return np.stack(out)


def _even_mixer_kernel(*refs, nt, ntot):
    for parity in (0, 1):
        @pl.when(pl.program_id(0) % 2 == parity)
        def _():
            _even_mixer_step(*refs, nt=nt, ntot=ntot, slot_a=parity)


def _even_mixer_step(xa_ref, xb_ref, g_ref, cst_ref, s0_ref, c0_ref, m0_ref, wbig_ref, wsm_ref, convw_ref, gp_ref,
                     tri_ref, gnw_ref, mnw_ref, lvl_ref, wout_ref,
                     o_ref, ncst_ref, s_ref, c_ref, m_ref,
                     big_scr, gt_scr, cum_scr, hist_ref, mix_ref, *, nt, ntot, slot_a):
    L = SEQ_CHUNK
    tm = xa_ref.shape[0]
    nc = big_scr.shape[1] // L
    t_valid = min(tm, L)
    HA, HB = range(H_A), range(H_B)
    s = pl.program_id(0)
    slot_b = 1 - slot_a
    ja = jnp.minimum(s, ntot - 1) % nt
    jb = jnp.maximum(s - 1, 0) % nt
    big_a, gt_a, cum_a = big_scr.at[slot_a], gt_scr.at[slot_a], cum_scr.at[slot_a]
    big_b, gt_b, cum_b = big_scr.at[slot_b], gt_scr.at[slot_b], cum_scr.at[slot_b]

    @pl.when(s == 0)
    def _():
        big_scr[...] = jnp.zeros_like(big_scr)
        gt_scr[...] = jnp.zeros_like(gt_scr)
        cum_scr[...] = jnp.zeros_like(cum_scr)

    @pl.when(ja == 0)
    def _():
        hist_ref[5:8, :] = cst_ref[...]

    @pl.when(jb == 0)
    def _():
        s_ref[...] = s0_ref[...]
        c_ref[...] = c0_ref[...]
        m_ref[...] = m0_ref[...]

    xn = _rms(xa_ref[...], g_ref[...])
    xh, xl = _split_bf16(xn)

    def proj_task(c0, width):
        def run():
            r = _dot(xh, wbig_ref[:, c0:c0 + width])
            if c0 < OFF_Z:
                hist_ref[8:8 + tm, c0:c0 + width] = r
            elif OFF_QB <= c0 < OFF_KB:
                big_a[0:tm, c0:c0 + width] = r * (DK_B ** -0.5)
            else:
                big_a[0:tm, c0:c0 + width] = r
        return run

    def gate_task():
        r = _dot(xh, wsm_ref[...])
        y = r[:, :GATE_PAD] + (_dot(xl, wsm_ref[:, :GATE_PAD]) + r[:, GATE_PAD:]) + gp_ref[0:1, :]
        lane = lax.broadcasted_iota(jnp.int32, y.shape, 1)
        sp_pos, sp_neg = _softplus_parts(y)
        gt = jnp.where(lane < COL_G, jax.nn.sigmoid(y),
                       jnp.where(lane < COL_LOGI, -jnp.exp(gp_ref[1:2, :]) * sp_pos,
                                 jnp.where(lane < COL_LOGF, y, -sp_neg)))
        gt = jnp.where(lane < COL_LOGF + H_B, gt, 0.0)
        gt_a[0:tm, :] = gt
        g1 = gt.astype(BF16)
        r1 = gt - g1.astype(F32)
        g2 = r1.astype(BF16)
        g3 = (r1 - g2.astype(F32)).astype(BF16)
        tri = tri_ref[...]
        cum = _dot(tri, g1) + (_dot(tri, g2) + _dot(tri, g3))
        cum_a[0:tm, :] = cum
        if tm < L:
            cum_a[tm:L, :] = jnp.broadcast_to(cum[tm - 1:tm, :], (L - tm, GATE_PAD))

    def conv_block(blk):
        c0 = blk * 128
        acc = hist_ref[5:5 + tm, c0:c0 + 128] * convw_ref[0:1, c0:c0 + 128]
        for j in range(1, CONV_A):
            acc = acc + hist_ref[5 + j:5 + j + tm, c0:c0 + 128] * convw_ref[j:j + 1, c0:c0 + 128]
        v = acc * jax.nn.sigmoid(acc)
        if blk < 2 * H_A:
            v = v * lax.rsqrt(jnp.sum(v * v, axis=-1, keepdims=True) + EPS)
        if blk < H_A:
            v = v * (DK_A ** -0.5)
        big_a[0:tm, c0:c0 + 128] = v

    a_tasks = [proj_task(c0, 256) for c0 in range(0, N_BIG, 256)]
    a_tasks.insert(OFF_Z // 256, gate_task)

    def a_step(n=1):
        for _ in range(n):
            if a_tasks:
                a_tasks.pop(0)()

    row = lax.broadcasted_iota(jnp.int32, (L, L), 0)
    col = lax.broadcasted_iota(jnp.int32, (L, L), 1)
    incl = row >= col
    col_ok = incl if t_valid == L else (incl & (col < t_valid))
    eye = jnp.where(row == col, 1.0, 0.0)
    one_col = jnp.where(col == 0, 1.0, 0.0)

    def blk(c, off, h):
        return big_b[c * L:(c + 1) * L, off + h * 128:off + (h + 1) * 128]

    def mlstm_local(c):
        cum_c = cum_b[c * L:(c + 1) * L, :]
        cum_t, gt_t = cum_c.T, gt_b[c * L:(c + 1) * L, :].T
        p = dict(q=[blk(c, OFF_QB, h) for h in HB], k_t=[blk(c, OFF_KB, h).T for h in HB],
                 v_ext=[jnp.concatenate([blk(c, OFF_VB, h), one_col], axis=1) for h in HB],
                 b_c=[cum_c[:, COL_LOGF + h:COL_LOGF + h + 1] for h in HB])
        p["dmat"] = [jnp.where(col_ok, p["b_c"][h] - cum_t[COL_LOGF + h:COL_LOGF + h + 1, :]
                               + gt_t[COL_LOGI + h:COL_LOGI + h + 1, :], -jnp.inf) for h in HB]
        p["dmax"] = [jnp.max(p["dmat"][h], axis=-1, keepdims=True) for h in HB]
        p["s"] = [_bdot(p["q"][h], p["k_t"][h]) for h in HB]
        return p

    def mlstm_state(p, m_prev, c_cur):
        m_t = [jnp.maximum(p["b_c"][h] + m_prev[h], p["dmax"][h]) for h in HB]
        w_inter = [jnp.exp(p["b_c"][h] + m_prev[h] - m_t[h]) for h in HB]
        w_intra = [jnp.exp(p["dmat"][h] - m_t[h]) for h in HB]
        av = [_bdot(p["s"][h] * w_intra[h], p["v_ext"][h]) for h in HB]
        qc = [_bdot(p["q"][h], c_cur[h]) for h in HB]
        cu = [_bdot(p["k_t"][h] * w_intra[h][L - 1:L, :], p["v_ext"][h]) for h in HB]
        tot = [w_inter[h] * qc[h] + av[h] for h in HB]
        out = [tot[h][:, :DV_B] / jnp.maximum(jnp.abs(tot[h][:, DV_B:DV_B + 1]), jnp.exp(-m_t[h])) for h in HB]
        c_new = [w_inter[h][L - 1:L, :] * c_cur[h] + cu[h] for h in HB]
        return out, [m_t[h][L - 1:L, :] for h in HB], c_new

    def gdn_local(c):
        gt_c, cum_c = gt_b[c * L:(c + 1) * L, :], cum_b[c * L:(c + 1) * L, :]
        cum_t = cum_c.T
        p = dict(q=[blk(c, 0, h) for h in HA], k=[blk(c, H_A * DK_A, h) for h in HA],
                 v=[blk(c, 2 * H_A * DK_A, h) for h in HA],
                 beta=[gt_c[:, COL_BETA + h:COL_BETA + h + 1] for h in HA],
                 g_c=[cum_c[:, COL_G + h:COL_G + h + 1] for h in HA],
                 g_r=[cum_t[COL_G + h:COL_G + h + 1, :] for h in HA])
        p["gam"] = [jnp.where(incl, jnp.exp(jnp.where(incl, p["g_c"][h] - p["g_r"][h], 0.0)), 0.0) for h in HA]
        p["k_t"] = [p["k"][h].T for h in HA]
        p["qk_kk"] = [_bdot(jnp.concatenate([p["q"][h], p["k"][h]], axis=0), p["k_t"][h]) for h in HA]
        p["a_s"] = [p["beta"][h] * p["qk_kk"][h][L:] * p["gam"][h] for h in HA]
        p["d"] = [eye - p["a_s"][h] * lvl_ref[0] for h in HA]
        return p

    def gdn_solve(p):
        p["e_g"] = [jnp.exp(p["g_c"][h]) for h in HA]
        p["sol"] = [_bdot(p["d"][h], jnp.concatenate([p["beta"][h] * p["v"][h],
                                                      (p["beta"][h] * p["e_g"][h]) * p["k"][h]], axis=1)) for h in HA]

    def gdn_state_a(p, s_cur):
        wq = [_bdot(jnp.concatenate([p["sol"][h][:, DV_A:], p["q"][h]], axis=0), s_cur[h]) for h in HA]
        p["u"] = [p["sol"][h][:, :DV_A] - wq[h][:L] for h in HA]
        p["qs"] = [wq[h][L:] for h in HA]

    def gdn_state_b(p, s_cur):
        out = [p["e_g"][h] * p["qs"][h] + _bdot(p["qk_kk"][h][:L] * p["gam"][h], p["u"][h]) for h in HA]
        g_last = [p["g_r"][h][:, L - 1:L] for h in HA]
        s_new = [jnp.exp(g_last[h]) * s_cur[h] + _bdot(p["k_t"][h] * jnp.exp(g_last[h] - p["g_r"][h]), p["u"][h])
                 for h in HA]
        return out, s_new

    m_prev = [m_ref[h, 0:1, 0:1] for h in HB]
    c_cur = [c_ref[h] for h in HB]
    s_cur = [s_ref[h] for h in HA]
    m_out, g_out = [None] * nc, [None] * nc
    a_step(A_FRONT)
    prev = []
    for g0 in range(0, nc, 2):
        cs = list(range(g0, min(g0 + 2, nc)))
        ml = [mlstm_local(c) for c in cs]
        gd = [gdn_local(c) for c in cs]
        riders = {1 + i: ("mlstm", ml[i], c) for i, c in enumerate(cs)}
        for i, (c, p) in enumerate(prev):
            riders[3 + 2 * i], riders[4 + 2 * i] = ("gdn_a", p, c), ("gdn_b", p, c)
        for lv in range(1, N_LEVELS):
            de = [[_bdot(p["d"][h], p["a_s"][h] * lvl_ref[lv]) for h in HA] for p in gd]
            kind, arg, c = riders.get(lv, (None, None, None))
            if kind == "mlstm":
                m_out[c], m_prev, c_cur = mlstm_state(arg, m_prev, c_cur)
            elif kind == "gdn_a":
                gdn_state_a(arg, s_cur)
            elif kind == "gdn_b":
                g_out[c], s_cur = gdn_state_b(arg, s_cur)
            a_step()
            for p, de_p in zip(gd, de):
                p["d"] = [p["d"][h] - _bdot(de_p[h], p["d"][h]) for h in HA]
            a_step()
        for p in gd:
            gdn_solve(p)
        prev = list(zip(cs, gd))
    a_step(len(a_tasks))
    for c, p in prev:
        gdn_state_a(p, s_cur)
        g_out[c], s_cur = gdn_state_b(p, s_cur)

    for h in HA:
        s_ref[h] = s_cur[h]
    for h in HB:
        c_ref[h] = c_cur[h]
        m_ref[h] = jnp.broadcast_to(m_prev[h], m_ref.shape[1:])

    for b in range(QKV_A // 128):
        conv_block(b)
    last = hist_ref[tm + 5:tm + 8, :]
    hist_ref[5:8, :] = last
    ncst_ref[...] = last

    for c in range(nc):
        for h in HA:
            o = g_out[c][h]
            o = o * lax.rsqrt(jnp.mean(o * o, axis=-1, keepdims=True) + EPS) * gnw_ref[...]
            z = blk(c, OFF_Z, h)
            mix_ref[c * L:(c + 1) * L, h * DV_A:(h + 1) * DV_A] = (o * (z * jax.nn.sigmoid(z))).astype(BF16)
        for h in HB:
            hh = m_out[c][h]
            hh = hh * lax.rsqrt(jnp.mean(hh * hh, axis=-1, keepdims=True) + EPS)
            hh = hh * mnw_ref[:, h * DV_B:(h + 1) * DV_B] * jax.nn.sigmoid(blk(c, OFF_OB, h))
            c0 = H_A * DV_A + h * DV_B
            mix_ref[c * L:(c + 1) * L, c0:c0 + DV_B] = hh.astype(BF16)
    o_ref[...] = xb_ref[...] + _dot(mix_ref[...], wout_ref[...])[0:tm]


def _even_mixer(x, g, conv_st, S0, C0, n0, m0, wbig_all, wsm_all, conv_w, gp, gdn_norm_w, mlstm_norm_w, wout_all, j,
                tile):
    B, T, D = x.shape
    L = SEQ_CHUNK
    tm = _row_tile(T, tile)
    assert tm % L == 0 or tm == T < L
    nt = T // tm
    ntot = B * nt
    rows = max(tm, L)
    dmix = H_A * DV_A + H_B * DV_B
    c_ext = jnp.concatenate([C0, n0[..., None], jnp.zeros(C0.shape[:-1] + (DV_B - 1,), F32)], axis=-1)
    m_b = jnp.broadcast_to(m0[:, :, None, None], (B, H_B, 8, 128))
    lvl = jnp.asarray(_level_masks())
    tri = jnp.asarray(_chunk_tri(tm), BF16)
    xt = x.reshape(ntot, tm, D)
    a_idx = lambda s: jnp.minimum(s, ntot - 1)
    b_idx = lambda s: jnp.maximum(s - 1, 0)
    seq_a = lambda *shape: pl.BlockSpec((None,) + shape, lambda s: (a_idx(s) // nt,) + (0,) * len(shape))
    seq_b = lambda *shape: pl.BlockSpec((None,) + shape, lambda s: (b_idx(s) // nt,) + (0,) * len(shape))
    out, ncst, S, c_new, m_new = pl.pallas_call(
        functools.partial(_even_mixer_kernel, nt=nt, ntot=ntot),
        out_shape=(jax.ShapeDtypeStruct((ntot, tm, D), F32),
                   jax.ShapeDtypeStruct((B, CONV_A - 1, QKV_A), F32),
                   jax.ShapeDtypeStruct((B, H_A, DK_A, DV_A), F32),
                   jax.ShapeDtypeStruct((B, H_B, DK_B, 2 * DV_B), F32),
                   jax.ShapeDtypeStruct((B, H_B, 8, 128), F32)),
        grid=(ntot + 1,),
        in_specs=[pl.BlockSpec((None, tm, D), lambda s: (a_idx(s), 0, 0)),
                  pl.BlockSpec((None, tm, D), lambda s: (b_idx(s), 0, 0)),
                  _const_spec((1, D)), seq_a(CONV_A - 1, QKV_A),
                  seq_b(H_A, DK_A, DV_A), seq_b(H_B, DK_B, 2 * DV_B), seq_b(H_B, 8, 128),
                  _sel_spec((D, N_BIG), (j,)), _sel_spec((D, 2 * GATE_PAD), (j,)), _const_spec((CONV_A, QKV_A)),
                  _const_spec((8, GATE_PAD)), _const_spec((tm, tm)), _const_spec((1, DV_A)),
                  _const_spec((1, H_B * DV_B)), _const_spec((N_LEVELS, L, L)), _sel_spec((dmix, D), (j,))],
        out_specs=(pl.BlockSpec((None, tm, D), lambda s: (b_idx(s), 0, 0)), seq_a(CONV_A - 1, QKV_A),
                   seq_b(H_A, DK_A, DV_A), seq_b(H_B, DK_B, 2 * DV_B), seq_b(H_B, 8, 128)),
        scratch_shapes=[pltpu.VMEM((2, rows, N_BIG), F32), pltpu.VMEM((2, rows, GATE_PAD), F32),
                        pltpu.VMEM((2, rows, GATE_PAD), F32), pltpu.VMEM((tm + 8, QKV_A), F32),
                        pltpu.VMEM((rows, dmix), BF16)],
        compiler_params=_params(("arbitrary",)),
        name="even_mixer",
    )(xt, xt, g, conv_st, S0, c_ext, m_b, wbig_all, wsm_all, conv_w, gp, tri, gdn_norm_w.reshape(1, DV_A),
      mlstm_norm_w.reshape(1, H_B * DV_B), lvl, wout_all)
    return out.reshape(B, T, D), (ncst, S, c_new[..., :DV_B], c_new[..., DV_B], m_new[:, :, 0, 0])


def _prep_weights(ffn_w_gu, ffn_w_d, w_ple, w_ple_gate, w_in_even, w_out_even, w_in_odd, w_out_odd):
    nbig = QKV_A + H_A * DV_A
    nmid = nbig + 2 * H_A
    nb_end = nmid + 4 * H_B * DK_B
    w_big = jnp.concatenate([w_in_even[:, :, :nbig], w_in_even[:, :, nmid:nb_end]], axis=-1).astype(BF16)
    w_sm = jnp.concatenate([w_in_even[:, :, nbig:nmid], w_in_even[:, :, nb_end:]], axis=-1)
    w_sm = jnp.pad(w_sm, ((0, 0), (0, 0), (0, GATE_PAD - w_sm.shape[-1])))
    w_sm = jnp.concatenate(_split_bf16(w_sm), axis=-1)
    return dict(wgu=ffn_w_gu.astype(BF16), wd=ffn_w_d.astype(BF16), wple=w_ple.astype(BF16),
                wgate=w_ple_gate.astype(BF16), w_big=w_big, w_sm=w_sm, w_out_even=w_out_even.astype(BF16),
                w_in_odd=w_in_odd.astype(BF16), w_out_odd=w_out_odd.astype(BF16))


def _gate_params(a_log, dt_bias, b_i, b_f):
    gp = jnp.zeros((8, GATE_PAD), F32)
    gp = gp.at[0, COL_G:COL_G + H_A].set(dt_bias).at[0, COL_LOGI:COL_LOGI + H_B].set(b_i)
    return gp.at[0, COL_LOGF:COL_LOGF + H_B].set(b_f).at[1, COL_G:COL_G + H_A].set(a_log)


def _run_trunk(x, p, init_even, init_odd, W, norm_g, final_norm, gdn_conv_w, gdn_a_log, gdn_dt_bias,
               gdn_norm_w, mlstm_b_i, mlstm_b_f, mlstm_norm_w, conv_c_w):
    B, T, D = x.shape
    depth = norm_g.shape[0]
    n = B * T
    x = x.reshape(n, D)
    new_even, new_odd = [], []
    gf = final_norm.reshape(1, D)
    for i in range(depth):
        j = i // 2
        g = lambda k: norm_g[i, k].reshape(1, D)
        x = _ffn(x, g(0), W["wgu"], W["wd"], (i, 0))
        if i % 2 == 0:
            conv_st, S0, C0, n0, m0 = init_even[j]
            gp = _gate_params(gdn_a_log[j], gdn_dt_bias[j], mlstm_b_i[j], mlstm_b_f[j])
            y, st = _even_mixer(x.reshape(B, T, D), g(1), conv_st, S0, C0, n0, m0, W["w_big"], W["w_sm"],
                                gdn_conv_w[j], gp, gdn_norm_w[j], mlstm_norm_w[j], W["w_out_even"], j, EVEN_TILE)
            new_even.append(st)
        else:
            y, st = _odd_mixer(x.reshape(B, T, D), g(1), init_odd[j], W["w_in_odd"], conv_c_w[j],
                               W["w_out_odd"], j)
            new_odd.append(st)
        x = y.reshape(n, D)
        x = _ffn(x, g(2), W["wgu"], W["wd"], (i, 1),
                 ple=(p.reshape(depth, n, -1), i, g(3), gf, W["wple"], W["wgate"], i == depth - 1))
    st_even = [jnp.stack([s[c] for s in new_even]) for c in range(5)]
    return (x.reshape(B, T, D), *st_even, jnp.stack(new_odd))


def kernel(x_prompt, x_sample, state_gdn_conv, state_gdn_S, state_mlstm_C, state_mlstm_n, state_mlstm_m,
           state_conv, p_prompt, p_sample, norm_g, final_norm, ffn_w_gu, ffn_w_d, w_ple, w_ple_gate,
           w_in_even, gdn_conv_w, gdn_a_log, gdn_dt_bias, gdn_norm_w, mlstm_b_i, mlstm_b_f, mlstm_norm_w,
           w_out_even, w_in_odd, conv_c_w, w_out_odd):
    Bp = x_prompt.shape[0]
    n_even, n_odd = state_gdn_S.shape[0], state_conv.shape[0]
    D = x_prompt.shape[-1]
    W = _prep_weights(ffn_w_gu, ffn_w_d, w_ple, w_ple_gate, w_in_even, w_out_even, w_in_odd, w_out_odd)
    zero_even = (jnp.zeros((Bp, CONV_A - 1, QKV_A), F32), jnp.zeros((Bp, H_A, DK_A, DV_A), F32),
                 jnp.zeros((Bp, H_B, DK_B, DV_B), F32), jnp.zeros((Bp, H_B, DK_B), F32),
                 jnp.zeros((Bp, H_B), F32))
    init_even_p = [zero_even] * n_even
    init_odd_p = [jnp.zeros((Bp, CONV_C - 1, D), F32)] * n_odd
    init_even_s = [(state_gdn_conv[j], state_gdn_S[j], state_mlstm_C[j], state_mlstm_n[j], state_mlstm_m[j])
                   for j in range(n_even)]
    init_odd_s = [state_conv[j] for j in range(n_odd)]
    rest = (norm_g, final_norm, gdn_conv_w, gdn_a_log, gdn_dt_bias, gdn_norm_w, mlstm_b_i, mlstm_b_f,
            mlstm_norm_w, conv_c_w)
    outs_p = _run_trunk(x_prompt, p_prompt, init_even_p, init_odd_p, W, *rest)
    outs_s = _run_trunk(x_sample, p_sample, init_even_s, init_odd_s, W, *rest)
    return (outs_p[0], outs_s[0]) + tuple(outs_p[1:]) + tuple(outs_s[1:])
```

```python
import functools

import jax
import jax.numpy as jnp
import numpy as np
from jax import lax
from jax.experimental import pallas as pl
from jax.experimental.pallas import tpu as pltpu

F32 = jnp.float32
BF16 = jnp.bfloat16

EPS = 1e-6
H_A, DK_A, DV_A, CONV_A = 4, 128, 128, 4
H_B, DK_B, DV_B = 4, 128, 128
CONV_C = 3
QKV_A = H_A * (2 * DK_A + DV_A)
FF_CHUNK = 256
GATE_PAD = 128
SEQ_CHUNK = 128
N_LEVELS = 7
EVEN_TILE = 256
A_FRONT = 10
VMEM_LIMIT = 56 * 1024 * 1024

OFF_Z = QKV_A
OFF_QB = OFF_Z + H_A * DV_A
OFF_KB = OFF_QB + H_B * DK_B
OFF_VB = OFF_KB + H_B * DK_B
OFF_OB = OFF_VB + H_B * DV_B
N_BIG = OFF_OB + H_B * DV_B
COL_BETA, COL_G, COL_LOGI, COL_LOGF = 0, H_A, 2 * H_A, 2 * H_A + H_B


def _rms(x, g):
    ms = jnp.mean(x * x, axis=-1, keepdims=True)
    return x * lax.rsqrt(ms + EPS) * g


def _dot(a, b):
    return jnp.dot(a, b, preferred_element_type=F32)


def _bdot(a, b):
    return jnp.dot(a.astype(BF16), b.astype(BF16), preferred_element_type=F32)


def _const_spec(shape):
    n = len(shape)
    return pl.BlockSpec(shape, lambda *_: (0,) * n, pipeline_mode=pl.Buffered(1))


def _sel_spec(shape, idx):
    n = len(shape)
    return pl.BlockSpec((None,) * len(idx) + tuple(shape), lambda *_: tuple(idx) + (0,) * n,
                        pipeline_mode=pl.Buffered(1))


def _params(sem):
    return pltpu.CompilerParams(dimension_semantics=sem, vmem_limit_bytes=VMEM_LIMIT)


def _row_tile(n, want):
    t = min(n, want)
    assert n % t == 0, (n, t)
    return t


def _swiglu_residual(x, g_ref, wgu_ref, wd_ref, act_ref):
    dff = wd_ref.shape[0]
    xn = _rms(x, g_ref[...]).astype(BF16)
    for c in range(dff // FF_CHUNK):
        lo = c * FF_CHUNK
        a = _dot(xn, wgu_ref[:, lo:lo + FF_CHUNK])
        b = _dot(xn, wgu_ref[:, dff + lo:dff + lo + FF_CHUNK])
        act_ref[:, lo:lo + FF_CHUNK] = (a * jax.nn.sigmoid(a) * b).astype(BF16)
    return x + 0.5 * _dot(act_ref[...], wd_ref[...])


def _ffn_kernel(x_ref, g_ref, wgu_ref, wd_ref, o_ref, act_ref):
    o_ref[...] = _swiglu_residual(x_ref[...], g_ref, wgu_ref, wd_ref, act_ref)


def _ffn_ple_kernel(x_ref, g_ref, wgu_ref, wd_ref, p_ref, gp_ref, gf_ref, wple_ref, wgate_ref, o_ref, act_ref, *,
                    final):
    y = _swiglu_residual(x_ref[...], g_ref, wgu_ref, wd_ref, act_ref)
    gate = jax.nn.sigmoid(_dot(_rms(y, gp_ref[...]).astype(BF16), wgate_ref[...]))
    y = y + _dot(p_ref[...].astype(BF16), wple_ref[...]) * gate
    if final:
        y = _rms(y, gf_ref[...])
    o_ref[...] = y


def _ffn(x, g, wgu_all, wd_all, idx, ple=None):
    n, d = x.shape
    dff = wd_all.shape[-2]
    tm = _row_tile(n, 1024)
    row = lambda w: pl.BlockSpec((tm, w), lambda i: (i, 0))
    in_specs = [row(d), _const_spec((1, d)), _sel_spec((d, 2 * dff), idx), _sel_spec((dff, d), idx)]
    args = [x, g, wgu_all, wd_all]
    body = _ffn_kernel
    if ple is not None:
        p_all, layer, gp, gf, wple_all, wgate_all, final = ple
        dp = p_all.shape[-1]
        in_specs += [pl.BlockSpec((None, tm, dp), lambda i: (layer, i, 0)), _const_spec((1, d)), _const_spec((1, d)),
                     _sel_spec((dp, d), (layer,)), _sel_spec((d, d), (layer,))]
        args += [p_all, gp, gf, wple_all, wgate_all]
        body = functools.partial(_ffn_ple_kernel, final=final)
    return pl.pallas_call(
        body,
        out_shape=jax.ShapeDtypeStruct((n, d), F32),
        grid=(n // tm,),
        in_specs=in_specs,
        out_specs=row(d),
        scratch_shapes=[pltpu.VMEM((tm, dff), BF16)],
        compiler_params=_params(("parallel",)),
        name="ffn" if ple is None else "ffn_ple",
    )(*args)


def _odd_kernel(x_ref, g_ref, st_ref, win_ref, cw_ref, wout_ref, o_ref, newst_ref, hist_ref, y_ref):
    sb, tt, d = x_ref.shape
    x = x_ref[...].reshape(sb * tt, d)
    xn = _rms(x, g_ref[...]).astype(BF16)

    @pl.when(pl.program_id(1) == 0)
    def _():
        hist_ref[:, 6:8, :] = st_ref[...]

    for c0 in range(0, d, FF_CHUNK):
        c1 = c0 + FF_CHUNK
        h = _dot(xn, win_ref[:, c0:c1])
        bg = _dot(xn, win_ref[:, d + c0:d + c1])
        cg = _dot(xn, win_ref[:, 2 * d + c0:2 * d + c1])
        hist_ref[:, 8:8 + tt, c0:c1] = (cg * h).reshape(sb, tt, FF_CHUNK)
        conv = (hist_ref[:, 6:6 + tt, c0:c1] * cw_ref[0:1, c0:c1] + hist_ref[:, 7:7 + tt, c0:c1] * cw_ref[1:2, c0:c1]
                + hist_ref[:, 8:8 + tt, c0:c1] * cw_ref[2:3, c0:c1])
        y_ref[:, c0:c1] = (bg * conv.reshape(sb * tt, FF_CHUNK)).astype(BF16)
    o_ref[...] = (x + _dot(y_ref[...], wout_ref[...])).reshape(sb, tt, d)
    last = hist_ref[:, tt + 6:tt + 8, :]
    hist_ref[:, 6:8, :] = last
    newst_ref[...] = last


def _odd_mixer(x, g, st, win_all, cw, wout_all, j):
    b, t, d = x.shape
    tt = _row_tile(t, 1024)
    sb = b if tt < 128 else 1
    return pl.pallas_call(
        _odd_kernel,
        out_shape=(jax.ShapeDtypeStruct((b, t, d), F32),
                   jax.ShapeDtypeStruct((b, CONV_C - 1, d), F32)),
        grid=(b // sb, t // tt),
        in_specs=[pl.BlockSpec((sb, tt, d), lambda i, k: (i, k, 0)),
                  _const_spec((1, d)),
                  pl.BlockSpec((sb, CONV_C - 1, d), lambda i, k: (i, 0, 0)),
                  _sel_spec((d, 3 * d), (j,)), _const_spec((CONV_C, d)), _sel_spec((d, d), (j,))],
        out_specs=(pl.BlockSpec((sb, tt, d), lambda i, k: (i, k, 0)),
                   pl.BlockSpec((sb, CONV_C - 1, d), lambda i, k: (i, 0, 0))),
        scratch_shapes=[pltpu.VMEM((sb, tt + 8, d), F32), pltpu.VMEM((sb * tt, d), BF16)],
        compiler_params=_params(("parallel", "arbitrary")),
        name="odd_mixer",
    )(x, g, st, win_all, cw, wout_all)


def _split_bf16(a):
    hi = a.astype(BF16)
    lo = (a - hi.astype(F32)).astype(BF16)
    return hi, lo


def _softplus_parts(y):
    t = jnp.log1p(jnp.exp(-jnp.abs(y)))
    return jnp.maximum(y, 0.0) + t, jnp.maximum(-y, 0.0) + t


def _chunk_tri(tm):
    t = np.arange(tm)
    same = (t[:, None] // SEQ_CHUNK) == (t[None, :] // SEQ_CHUNK)
    return (same & (t[:, None] >= t[None, :])).astype(np.float32)


def _level_masks():
    t = np.arange(SEQ_CHUNK)
    out = []
    for lv in range(N_LEVELS):
        b = 1 << lv
        tb, sb = t[:, None] // b, t[None, :] // b
        out.append(((tb % 2 == 1) & (sb == tb - 1)).astype(np.float32))
    return np.stack(out)


def _even_mixer_kernel(*refs, nt, ntot):
    for parity in (0, 1):
        @pl.when(pl.program_id(0) % 2 == parity)
        def _():
            _even_mixer_step(*refs, nt=nt, ntot=ntot, slot_a=parity)


def _even_mixer_step(xa_ref, g_ref, cst_ref, s0_ref, c0_ref, m0_ref, wbig_ref, wsm_ref, convw_ref, gp_ref,
                     tri_ref, gnw_ref, mnw_ref, lvl_ref, wout_ref,
                     o_ref, ncst_ref, s_ref, c_ref, m_ref,
                     big_scr, gt_scr, cum_scr, x_scr, hist_ref, mix_ref, *, nt, ntot, slot_a):
    L = SEQ_CHUNK
    tm = xa_ref.shape[0]
    nc = big_scr.shape[1] // L
    t_valid = min(tm, L)
    HA, HB = range(H_A), range(H_B)
    s = pl.program_id(0)
    slot_b = 1 - slot_a
    ja = jnp.minimum(s, ntot - 1) % nt
    jb = jnp.maximum(s - 1, 0) % nt
    big_a, gt_a, cum_a = big_scr.at[slot_a], gt_scr.at[slot_a], cum_scr.at[slot_a]
    big_b, gt_b, cum_b = big_scr.at[slot_b], gt_scr.at[slot_b], cum_scr.at[slot_b]

    @pl.when(s == 0)
    def _():
        big_scr[...] = jnp.zeros_like(big_scr)
        gt_scr[...] = jnp.zeros_like(gt_scr)
        cum_scr[...] = jnp.zeros_like(cum_scr)
        x_scr[...] = jnp.zeros_like(x_scr)

    @pl.when(ja == 0)
    def _():
        hist_ref[5:8, :] = cst_ref[...]

    @pl.when(jb == 0)
    def _():
        s_ref[...] = s0_ref[...]
        c_ref[...] = c0_ref[...]
        m_ref[...] = m0_ref[...]

    x_scr[slot_a] = xa_ref[...]
    xn = _rms(xa_ref[...], g_ref[...])
    xh, xl = _split_bf16(xn)

    def proj_task(c0, width):
        def run():
            r = _dot(xh, wbig_ref[:, c0:c0 + width])
            if c0 < OFF_Z:
                hist_ref[8:8 + tm, c0:c0 + width] = r
            elif OFF_QB <= c0 < OFF_KB:
                big_a[0:tm, c0:c0 + width] = r * (DK_B ** -0.5)
            else:
                big_a[0:tm, c0:c0 + width] = r
        return run

    def gate_task():
        r = _dot(xh, wsm_ref[...])
        y = r[:, :GATE_PAD] + (_dot(xl, wsm_ref[:, :GATE_PAD]) + r[:, GATE_PAD:]) + gp_ref[0:1, :]
        lane = lax.broadcasted_iota(jnp.int32, y.shape, 1)
        sp_pos, sp_neg = _softplus_parts(y)
        gt = jnp.where(lane < COL_G, jax.nn.sigmoid(y),
                       jnp.where(lane < COL_LOGI, -jnp.exp(gp_ref[1:2, :]) * sp_pos,
                                 jnp.where(lane < COL_LOGF, y, -sp_neg)))
        gt = jnp.where(lane < COL_LOGF + H_B, gt, 0.0)
        gt_a[0:tm, :] = gt
        g1 = gt.astype(BF16)
        r1 = gt - g1.astype(F32)
        g2 = r1.astype(BF16)
        g3 = (r1 - g2.astype(F32)).astype(BF16)
        tri = tri_ref[...]
        cum = _dot(tri, g1) + (_dot(tri, g2) + _dot(tri, g3))
        cum_a[0:tm, :] = cum
        if tm < L:
            cum_a[tm:L, :] = jnp.broadcast_to(cum[tm - 1:tm, :], (L - tm, GATE_PAD))

    def conv_block(blk):
        c0 = blk * 128
        acc = hist_ref[5:5 + tm, c0:c0 + 128] * convw_ref[0:1, c0:c0 + 128]
        for j in range(1, CONV_A):
            acc = acc + hist_ref[5 + j:5 + j + tm, c0:c0 + 128] * convw_ref[j:j + 1, c0:c0 + 128]
        v = acc * jax.nn.sigmoid(acc)
        if blk < 2 * H_A:
            v = v * lax.rsqrt(jnp.sum(v * v, axis=-1, keepdims=True) + EPS)
        if blk < H_A:
            v = v * (DK_A ** -0.5)
        big_a[0:tm, c0:c0 + 128] = v

    a_tasks = [proj_task(c0, 256) for c0 in range(0, N_BIG, 256)]
    a_tasks.insert(OFF_Z // 256, gate_task)

    def a_step(n=1):
        for _ in range(n):
            if a_tasks:
                a_tasks.pop(0)()

    row = lax.broadcasted_iota(jnp.int32, (L, L), 0)
    col = lax.broadcasted_iota(jnp.int32, (L, L), 1)
    incl = row >= col
    col_ok = incl if t_valid == L else (incl & (col < t_valid))
    eye = jnp.where(row == col, 1.0, 0.0)
    one_col = jnp.where(col == 0, 1.0, 0.0)

    def blk(c, off, h):
        return big_b[c * L:(c + 1) * L, off + h * 128:off + (h + 1) * 128]

    def mlstm_local(c):
        cum_c = cum_b[c * L:(c + 1) * L, :]
        cum_t, gt_t = cum_c.T, gt_b[c * L:(c + 1) * L, :].T
        p = dict(q=[blk(c, OFF_QB, h) for h in HB], k_t=[blk(c, OFF_KB, h).T for h in HB],
                 v_ext=[jnp.concatenate([blk(c, OFF_VB, h), one_col], axis=1) for h in HB],
                 b_c=[cum_c[:, COL_LOGF + h:COL_LOGF + h + 1] for h in HB])
        p["dmat"] = [jnp.where(col_ok, p["b_c"][h] - cum_t[COL_LOGF + h:COL_LOGF + h + 1, :]
                               + gt_t[COL_LOGI + h:COL_LOGI + h + 1, :], -jnp.inf) for h in HB]
        p["dmax"] = [jnp.max(p["dmat"][h], axis=-1, keepdims=True) for h in HB]
        p["s"] = [_bdot(p["q"][h], p["k_t"][h]) for h in HB]
        return p

    def mlstm_state(p, m_prev, c_cur):
        m_t = [jnp.maximum(p["b_c"][h] + m_prev[h], p["dmax"][h]) for h in HB]
        w_inter = [jnp.exp(p["b_c"][h] + m_prev[h] - m_t[h]) for h in HB]
        w_intra = [jnp.exp(p["dmat"][h] - m_t[h]) for h in HB]
        av = [_bdot(p["s"][h] * w_intra[h], p["v_ext"][h]) for h in HB]
        qc = [_bdot(p["q"][h], c_cur[h]) for h in HB]
        cu = [_bdot(p["k_t"][h] * w_intra[h][L - 1:L, :], p["v_ext"][h]) for h in HB]
        tot = [w_inter[h] * qc[h] + av[h] for h in HB]
        out = [tot[h][:, :DV_B] / jnp.maximum(jnp.abs(tot[h][:, DV_B:DV_B + 1]), jnp.exp(-m_t[h])) for h in HB]
        c_new = [w_inter[h][L - 1:L, :] * c_cur[h] + cu[h] for h in HB]
        return out, [m_t[h][L - 1:L, :] for h in HB], c_new

    def gdn_local(c):
        gt_c, cum_c = gt_b[c * L:(c + 1) * L, :], cum_b[c * L:(c + 1) * L, :]
        cum_t = cum_c.T
        p = dict(q=[blk(c, 0, h) for h in HA], k=[blk(c, H_A * DK_A, h) for h in HA],
                 v=[blk(c, 2 * H_A * DK_A, h) for h in HA],
                 beta=[gt_c[:, COL_BETA + h:COL_BETA + h + 1] for h in HA],
                 g_c=[cum_c[:, COL_G + h:COL_G + h + 1] for h in HA],
                 g_r=[cum_t[COL_G + h:COL_G + h + 1, :] for h in HA])
        p["gam"] = [jnp.where(incl, jnp.exp(jnp.where(incl, p["g_c"][h] - p["g_r"][h], 0.0)), 0.0) for h in HA]
        p["k_t"] = [p["k"][h].T for h in HA]
        p["qk_kk"] = [_bdot(jnp.concatenate([p["q"][h], p["k"][h]], axis=0), p["k_t"][h]) for h in HA]
        p["a_s"] = [p["beta"][h] * p["qk_kk"][h][L:] * p["gam"][h] for h in HA]
        p["d"] = [eye - p["a_s"][h] * lvl_ref[0] for h in HA]
        return p

    def gdn_solve(p):
        p["e_g"] = [jnp.exp(p["g_c"][h]) for h in HA]
        p["sol"] = [_bdot(p["d"][h], jnp.concatenate([p["beta"][h] * p["v"][h],
                                                      (p["beta"][h] * p["e_g"][h]) * p["k"][h]], axis=1)) for h in HA]

    def gdn_state_a(p, s_cur):
        wq = [_bdot(jnp.concatenate([p["sol"][h][:, DV_A:], p["q"][h]], axis=0), s_cur[h]) for h in HA]
        p["u"] = [p["sol"][h][:, :DV_A] - wq[h][:L] for h in HA]
        p["qs"] = [wq[h][L:] for h in HA]

    def gdn_state_b(p, s_cur):
        out = [p["e_g"][h] * p["qs"][h] + _bdot(p["qk_kk"][h][:L] * p["gam"][h], p["u"][h]) for h in HA]
        g_last = [p["g_r"][h][:, L - 1:L] for h in HA]
        s_new = [jnp.exp(g_last[h]) * s_cur[h] + _bdot(p["k_t"][h] * jnp.exp(g_last[h] - p["g_r"][h]), p["u"][h])
                 for h in HA]
        return out, s_new

    m_prev = [m_ref[h, 0:1, 0:1] for h in HB]
    c_cur = [c_ref[h] for h in HB]
    s_cur = [s_ref[h] for h in HA]
    m_out, g_out = [None] * nc, [None] * nc
    a_step(A_FRONT)
    prev = []
    for g0 in range(0, nc, 2):
        cs = list(range(g0, min(g0 + 2, nc)))
        ml = [mlstm_local(c) for c in cs]
        gd = [gdn_local(c) for c in cs]
        riders = {1 + i: ("mlstm", ml[i], c) for i, c in enumerate(cs)}
        for i, (c, p) in enumerate(prev):
            riders[3 + 2 * i], riders[4 + 2 * i] = ("gdn_a", p, c), ("gdn_b", p, c)
        for lv in range(1, N_LEVELS):
            de = [[_bdot(p["d"][h], p["a_s"][h] * lvl_ref[lv]) for h in HA] for p in gd]
            kind, arg, c = riders.get(lv, (None, None, None))
            if kind == "mlstm":
                m_out[c], m_prev, c_cur = mlstm_state(arg, m_prev, c_cur)
            elif kind == "gdn_a":
                gdn_state_a(arg, s_cur)
            elif kind == "gdn_b":
                g_out[c], s_cur = gdn_state_b(arg, s_cur)
            a_step()
            for p, de_p in zip(gd, de):
                p["d"] = [p["d"][h] - _bdot(de_p[h], p["d"][h]) for h in HA]
            a_step()
        for p in gd:
            gdn_solve(p)
        prev = list(zip(cs, gd))
    a_step(len(a_tasks))
    for c, p in prev:
        gdn_state_a(p, s_cur)
        g_out[c], s_cur = gdn_state_b(p, s_cur)

    for h in HA:
        s_ref[h] = s_cur[h]
    for h in HB:
        c_ref[h] = c_cur[h]
        m_ref[h] = jnp.broadcast_to(m_prev[h], m_ref.shape[1:])

    for b in range(QKV_A // 128):
        conv_block(b)
    last = hist_ref[tm + 5:tm + 8, :]
    hist_ref[5:8, :] = last
    ncst_ref[...] = last

    for c in range(nc):
        for h in HA:
            o = g_out[c][h]
            o = o * lax.rsqrt(jnp.mean(o * o, axis=-1, keepdims=True) + EPS) * gnw_ref[...]
            z = blk(c, OFF_Z, h)
            mix_ref[c * L:(c + 1) * L, h * DV_A:(h + 1) * DV_A] = (o * (z * jax.nn.sigmoid(z))).astype(BF16)
        for h in HB:
            hh = m_out[c][h]
            hh = hh * lax.rsqrt(jnp.mean(hh * hh, axis=-1, keepdims=True) + EPS)
            hh = hh * mnw_ref[:, h * DV_B:(h + 1) * DV_B] * jax.nn.sigmoid(blk(c, OFF_OB, h))
            c0 = H_A * DV_A + h * DV_B
            mix_ref[c * L:(c + 1) * L, c0:c0 + DV_B] = hh.astype(BF16)
    o_ref[...] = x_scr[slot_b] + _dot(mix_ref[...], wout_ref[...])[0:tm]


def _even_mixer(x, g, conv_st, S0, C0, n0, m0, wbig_all, wsm_all, conv_w, gp, gdn_norm_w, mlstm_norm_w, wout_all, j,
                tile):
    B, T, D = x.shape
    L = SEQ_CHUNK
    tm = _row_tile(T, tile)
    assert tm % L == 0 or tm == T < L
    nt = T // tm
    ntot = B * nt
    rows = max(tm, L)
    dmix = H_A * DV_A + H_B * DV_B
    c_ext = jnp.concatenate([C0, n0[..., None], jnp.zeros(C0.shape[:-1] + (DV_B - 1,), F32)], axis=-1)
    m_b = jnp.broadcast_to(m0[:, :, None, None], (B, H_B, 8, 128))
    lvl = jnp.asarray(_level_masks())
    tri = jnp.asarray(_chunk_tri(tm), BF16)
    xt = x.reshape(ntot, tm, D)
    a_idx = lambda s: jnp.minimum(s, ntot - 1)
    b_idx = lambda s: jnp.maximum(s - 1, 0)
    seq_a = lambda *shape: pl.BlockSpec((None,) + shape, lambda s: (a_idx(s) // nt,) + (0,) * len(shape))
    seq_b = lambda *shape: pl.BlockSpec((None,) + shape, lambda s: (b_idx(s) // nt,) + (0,) * len(shape))
    out, ncst, S, c_new, m_new = pl.pallas_call(
        functools.partial(_even_mixer_kernel, nt=nt, ntot=ntot),
        out_shape=(jax.ShapeDtypeStruct((ntot, tm, D), F32),
                   jax.ShapeDtypeStruct((B, CONV_A - 1, QKV_A), F32),
                   jax.ShapeDtypeStruct((B, H_A, DK_A, DV_A), F32),
                   jax.ShapeDtypeStruct((B, H_B, DK_B, 2 * DV_B), F32),
                   jax.ShapeDtypeStruct((B, H_B, 8, 128), F32)),
        grid=(ntot + 1,),
        in_specs=[pl.BlockSpec((None, tm, D), lambda s: (a_idx(s), 0, 0)),
                  _const_spec((1, D)), seq_a(CONV_A - 1, QKV_A),
                  seq_b(H_A, DK_A, DV_A), seq_b(H_B, DK_B, 2 * DV_B), seq_b(H_B, 8, 128),
                  _sel_spec((D, N_BIG), (j,)), _sel_spec((D, 2 * GATE_PAD), (j,)), _const_spec((CONV_A, QKV_A)),
                  _const_spec((8, GATE_PAD)), _const_spec((tm, tm)), _const_spec((1, DV_A)),
                  _const_spec((1, H_B * DV_B)), _const_spec((N_LEVELS, L, L)), _sel_spec((dmix, D), (j,))],
        out_specs=(pl.BlockSpec((None, tm, D), lambda s: (b_idx(s), 0, 0)), seq_a(CONV_A - 1, QKV_A),
                   seq_b(H_A, DK_A, DV_A), seq_b(H_B, DK_B, 2 * DV_B), seq_b(H_B, 8, 128)),
        scratch_shapes=[pltpu.VMEM((2, rows, N_BIG), F32), pltpu.VMEM((2, rows, GATE_PAD), F32),
                        pltpu.VMEM((2, rows, GATE_PAD), F32), pltpu.VMEM((2, tm, D), F32),
                        pltpu.VMEM((tm + 8, QKV_A), F32), pltpu.VMEM((rows, dmix), BF16)],
        compiler_params=_params(("arbitrary",)),
        name="even_mixer",
    )(xt, g, conv_st, S0, c_ext, m_b, wbig_all, wsm_all, conv_w, gp, tri, gdn_norm_w.reshape(1, DV_A),
      mlstm_norm_w.reshape(1, H_B * DV_B), lvl, wout_all)
    return out.reshape(B, T, D), (ncst, S, c_new[..., :DV_B], c_new[..., DV_B], m_new[:, :, 0, 0])


def _prep_weights(ffn_w_gu, ffn_w_d, w_ple, w_ple_gate, w_in_even, w_out_even, w_in_odd, w_out_odd):
    nbig = QKV_A + H_A * DV_A
    nmid = nbig + 2 * H_A
    nb_end = nmid + 4 * H_B * DK_B
    w_big = jnp.concatenate([w_in_even[:, :, :nbig], w_in_even[:, :, nmid:nb_end]], axis=-1).astype(BF16)
    w_sm = jnp.concatenate([w_in_even[:, :, nbig:nmid], w_in_even[:, :, nb_end:]], axis=-1)
    w_sm = jnp.pad(w_sm, ((0, 0), (0, 0), (0, GATE_PAD - w_sm.shape[-1])))
    w_sm = jnp.concatenate(_split_bf16(w_sm), axis=-1)
    return dict(wgu=ffn_w_gu.astype(BF16), wd=ffn_w_d.astype(BF16), wple=w_ple.astype(BF16),
                wgate=w_ple_gate.astype(BF16), w_big=w_big, w_sm=w_sm, w_out_even=w_out_even.astype(BF16),
                w_in_odd=w_in_odd.astype(BF16), w_out_odd=w_out_odd.astype(BF16))


def _gate_params(a_log, dt_bias, b_i, b_f):
    gp = jnp.zeros((8, GATE_PAD), F32)
    gp = gp.at[0, COL_G:COL_G + H_A].set(dt_bias).at[0, COL_LOGI:COL_LOGI + H_B].set(b_i)
    return gp.at[0, COL_LOGF:COL_LOGF + H_B].set(b_f).at[1, COL_G:COL_G + H_A].set(a_log)


def _run_trunk(x, p, init_even, init_odd, W, norm_g, final_norm, gdn_conv_w, gdn_a_log, gdn_dt_bias,
               gdn_norm_w, mlstm_b_i, mlstm_b_f, mlstm_norm_w, conv_c_w):
    B, T, D = x.shape
    depth = norm_g.shape[0]
    n = B * T
    x = x.reshape(n, D)
    new_even, new_odd = [], []
    gf = final_norm.reshape(1, D)
    for i in range(depth):
        j = i // 2
        g = lambda k: norm_g[i, k].reshape(1, D)
        x = _ffn(x, g(0), W["wgu"], W["wd"], (i, 0))
        if i % 2 == 0:
            conv_st, S0, C0, n0, m0 = init_even[j]
            gp = _gate_params(gdn_a_log[j], gdn_dt_bias[j], mlstm_b_i[j], mlstm_b_f[j])
            y, st = _even_mixer(x.reshape(B, T, D), g(1), conv_st, S0, C0, n0, m0, W["w_big"], W["w_sm"],
                                gdn_conv_w[j], gp, gdn_norm_w[j], mlstm_norm_w[j], W["w_out_even"], j, EVEN_TILE)
            new_even.append(st)
        else:
            y, st = _odd_mixer(x.reshape(B, T, D), g(1), init_odd[j], W["w_in_odd"], conv_c_w[j],
                               W["w_out_odd"], j)
            new_odd.append(st)
        x = y.reshape(n, D)
        x = _ffn(x, g(2), W["wgu"], W["wd"], (i, 1),
                 ple=(p.reshape(depth, n, -1), i, g(3), gf, W["wple"], W["wgate"], i == depth - 1))
    st_even = [jnp.stack([s[c] for s in new_even]) for c in range(5)]
    return (x.reshape(B, T, D), *st_even, jnp.stack(new_odd))


def kernel(x_prompt, x_sample, state_gdn_conv, state_gdn_S, state_mlstm_C, state_mlstm_n, state_mlstm_m,
           state_conv, p_prompt, p_sample, norm_g, final_norm, ffn_w_gu, ffn_w_d, w_ple, w_ple_gate,
           w_in_even, gdn_conv_w, gdn_a_log, gdn_dt_bias, gdn_norm_w, mlstm_b_i, mlstm_b_f, mlstm_norm_w,
           w_out_even, w_in_odd, conv_c_w, w_out_odd):
    Bp = x_prompt.shape[0]
    n_even, n_odd = state_gdn_S.shape[0], state_conv.shape[0]
    D = x_prompt.shape[-1]
    W = _prep_weights(ffn_w_gu, ffn_w_d, w_ple, w_ple_gate, w_in_even, w_out_even, w_in_odd, w_out_odd)
    zero_even = (jnp.zeros((Bp, CONV_A - 1, QKV_A), F32), jnp.zeros((Bp, H_A, DK_A, DV_A), F32),
                 jnp.zeros((Bp, H_B, DK_B, DV_B), F32), jnp.zeros((Bp, H_B, DK_B), F32),
                 jnp.zeros((Bp, H_B), F32))
    init_even_p = [zero_even] * n_even
    init_odd_p = [jnp.zeros((Bp, CONV_C - 1, D), F32)] * n_odd
    init_even_s = [(state_gdn_conv[j], state_gdn_S[j], state_mlstm_C[j], state_mlstm_n[j], state_mlstm_m[j])
                   for j in range(n_even)]
    init_odd_s = [state_conv[j] for j in range(n_odd)]
    rest = (norm_g, final_norm, gdn_conv_w, gdn_a_log, gdn_dt_bias, gdn_norm_w, mlstm_b_i, mlstm_b_f,
            mlstm_norm_w, conv_c_w)
    outs_p = _run_trunk(x_prompt, p_prompt, init_even_p, init_odd_p, W, *rest)
    outs_s = _run_trunk(x_sample, p_sample, init_even_s, init_odd_s, W, *rest)
    return (outs_p[0], outs_s[0]) + tuple(outs_p[1:]) + tuple(outs_s[1:])
```

```python
import functools

import jax
import jax.numpy as jnp
import numpy as np
from jax import lax
from jax.experimental import pallas as pl
from jax.experimental.pallas import tpu as pltpu

F32 = jnp.float32
BF16 = jnp.bfloat16

EPS = 1e-6
H_A, DK_A, DV_A, CONV_A = 4, 128, 128, 4
H_B, DK_B, DV_B = 4, 128, 128
CONV_C = 3
QKV_A = H_A * (2 * DK_A + DV_A)
FF_CHUNK = 256
GATE_PAD = 128
SEQ_CHUNK = 128
N_LEVELS = 7
EVEN_TILE = 256
A_FRONT = 10
VMEM_LIMIT = 56 * 1024 * 1024

OFF_Z = QKV_A
OFF_QB = OFF_Z + H_A * DV_A
OFF_KB = OFF_QB + H_B * DK_B
OFF_VB = OFF_KB + H_B * DK_B
OFF_OB = OFF_VB + H_B * DV_B
N_BIG = OFF_OB + H_B * DV_B
COL_BETA, COL_G, COL_LOGI, COL_LOGF = 0, H_A, 2 * H_A, 2 * H_A + H_B


def _rms(x, g):
    ms = jnp.mean(x * x, axis=-1, keepdims=True)
    return x * lax.rsqrt(ms + EPS) * g


def _dot(a, b):
    return jnp.dot(a, b, preferred_element_type=F32)


def _bdot(a, b):
    return jnp.dot(a.astype(BF16), b.astype(BF16), preferred_element_type=F32)


def _const_spec(shape):
    n = len(shape)
    return pl.BlockSpec(shape, lambda *_: (0,) * n, pipeline_mode=pl.Buffered(1))


def _sel_spec(shape, idx):
    n = len(shape)
    return pl.BlockSpec((None,) * len(idx) + tuple(shape), lambda *_: tuple(idx) + (0,) * n,
                        pipeline_mode=pl.Buffered(1))


def _params(sem):
    return pltpu.CompilerParams(dimension_semantics=sem, vmem_limit_bytes=VMEM_LIMIT)


def _row_tile(n, want):
    t = min(n, want)
    assert n % t == 0, (n, t)
    return t


def _swiglu_residual(x, g_ref, wgu_ref, wd_ref, act_ref):
    dff = wd_ref.shape[0]
    xn = _rms(x, g_ref[...]).astype(BF16)
    for c in range(dff // FF_CHUNK):
        lo = c * FF_CHUNK
        a = _dot(xn, wgu_ref[:, lo:lo + FF_CHUNK])
        b = _dot(xn, wgu_ref[:, dff + lo:dff + lo + FF_CHUNK])
        act_ref[:, lo:lo + FF_CHUNK] = (a * jax.nn.sigmoid(a) * b).astype(BF16)
    return x + 0.5 * _dot(act_ref[...], wd_ref[...])


def _ffn_kernel(x_ref, g_ref, wgu_ref, wd_ref, o_ref, act_ref):
    o_ref[...] = _swiglu_residual(x_ref[...], g_ref, wgu_ref, wd_ref, act_ref)


def _ffn_ple_kernel(x_ref, g_ref, wgu_ref, wd_ref, p_ref, gp_ref, gf_ref, wple_ref, wgate_ref, o_ref, act_ref, *,
                    final):
    y = _swiglu_residual(x_ref[...], g_ref, wgu_ref, wd_ref, act_ref)
    gate = jax.nn.sigmoid(_dot(_rms(y, gp_ref[...]).astype(BF16), wgate_ref[...]))
    y = y + _dot(p_ref[...].astype(BF16), wple_ref[...]) * gate
    if final:
        y = _rms(y, gf_ref[...])
    o_ref[...] = y


def _ffn(x, g, wgu_all, wd_all, idx, ple=None):
    n, d = x.shape
    dff = wd_all.shape[-2]
    tm = _row_tile(n, 1024)
    row = lambda w: pl.BlockSpec((tm, w), lambda i: (i, 0))
    in_specs = [row(d), _const_spec((1, d)), _sel_spec((d, 2 * dff), idx), _sel_spec((dff, d), idx)]
    args = [x, g, wgu_all, wd_all]
    body = _ffn_kernel
    if ple is not None:
        p_all, layer, gp, gf, wple_all, wgate_all, final = ple
        dp = p_all.shape[-1]
        in_specs += [pl.BlockSpec((None, tm, dp), lambda i: (layer, i, 0)), _const_spec((1, d)), _const_spec((1, d)),
                     _sel_spec((dp, d), (layer,)), _sel_spec((d, d), (layer,))]
        args += [p_all, gp, gf, wple_all, wgate_all]
        body = functools.partial(_ffn_ple_kernel, final=final)
    return pl.pallas_call(
        body,
        out_shape=jax.ShapeDtypeStruct((n, d), F32),
        grid=(n // tm,),
        in_specs=in_specs,
        out_specs=row(d),
        scratch_shapes=[pltpu.VMEM((tm, dff), BF16)],
        compiler_params=_params(("parallel",)),
        name="ffn" if ple is None else "ffn_ple",
    )(*args)


def _odd_kernel(x_ref, g_ref, st_ref, win_ref, cw_ref, wout_ref, o_ref, newst_ref, hist_ref, y_ref):
    sb, tt, d = x_ref.shape
    x = x_ref[...].reshape(sb * tt, d)
    xn = _rms(x, g_ref[...]).astype(BF16)

    @pl.when(pl.program_id(1) == 0)
    def _():
        hist_ref[:, 6:8, :] = st_ref[...]

    for c0 in range(0, d, FF_CHUNK):
        c1 = c0 + FF_CHUNK
        h = _dot(xn, win_ref[:, c0:c1])
        cg = _dot(xn, win_ref[:, 2 * d + c0:2 * d + c1])
        bg = _dot(xn, win_ref[:, d + c0:d + c1])
        hist_ref[:, 8:8 + tt, c0:c1] = (cg * h).reshape(sb, tt, FF_CHUNK)
        conv = (hist_ref[:, 6:6 + tt, c0:c1] * cw_ref[0:1, c0:c1] + hist_ref[:, 7:7 + tt, c0:c1] * cw_ref[1:2, c0:c1]
                + hist_ref[:, 8:8 + tt, c0:c1] * cw_ref[2:3, c0:c1])
        y_ref[:, c0:c1] = (bg * conv.reshape(sb * tt, FF_CHUNK)).astype(BF16)
    o_ref[...] = (x + _dot(y_ref[...], wout_ref[...])).reshape(sb, tt, d)
    last = hist_ref[:, tt + 6:tt + 8, :]
    hist_ref[:, 6:8, :] = last
    newst_ref[...] = last


def _odd_mixer(x, g, st, win_all, cw, wout_all, j):
    b, t, d = x.shape
    tt = _row_tile(t, 1024)
    sb = b if tt < 128 else 1
    return pl.pallas_call(
        _odd_kernel,
        out_shape=(jax.ShapeDtypeStruct((b, t, d), F32),
                   jax.ShapeDtypeStruct((b, CONV_C - 1, d), F32)),
        grid=(b // sb, t // tt),
        in_specs=[pl.BlockSpec((sb, tt, d), lambda i, k: (i, k, 0)),
                  _const_spec((1, d)),
                  pl.BlockSpec((sb, CONV_C - 1, d), lambda i, k: (i, 0, 0)),
                  _sel_spec((d, 3 * d), (j,)), _const_spec((CONV_C, d)), _sel_spec((d, d), (j,))],
        out_specs=(pl.BlockSpec((sb, tt, d), lambda i, k: (i, k, 0)),
                   pl.BlockSpec((sb, CONV_C - 1, d), lambda i, k: (i, 0, 0))),
        scratch_shapes=[pltpu.VMEM((sb, tt + 8, d), F32), pltpu.VMEM((sb * tt, d), BF16)],
        compiler_params=_params(("parallel", "arbitrary")),
        name="odd_mixer",
    )(x, g, st, win_all, cw, wout_all)


def _split_bf16(a):
    hi = a.astype(BF16)
    lo = (a - hi.astype(F32)).astype(BF16)
    return hi, lo


def _softplus_parts(y):
    t = jnp.log1p(jnp.exp(-jnp.abs(y)))
    return jnp.maximum(y, 0.0) + t, jnp.maximum(-y, 0.0) + t


def _chunk_tri(tm):
    t = np.arange(tm)
    same = (t[:, None] // SEQ_CHUNK) == (t[None, :] // SEQ_CHUNK)
    return (same & (t[:, None] >= t[None, :])).astype(np.float32)


def _level_masks():
    t = np.arange(SEQ_CHUNK)
    out = []
    for lv in range(N_LEVELS):
        b = 1 << lv
        tb, sb = t[:, None] // b, t[None, :] // b
        out.append(((tb % 2 == 1) & (sb == tb - 1)).astype(np.float32))
    return np.stack(out)


def _even_mixer_kernel(*refs, nt, ntot):
    for parity in (0, 1):
        @pl.when(pl.program_id(0) % 2 == parity)
        def _():
            _even_mixer_step(*refs, nt=nt, ntot=ntot, slot_a=parity)


def _even_mixer_step(xa_ref, xb_ref, g_ref, cst_ref, s0_ref, c0_ref, m0_ref, wbig_ref, wsm_ref, convw_ref, gp_ref,
                     tri_ref, gnw_ref, mnw_ref, lvl_ref, wout_ref,
                     o_ref, ncst_ref, s_ref, c_ref, m_ref,
                     big_scr, gt_scr, cum_scr, hist_ref, mix_ref, *, nt, ntot, slot_a):
    L = SEQ_CHUNK
    tm = xa_ref.shape[0]
    nc = big_scr.shape[1] // L
    t_valid = min(tm, L)
    HA, HB = range(H_A), range(H_B)
    s = pl.program_id(0)
    slot_b = 1 - slot_a
    ja = jnp.minimum(s, ntot - 1) % nt
    jb = jnp.maximum(s - 1, 0) % nt
    big_a, gt_a, cum_a = big_scr.at[slot_a], gt_scr.at[slot_a], cum_scr.at[slot_a]
    big_b, gt_b, cum_b = big_scr.at[slot_b], gt_scr.at[slot_b], cum_scr.at[slot_b]

    @pl.when(s == 0)
    def _():
        big_scr[...] = jnp.zeros_like(big_scr)
        gt_scr[...] = jnp.zeros_like(gt_scr)
        cum_scr[...] = jnp.zeros_like(cum_scr)

    @pl.when(ja == 0)
    def _():
        hist_ref[5:8, :] = cst_ref[...]

    @pl.when(jb == 0)
    def _():
        s_ref[...] = s0_ref[...]
        c_ref[...] = c0_ref[...]
        m_ref[...] = m0_ref[...]

    xn = _rms(xa_ref[...], g_ref[...])
    xh, xl = _split_bf16(xn)

    def proj_task(c0, width):
        def run():
            r = _dot(xh, wbig_ref[:, c0:c0 + width])
            if c0 < OFF_Z:
                hist_ref[8:8 + tm, c0:c0 + width] = r
            elif OFF_QB <= c0 < OFF_KB:
                big_a[0:tm, c0:c0 + width] = r * (DK_B ** -0.5)
            else:
                big_a[0:tm, c0:c0 + width] = r
        return run

    def gate_task():
        r = _dot(xh, wsm_ref[...])
        y = r[:, :GATE_PAD] + (_dot(xl, wsm_ref[:, :GATE_PAD]) + r[:, GATE_PAD:]) + gp_ref[0:1, :]
        lane = lax.broadcasted_iota(jnp.int32, y.shape, 1)
        sp_pos, sp_neg = _softplus_parts(y)
        gt = jnp.where(lane < COL_G, jax.nn.sigmoid(y),
                       jnp.where(lane < COL_LOGI, -jnp.exp(gp_ref[1:2, :]) * sp_pos,
                                 jnp.where(lane < COL_LOGF, y, -sp_neg)))
        gt = jnp.where(lane < COL_LOGF + H_B, gt, 0.0)
        gt_a[0:tm, :] = gt
        g1 = gt.astype(BF16)
        r1 = gt - g1.astype(F32)
        g2 = r1.astype(BF16)
        g3 = (r1 - g2.astype(F32)).astype(BF16)
        tri = tri_ref[...]
        cum = _dot(tri, g1) + (_dot(tri, g2) + _dot(tri, g3))
        cum_a[0:tm, :] = cum
        if tm < L:
            cum_a[tm:L, :] = jnp.broadcast_to(cum[tm - 1:tm, :], (L - tm, GATE_PAD))

    def conv_block(blk):
        c0 = blk * 128
        acc = hist_ref[5:5 + tm, c0:c0 + 128] * convw_ref[0:1, c0:c0 + 128]
        for j in range(1, CONV_A):
            acc = acc + hist_ref[5 + j:5 + j + tm, c0:c0 + 128] * convw_ref[j:j + 1, c0:c0 + 128]
        v = acc * jax.nn.sigmoid(acc)
        if blk < 2 * H_A:
            v = v * lax.rsqrt(jnp.sum(v * v, axis=-1, keepdims=True) + EPS)
        if blk < H_A:
            v = v * (DK_A ** -0.5)
        big_a[0:tm, c0:c0 + 128] = v

    a_tasks = [proj_task(c0, 256) for c0 in range(0, N_BIG, 256)]
    a_tasks.insert(0, gate_task)

    def a_step(n=1):
        for _ in range(n):
            if a_tasks:
                a_tasks.pop(0)()

    row = lax.broadcasted_iota(jnp.int32, (L, L), 0)
    col = lax.broadcasted_iota(jnp.int32, (L, L), 1)
    incl = row >= col
    col_ok = incl if t_valid == L else (incl & (col < t_valid))
    eye = jnp.where(row == col, 1.0, 0.0)
    one_col = jnp.where(col == 0, 1.0, 0.0)

    def blk(c, off, h):
        return big_b[c * L:(c + 1) * L, off + h * 128:off + (h + 1) * 128]

    def mlstm_local(c):
        cum_c = cum_b[c * L:(c + 1) * L, :]
        cum_t, gt_t = cum_c.T, gt_b[c * L:(c + 1) * L, :].T
        p = dict(q=[blk(c, OFF_QB, h) for h in HB], k_t=[blk(c, OFF_KB, h).T for h in HB],
                 v_ext=[jnp.concatenate([blk(c, OFF_VB, h), one_col], axis=1) for h in HB],
                 b_c=[cum_c[:, COL_LOGF + h:COL_LOGF + h + 1] for h in HB])
        p["dmat"] = [jnp.where(col_ok, p["b_c"][h] - cum_t[COL_LOGF + h:COL_LOGF + h + 1, :]
                               + gt_t[COL_LOGI + h:COL_LOGI + h + 1, :], -jnp.inf) for h in HB]
        p["dmax"] = [jnp.max(p["dmat"][h], axis=-1, keepdims=True) for h in HB]
        p["s"] = [_bdot(p["q"][h], p["k_t"][h]) for h in HB]
        return p

    def mlstm_state(p, m_prev, c_cur):
        m_t = [jnp.maximum(p["b_c"][h] + m_prev[h], p["dmax"][h]) for h in HB]
        w_inter = [jnp.exp(p["b_c"][h] + m_prev[h] - m_t[h]) for h in HB]
        w_intra = [jnp.exp(p["dmat"][h] - m_t[h]) for h in HB]
        av = [_bdot(p["s"][h] * w_intra[h], p["v_ext"][h]) for h in HB]
        qc = [_bdot(p["q"][h], c_cur[h]) for h in HB]
        cu = [_bdot(p["k_t"][h] * w_intra[h][L - 1:L, :], p["v_ext"][h]) for h in HB]
        tot = [w_inter[h] * qc[h] + av[h] for h in HB]
        out = [tot[h][:, :DV_B] / jnp.maximum(jnp.abs(tot[h][:, DV_B:DV_B + 1]), jnp.exp(-m_t[h])) for h in HB]
        c_new = [w_inter[h][L - 1:L, :] * c_cur[h] + cu[h] for h in HB]
        return out, [m_t[h][L - 1:L, :] for h in HB], c_new

    def gdn_local(c):
        gt_c, cum_c = gt_b[c * L:(c + 1) * L, :], cum_b[c * L:(c + 1) * L, :]
        cum_t = cum_c.T
        p = dict(q=[blk(c, 0, h) for h in HA], k=[blk(c, H_A * DK_A, h) for h in HA],
                 v=[blk(c, 2 * H_A * DK_A, h) for h in HA],
                 beta=[gt_c[:, COL_BETA + h:COL_BETA + h + 1] for h in HA],
                 g_c=[cum_c[:, COL_G + h:COL_G + h + 1] for h in HA],
                 g_r=[cum_t[COL_G + h:COL_G + h + 1, :] for h in HA])
        p["gam"] = [jnp.where(incl, jnp.exp(jnp.where(incl, p["g_c"][h] - p["g_r"][h], 0.0)), 0.0) for h in HA]
        p["k_t"] = [p["k"][h].T for h in HA]
        p["qk_kk"] = [_bdot(jnp.concatenate([p["q"][h], p["k"][h]], axis=0), p["k_t"][h]) for h in HA]
        p["a_s"] = [p["beta"][h] * p["qk_kk"][h][L:] * p["gam"][h] for h in HA]
        p["d"] = [eye - p["a_s"][h] * lvl_ref[0] for h in HA]
        return p

    def gdn_solve(p):
        p["e_g"] = [jnp.exp(p["g_c"][h]) for h in HA]
        p["sol"] = [_bdot(p["d"][h], jnp.concatenate([p["beta"][h] * p["v"][h],
                                                      (p["beta"][h] * p["e_g"][h]) * p["k"][h]], axis=1)) for h in HA]

    def gdn_state_a(p, s_cur):
        wq = [_bdot(jnp.concatenate([p["sol"][h][:, DV_A:], p["q"][h]], axis=0), s_cur[h]) for h in HA]
        p["u"] = [p["sol"][h][:, :DV_A] - wq[h][:L] for h in HA]
        p["qs"] = [wq[h][L:] for h in HA]

    def gdn_state_b(p, s_cur):
        out = [p["e_g"][h] * p["qs"][h] + _bdot(p["qk_kk"][h][:L] * p["gam"][h], p["u"][h]) for h in HA]
        g_last = [p["g_r"][h][:, L - 1:L] for h in HA]
        s_new = [jnp.exp(g_last[h]) * s_cur[h] + _bdot(p["k_t"][h] * jnp.exp(g_last[h] - p["g_r"][h]), p["u"][h])
                 for h in HA]
        return out, s_new

    m_prev = [m_ref[h, 0:1, 0:1] for h in HB]
    c_cur = [c_ref[h] for h in HB]
    s_cur = [s_ref[h] for h in HA]
    m_out, g_out = [None] * nc, [None] * nc
    a_step(A_FRONT)
    prev = []
    for g0 in range(0, nc, 2):
        cs = list(range(g0, min(g0 + 2, nc)))
        ml = [mlstm_local(c) for c in cs]
        gd = [gdn_local(c) for c in cs]
        riders = {1 + i: ("mlstm", ml[i], c) for i, c in enumerate(cs)}
        for i, (c, p) in enumerate(prev):
            riders[3 + 2 * i], riders[4 + 2 * i] = ("gdn_a", p, c), ("gdn_b", p, c)
        for lv in range(1, N_LEVELS):
            de = [[_bdot(p["d"][h], p["a_s"][h] * lvl_ref[lv]) for h in HA] for p in gd]
            kind, arg, c = riders.get(lv, (None, None, None))
            if kind == "mlstm":
                m_out[c], m_prev, c_cur = mlstm_state(arg, m_prev, c_cur)
            elif kind == "gdn_a":
                gdn_state_a(arg, s_cur)
            elif kind == "gdn_b":
                g_out[c], s_cur = gdn_state_b(arg, s_cur)
            a_step()
            for p, de_p in zip(gd, de):
                p["d"] = [p["d"][h] - _bdot(de_p[h], p["d"][h]) for h in HA]
            a_step()
        for p in gd:
            gdn_solve(p)
        prev = list(zip(cs, gd))
    a_step(len(a_tasks))
    for c, p in prev:
        gdn_state_a(p, s_cur)
        g_out[c], s_cur = gdn_state_b(p, s_cur)

    for h in HA:
        s_ref[h] = s_cur[h]
    for h in HB:
        c_ref[h] = c_cur[h]
        m_ref[h] = jnp.broadcast_to(m_prev[h], m_ref.shape[1:])

    for b in range(QKV_A // 128):
        conv_block(b)
    last = hist_ref[tm + 5:tm + 8, :]
    hist_ref[5:8, :] = last
    ncst_ref[...] = last

    for c in range(nc):
        for h in HA:
            o = g_out[c][h]
            o = o * lax.rsqrt(jnp.mean(o * o, axis=-1, keepdims=True) + EPS) * gnw_ref[...]
            z = blk(c, OFF_Z, h)
            mix_ref[c * L:(c + 1) * L, h * DV_A:(h + 1) * DV_A] = (o * (z * jax.nn.sigmoid(z))).astype(BF16)
        for h in HB:
            hh = m_out[c][h]
            hh = hh * lax.rsqrt(jnp.mean(hh * hh, axis=-1, keepdims=True) + EPS)
            hh = hh * mnw_ref[:, h * DV_B:(h + 1) * DV_B] * jax.nn.sigmoid(blk(c, OFF_OB, h))
            c0 = H_A * DV_A + h * DV_B
            mix_ref[c * L:(c + 1) * L, c0:c0 + DV_B] = hh.astype(BF16)
    o_ref[...] = xb_ref[...] + _dot(mix_ref[...], wout_ref[...])[0:tm]


def _even_mixer(x, g, conv_st, S0, C0, n0, m0, wbig_all, wsm_all, conv_w, gp, gdn_norm_w, mlstm_norm_w, wout_all, j,
                tile):
    B, T, D = x.shape
    L = SEQ_CHUNK
    tm = _row_tile(T, tile)
    assert tm % L == 0 or tm == T < L
    nt = T // tm
    ntot = B * nt
    rows = max(tm, L)
    dmix = H_A * DV_A + H_B * DV_B
    c_ext = jnp.concatenate([C0, n0[..., None], jnp.zeros(C0.shape[:-1] + (DV_B - 1,), F32)], axis=-1)
    m_b = jnp.broadcast_to(m0[:, :, None, None], (B, H_B, 8, 128))
    lvl = jnp.asarray(_level_masks())
    tri = jnp.asarray(_chunk_tri(tm), BF16)
    xt = x.reshape(ntot, tm, D)
    a_idx = lambda s: jnp.minimum(s, ntot - 1)
    b_idx = lambda s: jnp.maximum(s - 1, 0)
    seq_a = lambda *shape: pl.BlockSpec((None,) + shape, lambda s: (a_idx(s) // nt,) + (0,) * len(shape))
    seq_b = lambda *shape: pl.BlockSpec((None,) + shape, lambda s: (b_idx(s) // nt,) + (0,) * len(shape))
    out, ncst, S, c_new, m_new = pl.pallas_call(
        functools.partial(_even_mixer_kernel, nt=nt, ntot=ntot),
        out_shape=(jax.ShapeDtypeStruct((ntot, tm, D), F32),
                   jax.ShapeDtypeStruct((B, CONV_A - 1, QKV_A), F32),
                   jax.ShapeDtypeStruct((B, H_A, DK_A, DV_A), F32),
                   jax.ShapeDtypeStruct((B, H_B, DK_B, 2 * DV_B), F32),
                   jax.ShapeDtypeStruct((B, H_B, 8, 128), F32)),
        grid=(ntot + 1,),
        in_specs=[pl.BlockSpec((None, tm, D), lambda s: (a_idx(s), 0, 0)),
                  pl.BlockSpec((None, tm, D), lambda s: (b_idx(s), 0, 0)),
                  _const_spec((1, D)), seq_a(CONV_A - 1, QKV_A),
                  seq_b(H_A, DK_A, DV_A), seq_b(H_B, DK_B, 2 * DV_B), seq_b(H_B, 8, 128),
                  _sel_spec((D, N_BIG), (j,)), _sel_spec((D, 2 * GATE_PAD), (j,)), _const_spec((CONV_A, QKV_A)),
                  _const_spec((8, GATE_PAD)), _const_spec((tm, tm)), _const_spec((1, DV_A)),
                  _const_spec((1, H_B * DV_B)), _const_spec((N_LEVELS, L, L)), _sel_spec((dmix, D), (j,))],
        out_specs=(pl.BlockSpec((None, tm, D), lambda s: (b_idx(s), 0, 0)), seq_a(CONV_A - 1, QKV_A),
                   seq_b(H_A, DK_A, DV_A), seq_b(H_B, DK_B, 2 * DV_B), seq_b(H_B, 8, 128)),
        scratch_shapes=[pltpu.VMEM((2, rows, N_BIG), F32), pltpu.VMEM((2, rows, GATE_PAD), F32),
                        pltpu.VMEM((2, rows, GATE_PAD), F32), pltpu.VMEM((tm + 8, QKV_A), F32),
                        pltpu.VMEM((rows, dmix), BF16)],
        compiler_params=_params(("arbitrary",)),
        name="even_mixer",
    )(xt, xt, g, conv_st, S0, c_ext, m_b, wbig_all, wsm_all, conv_w, gp, tri, gdn_norm_w.reshape(1, DV_A),
      mlstm_norm_w.reshape(1, H_B * DV_B), lvl, wout_all)
    return out.reshape(B, T, D), (ncst, S, c_new[..., :DV_B], c_new[..., DV_B], m_new[:, :, 0, 0])


def _prep_weights(ffn_w_gu, ffn_w_d, w_ple, w_ple_gate, w_in_even, w_out_even, w_in_odd, w_out_odd):
    nbig = QKV_A + H_A * DV_A
    nmid = nbig + 2 * H_A
    nb_end = nmid + 4 * H_B * DK_B
    w_big = jnp.concatenate([w_in_even[:, :, :nbig], w_in_even[:, :, nmid:nb_end]], axis=-1).astype(BF16)
    w_sm = jnp.concatenate([w_in_even[:, :, nbig:nmid], w_in_even[:, :, nb_end:]], axis=-1)
    w_sm = jnp.pad(w_sm, ((0, 0), (0, 0), (0, GATE_PAD - w_sm.shape[-1])))
    w_sm = jnp.concatenate(_split_bf16(w_sm), axis=-1)
    return dict(wgu=ffn_w_gu.astype(BF16), wd=ffn_w_d.astype(BF16), wple=w_ple.astype(BF16),
                wgate=w_ple_gate.astype(BF16), w_big=w_big, w_sm=w_sm, w_out_even=w_out_even.astype(BF16),
                w_in_odd=w_in_odd.astype(BF16), w_out_odd=w_out_odd.astype(BF16))


def _gate_params(a_log, dt_bias, b_i, b_f):
    gp = jnp.zeros((8, GATE_PAD), F32)
    gp = gp.at[0, COL_G:COL_G + H_A].set(dt_bias).at[0, COL_LOGI:COL_LOGI + H_B].set(b_i)
    return gp.at[0, COL_LOGF:COL_LOGF + H_B].set(b_f).at[1, COL_G:COL_G + H_A].set(a_log)


def _run_trunk(x, p, init_even, init_odd, W, norm_g, final_norm, gdn_conv_w, gdn_a_log, gdn_dt_bias,
               gdn_norm_w, mlstm_b_i, mlstm_b_f, mlstm_norm_w, conv_c_w):
    B, T, D = x.shape
    depth = norm_g.shape[0]
    n = B * T
    x = x.reshape(n, D)
    new_even, new_odd = [], []
    gf = final_norm.reshape(1, D)
    for i in range(depth):
        j = i // 2
        g = lambda k: norm_g[i, k].reshape(1, D)
        x = _ffn(x, g(0), W["wgu"], W["wd"], (i, 0))
        if i % 2 == 0:
            conv_st, S0, C0, n0, m0 = init_even[j]
            gp = _gate_params(gdn_a_log[j], gdn_dt_bias[j], mlstm_b_i[j], mlstm_b_f[j])
            y, st = _even_mixer(x.reshape(B, T, D), g(1), conv_st, S0, C0, n0, m0, W["w_big"], W["w_sm"],
                                gdn_conv_w[j], gp, gdn_norm_w[j], mlstm_norm_w[j], W["w_out_even"], j, EVEN_TILE)
            new_even.append(st)
        else:
            y, st = _odd_mixer(x.reshape(B, T, D), g(1), init_odd[j], W["w_in_odd"], conv_c_w[j],
                               W["w_out_odd"], j)
            new_odd.append(st)
        x = y.reshape(n, D)
        x = _ffn(x, g(2), W["wgu"], W["wd"], (i, 1),
                 ple=(p.reshape(depth, n, -1), i, g(3), gf, W["wple"], W["wgate"], i == depth - 1))
    st_even = [jnp.stack([s[c] for s in new_even]) for c in range(5)]
    return (x.reshape(B, T, D), *st_even, jnp.stack(new_odd))


def kernel(x_prompt, x_sample, state_gdn_conv, state_gdn_S, state_mlstm_C, state_mlstm_n, state_mlstm_m,
           state_conv, p_prompt, p_sample, norm_g, final_norm, ffn_w_gu, ffn_w_d, w_ple, w_ple_gate,
           w_in_even, gdn_conv_w, gdn_a_log, gdn_dt_bias, gdn_norm_w, mlstm_b_i, mlstm_b_f, mlstm_norm_w,
           w_out_even, w_in_odd, conv_c_w, w_out_odd):
    Bp = x_prompt.shape[0]
    n_even, n_odd = state_gdn_S.shape[0], state_conv.shape[0]
    D = x_prompt.shape[-1]
    W = _prep_weights(ffn_w_gu, ffn_w_d, w_ple, w_ple_gate, w_in_even, w_out_even, w_in_odd, w_out_odd)
    zero_even = (jnp.zeros((Bp, CONV_A - 1, QKV_A), F32), jnp.zeros((Bp, H_A, DK_A, DV_A), F32),
                 jnp.zeros((Bp, H_B, DK_B, DV_B), F32), jnp.zeros((Bp, H_B, DK_B), F32),
                 jnp.zeros((Bp, H_B), F32))
    init_even_p = [zero_even] * n_even
    init_odd_p = [jnp.zeros((Bp, CONV_C - 1, D), F32)] * n_odd
    init_even_s = [(state_gdn_conv[j], state_gdn_S[j], state_mlstm_C[j], state_mlstm_n[j], state_mlstm_m[j])
                   for j in range(n_even)]
    init_odd_s = [state_conv[j] for j in range(n_odd)]
    rest = (norm_g, final_norm, gdn_conv_w, gdn_a_log, gdn_dt_bias, gdn_norm_w, mlstm_b_i, mlstm_b_f,
            mlstm_norm_w, conv_c_w)
    outs_p = _run_trunk(x_prompt, p_prompt, init_even_p, init_odd_p, W, *rest)
    outs_s = _run_trunk(x_sample, p_sample, init_even_s, init_odd_s, W, *rest)
    return (outs_p[0], outs_s[0]) + tuple(outs_p[1:]) + tuple(outs_s[1:])
```

```python
import functools

import jax
import jax.numpy as jnp
import numpy as np
from jax import lax
from jax.experimental import pallas as pl
from jax.experimental.pallas import tpu as pltpu

F32 = jnp.float32
BF16 = jnp.bfloat16

EPS = 1e-6
H_A, DK_A, DV_A, CONV_A = 4, 128, 128, 4
H_B, DK_B, DV_B = 4, 128, 128
CONV_C = 3
QKV_A = H_A * (2 * DK_A + DV_A)
FF_CHUNK = 256
GATE_PAD = 128
SEQ_CHUNK = 128
N_LEVELS = 7
EVEN_TILE = 256
A_FRONT = 10
GROUP = 1
VMEM_LIMIT = 56 * 1024 * 1024

OFF_Z = QKV_A
OFF_QB = OFF_Z + H_A * DV_A
OFF_KB = OFF_QB + H_B * DK_B
OFF_VB = OFF_KB + H_B * DK_B
OFF_OB = OFF_VB + H_B * DV_B
N_BIG = OFF_OB + H_B * DV_B
COL_BETA, COL_G, COL_LOGI, COL_LOGF = 0, H_A, 2 * H_A, 2 * H_A + H_B


def _rms(x, g):
    ms = jnp.mean(x * x, axis=-1, keepdims=True)
    return x * lax.rsqrt(ms + EPS) * g


def _dot(a, b):
    return jnp.dot(a, b, preferred_element_type=F32)


def _bdot(a, b):
    return jnp.dot(a.astype(BF16), b.astype(BF16), preferred_element_type=F32)


def _const_spec(shape):
    n = len(shape)
    return pl.BlockSpec(shape, lambda *_: (0,) * n, pipeline_mode=pl.Buffered(1))


def _sel_spec(shape, idx):
    n = len(shape)
    return pl.BlockSpec((None,) * len(idx) + tuple(shape), lambda *_: tuple(idx) + (0,) * n,
                        pipeline_mode=pl.Buffered(1))


def _params(sem):
    return pltpu.CompilerParams(dimension_semantics=sem, vmem_limit_bytes=VMEM_LIMIT)


def _row_tile(n, want):
    t = min(n, want)
    assert n % t == 0, (n, t)
    return t


def _swiglu_residual(x, g_ref, wgu_ref, wd_ref, act_ref):
    dff = wd_ref.shape[0]
    xn = _rms(x, g_ref[...]).astype(BF16)
    for c in range(dff // FF_CHUNK):
        lo = c * FF_CHUNK
        a = _dot(xn, wgu_ref[:, lo:lo + FF_CHUNK])
        b = _dot(xn, wgu_ref[:, dff + lo:dff + lo + FF_CHUNK])
        act_ref[:, lo:lo + FF_CHUNK] = (a * jax.nn.sigmoid(a) * b).astype(BF16)
    return x + 0.5 * _dot(act_ref[...], wd_ref[...])


def _ffn_kernel(x_ref, g_ref, wgu_ref, wd_ref, o_ref, act_ref):
    o_ref[...] = _swiglu_residual(x_ref[...], g_ref, wgu_ref, wd_ref, act_ref)


def _ffn_ple_kernel(x_ref, g_ref, wgu_ref, wd_ref, p_ref, gp_ref, gf_ref, wple_ref, wgate_ref, o_ref, act_ref, *,
                    final):
    y = _swiglu_residual(x_ref[...], g_ref, wgu_ref, wd_ref, act_ref)
    gate = jax.nn.sigmoid(_dot(_rms(y, gp_ref[...]).astype(BF16), wgate_ref[...]))
    y = y + _dot(p_ref[...].astype(BF16), wple_ref[...]) * gate
    if final:
        y = _rms(y, gf_ref[...])
    o_ref[...] = y


def _ffn(x, g, wgu_all, wd_all, idx, ple=None):
    n, d = x.shape
    dff = wd_all.shape[-2]
    tm = _row_tile(n, 1024)
    row = lambda w: pl.BlockSpec((tm, w), lambda i: (i, 0))
    in_specs = [row(d), _const_spec((1, d)), _sel_spec((d, 2 * dff), idx), _sel_spec((dff, d), idx)]
    args = [x, g, wgu_all, wd_all]
    body = _ffn_kernel
    if ple is not None:
        p_all, layer, gp, gf, wple_all, wgate_all, final = ple
        dp = p_all.shape[-1]
        in_specs += [pl.BlockSpec((None, tm, dp), lambda i: (layer, i, 0)), _const_spec((1, d)), _const_spec((1, d)),
                     _sel_spec((dp, d), (layer,)), _sel_spec((d, d), (layer,))]
        args += [p_all, gp, gf, wple_all, wgate_all]
        body = functools.partial(_ffn_ple_kernel, final=final)
    return pl.pallas_call(
        body,
        out_shape=jax.ShapeDtypeStruct((n, d), F32),
        grid=(n // tm,),
        in_specs=in_specs,
        out_specs=row(d),
        scratch_shapes=[pltpu.VMEM((tm, dff), BF16)],
        compiler_params=_params(("parallel",)),
        name="ffn" if ple is None else "ffn_ple",
    )(*args)


def _odd_kernel(x_ref, g_ref, st_ref, win_ref, cw_ref, wout_ref, o_ref, newst_ref, hist_ref, y_ref):
    sb, tt, d = x_ref.shape
    x = x_ref[...].reshape(sb * tt, d)
    xn = _rms(x, g_ref[...]).astype(BF16)

    @pl.when(pl.program_id(1) == 0)
    def _():
        hist_ref[:, 6:8, :] = st_ref[...]

    for c0 in range(0, d, FF_CHUNK):
        c1 = c0 + FF_CHUNK
        h = _dot(xn, win_ref[:, c0:c1])
        bg = _dot(xn, win_ref[:, d + c0:d + c1])
        cg = _dot(xn, win_ref[:, 2 * d + c0:2 * d + c1])
        hist_ref[:, 8:8 + tt, c0:c1] = (cg * h).reshape(sb, tt, FF_CHUNK)
        conv = (hist_ref[:, 6:6 + tt, c0:c1] * cw_ref[0:1, c0:c1] + hist_ref[:, 7:7 + tt, c0:c1] * cw_ref[1:2, c0:c1]
                + hist_ref[:, 8:8 + tt, c0:c1] * cw_ref[2:3, c0:c1])
        y_ref[:, c0:c1] = (bg * conv.reshape(sb * tt, FF_CHUNK)).astype(BF16)
    o_ref[...] = (x + _dot(y_ref[...], wout_ref[...])).reshape(sb, tt, d)
    last = hist_ref[:, tt + 6:tt + 8, :]
    hist_ref[:, 6:8, :] = last
    newst_ref[...] = last


def _odd_mixer(x, g, st, win_all, cw, wout_all, j):
    b, t, d = x.shape
    tt = _row_tile(t, 1024)
    sb = b if tt < 128 else 1
    return pl.pallas_call(
        _odd_kernel,
        out_shape=(jax.ShapeDtypeStruct((b, t, d), F32),
                   jax.ShapeDtypeStruct((b, CONV_C - 1, d), F32)),
        grid=(b // sb, t // tt),
        in_specs=[pl.BlockSpec((sb, tt, d), lambda i, k: (i, k, 0)),
                  _const_spec((1, d)),
                  pl.BlockSpec((sb, CONV_C - 1, d), lambda i, k: (i, 0, 0)),
                  _sel_spec((d, 3 * d), (j,)), _const_spec((CONV_C, d)), _sel_spec((d, d), (j,))],
        out_specs=(pl.BlockSpec((sb, tt, d), lambda i, k: (i, k, 0)),
                   pl.BlockSpec((sb, CONV_C - 1, d), lambda i, k: (i, 0, 0))),
        scratch_shapes=[pltpu.VMEM((sb, tt + 8, d), F32), pltpu.VMEM((sb * tt, d), BF16)],
        compiler_params=_params(("parallel", "arbitrary")),
        name="odd_mixer",
    )(x, g, st, win_all, cw, wout_all)


def _split_bf16(a):
    hi = a.astype(BF16)
    lo = (a - hi.astype(F32)).astype(BF16)
    return hi, lo


def _softplus_parts(y):
    t = jnp.log1p(jnp.exp(-jnp.abs(y)))
    return jnp.maximum(y, 0.0) + t, jnp.maximum(-y, 0.0) + t


def _chunk_tri(tm):
    t = np.arange(tm)
    same = (t[:, None] // SEQ_CHUNK) == (t[None, :] // SEQ_CHUNK)
    return (same & (t[:, None] >= t[None, :])).astype(np.float32)


def _level_masks():
    t = np.arange(SEQ_CHUNK)
    out = []
    for lv in range(N_LEVELS):
        b = 1 << lv
        tb, sb = t[:, None] // b, t[None, :] // b
        out.append(((tb % 2 == 1) & (sb == tb - 1)).astype(np.float32))
    return np.stack(out)


def _even_mixer_kernel(*refs, nt, ntot):
    for parity in (0, 1):
        @pl.when(pl.program_id(0) % 2 == parity)
        def _():
            _even_mixer_step(*refs, nt=nt, ntot=ntot, slot_a=parity)


def _even_mixer_step(xa_ref, xb_ref, g_ref, cst_ref, s0_ref, c0_ref, m0_ref, wbig_ref, wsm_ref, convw_ref, gp_ref,
                     tri_ref, gnw_ref, mnw_ref, lvl_ref, wout_ref,
                     o_ref, ncst_ref, s_ref, c_ref, m_ref,
                     big_scr, gt_scr, cum_scr, hist_ref, mix_ref, *, nt, ntot, slot_a):
    L = SEQ_CHUNK
    tm = xa_ref.shape[0]
    nc = big_scr.shape[1] // L
    t_valid = min(tm, L)
    HA, HB = range(H_A), range(H_B)
    s = pl.program_id(0)
    slot_b = 1 - slot_a
    ja = jnp.minimum(s, ntot - 1) % nt
    jb = jnp.maximum(s - 1, 0) % nt
    big_a, gt_a, cum_a = big_scr.at[slot_a], gt_scr.at[slot_a], cum_scr.at[slot_a]
    big_b, gt_b, cum_b = big_scr.at[slot_b], gt_scr.at[slot_b], cum_scr.at[slot_b]

    @pl.when(s == 0)
    def _():
        big_scr[...] = jnp.zeros_like(big_scr)
        gt_scr[...] = jnp.zeros_like(gt_scr)
        cum_scr[...] = jnp.zeros_like(cum_scr)

    @pl.when(ja == 0)
    def _():
        hist_ref[5:8, :] = cst_ref[...]

    @pl.when(jb == 0)
    def _():
        s_ref[...] = s0_ref[...]
        c_ref[...] = c0_ref[...]
        m_ref[...] = m0_ref[...]

    xn = _rms(xa_ref[...], g_ref[...])
    xh, xl = _split_bf16(xn)

    def proj_task(c0, width):
        def run():
            r = _dot(xh, wbig_ref[:, c0:c0 + width])
            if c0 < OFF_Z:
                hist_ref[8:8 + tm, c0:c0 + width] = r
            elif OFF_QB <= c0 < OFF_KB:
                big_a[0:tm, c0:c0 + width] = r * (DK_B ** -0.5)
            else:
                big_a[0:tm, c0:c0 + width] = r
        return run

    def gate_task():
        r = _dot(xh, wsm_ref[...])
        y = r[:, :GATE_PAD] + (_dot(xl, wsm_ref[:, :GATE_PAD]) + r[:, GATE_PAD:]) + gp_ref[0:1, :]
        lane = lax.broadcasted_iota(jnp.int32, y.shape, 1)
        sp_pos, sp_neg = _softplus_parts(y)
        gt = jnp.where(lane < COL_G, jax.nn.sigmoid(y),
                       jnp.where(lane < COL_LOGI, -jnp.exp(gp_ref[1:2, :]) * sp_pos,
                                 jnp.where(lane < COL_LOGF, y, -sp_neg)))
        gt = jnp.where(lane < COL_LOGF + H_B, gt, 0.0)
        gt_a[0:tm, :] = gt
        g1 = gt.astype(BF16)
        r1 = gt - g1.astype(F32)
        g2 = r1.astype(BF16)
        g3 = (r1 - g2.astype(F32)).astype(BF16)
        tri = tri_ref[...]
        cum = _dot(tri, g1) + (_dot(tri, g2) + _dot(tri, g3))
        cum_a[0:tm, :] = cum
        if tm < L:
            cum_a[tm:L, :] = jnp.broadcast_to(cum[tm - 1:tm, :], (L - tm, GATE_PAD))

    def conv_block(blk):
        c0 = blk * 128
        acc = hist_ref[5:5 + tm, c0:c0 + 128] * convw_ref[0:1, c0:c0 + 128]
        for j in range(1, CONV_A):
            acc = acc + hist_ref[5 + j:5 + j + tm, c0:c0 + 128] * convw_ref[j:j + 1, c0:c0 + 128]
        v = acc * jax.nn.sigmoid(acc)
        if blk < 2 * H_A:
            v = v * lax.rsqrt(jnp.sum(v * v, axis=-1, keepdims=True) + EPS)
        if blk < H_A:
            v = v * (DK_A ** -0.5)
        big_a[0:tm, c0:c0 + 128] = v

    a_tasks = [proj_task(c0, 256) for c0 in range(0, N_BIG, 256)]
    a_tasks.insert(OFF_Z // 256, gate_task)

    def a_step(n=1):
        for _ in range(n):
            if a_tasks:
                a_tasks.pop(0)()

    row = lax.broadcasted_iota(jnp.int32, (L, L), 0)
    col = lax.broadcasted_iota(jnp.int32, (L, L), 1)
    incl = row >= col
    col_ok = incl if t_valid == L else (incl & (col < t_valid))
    eye = jnp.where(row == col, 1.0, 0.0)
    one_col = jnp.where(col == 0, 1.0, 0.0)

    def blk(c, off, h):
        return big_b[c * L:(c + 1) * L, off + h * 128:off + (h + 1) * 128]

    def mlstm_local(c):
        cum_c = cum_b[c * L:(c + 1) * L, :]
        cum_t, gt_t = cum_c.T, gt_b[c * L:(c + 1) * L, :].T
        p = dict(q=[blk(c, OFF_QB, h) for h in HB], k_t=[blk(c, OFF_KB, h).T for h in HB],
                 v_ext=[jnp.concatenate([blk(c, OFF_VB, h), one_col], axis=1) for h in HB],
                 b_c=[cum_c[:, COL_LOGF + h:COL_LOGF + h + 1] for h in HB])
        p["dmat"] = [jnp.where(col_ok, p["b_c"][h] - cum_t[COL_LOGF + h:COL_LOGF + h + 1, :]
                               + gt_t[COL_LOGI + h:COL_LOGI + h + 1, :], -jnp.inf) for h in HB]
        p["dmax"] = [jnp.max(p["dmat"][h], axis=-1, keepdims=True) for h in HB]
        p["s"] = [_bdot(p["q"][h], p["k_t"][h]) for h in HB]
        return p

    def mlstm_state(p, m_prev, c_cur):
        m_t = [jnp.maximum(p["b_c"][h] + m_prev[h], p["dmax"][h]) for h in HB]
        w_inter = [jnp.exp(p["b_c"][h] + m_prev[h] - m_t[h]) for h in HB]
        w_intra = [jnp.exp(p["dmat"][h] - m_t[h]) for h in HB]
        av = [_bdot(p["s"][h] * w_intra[h], p["v_ext"][h]) for h in HB]
        qc = [_bdot(p["q"][h], c_cur[h]) for h in HB]
        cu = [_bdot(p["k_t"][h] * w_intra[h][L - 1:L, :], p["v_ext"][h]) for h in HB]
        tot = [w_inter[h] * qc[h] + av[h] for h in HB]
        out = [tot[h][:, :DV_B] / jnp.maximum(jnp.abs(tot[h][:, DV_B:DV_B + 1]), jnp.exp(-m_t[h])) for h in HB]
        c_new = [w_inter[h][L - 1:L, :] * c_cur[h] + cu[h] for h in HB]
        return out, [m_t[h][L - 1:L, :] for h in HB], c_new

    def gdn_local(c):
        gt_c, cum_c = gt_b[c * L:(c + 1) * L, :], cum_b[c * L:(c + 1) * L, :]
        cum_t = cum_c.T
        p = dict(q=[blk(c, 0, h) for h in HA], k=[blk(c, H_A * DK_A, h) for h in HA],
                 v=[blk(c, 2 * H_A * DK_A, h) for h in HA],
                 beta=[gt_c[:, COL_BETA + h:COL_BETA + h + 1] for h in HA],
                 g_c=[cum_c[:, COL_G + h:COL_G + h + 1] for h in HA],
                 g_r=[cum_t[COL_G + h:COL_G + h + 1, :] for h in HA])
        p["gam"] = [jnp.where(incl, jnp.exp(jnp.where(incl, p["g_c"][h] - p["g_r"][h], 0.0)), 0.0) for h in HA]
        p["k_t"] = [p["k"][h].T for h in HA]
        p["qk_kk"] = [_bdot(jnp.concatenate([p["q"][h], p["k"][h]], axis=0), p["k_t"][h]) for h in HA]
        p["a_s"] = [p["beta"][h] * p["qk_kk"][h][L:] * p["gam"][h] for h in HA]
        p["d"] = [eye - p["a_s"][h] * lvl_ref[0] for h in HA]
        return p

    def gdn_solve(p):
        p["e_g"] = [jnp.exp(p["g_c"][h]) for h in HA]
        p["sol"] = [_bdot(p["d"][h], jnp.concatenate([p["beta"][h] * p["v"][h],
                                                      (p["beta"][h] * p["e_g"][h]) * p["k"][h]], axis=1)) for h in HA]

    def gdn_state_a(p, s_cur):
        wq = [_bdot(jnp.concatenate([p["sol"][h][:, DV_A:], p["q"][h]], axis=0), s_cur[h]) for h in HA]
        p["u"] = [p["sol"][h][:, :DV_A] - wq[h][:L] for h in HA]
        p["qs"] = [wq[h][L:] for h in HA]

    def gdn_state_b(p, s_cur):
        out = [p["e_g"][h] * p["qs"][h] + _bdot(p["qk_kk"][h][:L] * p["gam"][h], p["u"][h]) for h in HA]
        g_last = [p["g_r"][h][:, L - 1:L] for h in HA]
        s_new = [jnp.exp(g_last[h]) * s_cur[h] + _bdot(p["k_t"][h] * jnp.exp(g_last[h] - p["g_r"][h]), p["u"][h])
                 for h in HA]
        return out, s_new

    m_prev = [m_ref[h, 0:1, 0:1] for h in HB]
    c_cur = [c_ref[h] for h in HB]
    s_cur = [s_ref[h] for h in HA]
    m_out, g_out = [None] * nc, [None] * nc
    a_step(A_FRONT)
    prev = []
    for g0 in range(0, nc, GROUP):
        cs = list(range(g0, min(g0 + GROUP, nc)))
        ml = [mlstm_local(c) for c in cs]
        gd = [gdn_local(c) for c in cs]
        riders = {1 + i: ("mlstm", ml[i], c) for i, c in enumerate(cs)}
        for i, (c, p) in enumerate(prev):
            riders[3 + 2 * i], riders[4 + 2 * i] = ("gdn_a", p, c), ("gdn_b", p, c)
        for lv in range(1, N_LEVELS):
            de = [[_bdot(p["d"][h], p["a_s"][h] * lvl_ref[lv]) for h in HA] for p in gd]
            kind, arg, c = riders.get(lv, (None, None, None))
            if kind == "mlstm":
                m_out[c], m_prev, c_cur = mlstm_state(arg, m_prev, c_cur)
            elif kind == "gdn_a":
                gdn_state_a(arg, s_cur)
            elif kind == "gdn_b":
                g_out[c], s_cur = gdn_state_b(arg, s_cur)
            a_step()
            for p, de_p in zip(gd, de):
                p["d"] = [p["d"][h] - _bdot(de_p[h], p["d"][h]) for h in HA]
            a_step()
        for p in gd:
            gdn_solve(p)
        prev = list(zip(cs, gd))
    a_step(len(a_tasks))
    for c, p in prev:
        gdn_state_a(p, s_cur)
        g_out[c], s_cur = gdn_state_b(p, s_cur)

    for h in HA:
        s_ref[h] = s_cur[h]
    for h in HB:
        c_ref[h] = c_cur[h]
        m_ref[h] = jnp.broadcast_to(m_prev[h], m_ref.shape[1:])

    for b in range(QKV_A // 128):
        conv_block(b)
    last = hist_ref[tm + 5:tm + 8, :]
    hist_ref[5:8, :] = last
    ncst_ref[...] = last

    for c in range(nc):
        for h in HA:
            o = g_out[c][h]
            o = o * lax.rsqrt(jnp.mean(o * o, axis=-1, keepdims=True) + EPS) * gnw_ref[...]
            z = blk(c, OFF_Z, h)
            mix_ref[c * L:(c + 1) * L, h * DV_A:(h + 1) * DV_A] = (o * (z * jax.nn.sigmoid(z))).astype(BF16)
        for h in HB:
            hh = m_out[c][h]
            hh = hh * lax.rsqrt(jnp.mean(hh * hh, axis=-1, keepdims=True) + EPS)
            hh = hh * mnw_ref[:, h * DV_B:(h + 1) * DV_B] * jax.nn.sigmoid(blk(c, OFF_OB, h))
            c0 = H_A * DV_A + h * DV_B
            mix_ref[c * L:(c + 1) * L, c0:c0 + DV_B] = hh.astype(BF16)
    o_ref[...] = xb_ref[...] + _dot(mix_ref[...], wout_ref[...])[0:tm]


def _even_mixer(x, g, conv_st, S0, C0, n0, m0, wbig_all, wsm_all, conv_w, gp, gdn_norm_w, mlstm_norm_w, wout_all, j,
                tile):
    B, T, D = x.shape
    L = SEQ_CHUNK
    tm = _row_tile(T, tile)
    assert tm % L == 0 or tm == T < L
    nt = T // tm
    ntot = B * nt
    rows = max(tm, L)
    dmix = H_A * DV_A + H_B * DV_B
    c_ext = jnp.concatenate([C0, n0[..., None], jnp.zeros(C0.shape[:-1] + (DV_B - 1,), F32)], axis=-1)
    m_b = jnp.broadcast_to(m0[:, :, None, None], (B, H_B, 8, 128))
    lvl = jnp.asarray(_level_masks())
    tri = jnp.asarray(_chunk_tri(tm), BF16)
    xt = x.reshape(ntot, tm, D)
    a_idx = lambda s: jnp.minimum(s, ntot - 1)
    b_idx = lambda s: jnp.maximum(s - 1, 0)
    seq_a = lambda *shape: pl.BlockSpec((None,) + shape, lambda s: (a_idx(s) // nt,) + (0,) * len(shape))
    seq_b = lambda *shape: pl.BlockSpec((None,) + shape, lambda s: (b_idx(s) // nt,) + (0,) * len(shape))
    out, ncst, S, c_new, m_new = pl.pallas_call(
        functools.partial(_even_mixer_kernel, nt=nt, ntot=ntot),
        out_shape=(jax.ShapeDtypeStruct((ntot, tm, D), F32),
                   jax.ShapeDtypeStruct((B, CONV_A - 1, QKV_A), F32),
                   jax.ShapeDtypeStruct((B, H_A, DK_A, DV_A), F32),
                   jax.ShapeDtypeStruct((B, H_B, DK_B, 2 * DV_B), F32),
                   jax.ShapeDtypeStruct((B, H_B, 8, 128), F32)),
        grid=(ntot + 1,),
        in_specs=[pl.BlockSpec((None, tm, D), lambda s: (a_idx(s), 0, 0)),
                  pl.BlockSpec((None, tm, D), lambda s: (b_idx(s), 0, 0)),
                  _const_spec((1, D)), seq_a(CONV_A - 1, QKV_A),
                  seq_b(H_A, DK_A, DV_A), seq_b(H_B, DK_B, 2 * DV_B), seq_b(H_B, 8, 128),
                  _sel_spec((D, N_BIG), (j,)), _sel_spec((D, 2 * GATE_PAD), (j,)), _const_spec((CONV_A, QKV_A)),
                  _const_spec((8, GATE_PAD)), _const_spec((tm, tm)), _const_spec((1, DV_A)),
                  _const_spec((1, H_B * DV_B)), _const_spec((N_LEVELS, L, L)), _sel_spec((dmix, D), (j,))],
        out_specs=(pl.BlockSpec((None, tm, D), lambda s: (b_idx(s), 0, 0)), seq_a(CONV_A - 1, QKV_A),
                   seq_b(H_A, DK_A, DV_A), seq_b(H_B, DK_B, 2 * DV_B), seq_b(H_B, 8, 128)),
        scratch_shapes=[pltpu.VMEM((2, rows, N_BIG), F32), pltpu.VMEM((2, rows, GATE_PAD), F32),
                        pltpu.VMEM((2, rows, GATE_PAD), F32), pltpu.VMEM((tm + 8, QKV_A), F32),
                        pltpu.VMEM((rows, dmix), BF16)],
        compiler_params=_params(("arbitrary",)),
        name="even_mixer",
    )(xt, xt, g, conv_st, S0, c_ext, m_b, wbig_all, wsm_all, conv_w, gp, tri, gdn_norm_w.reshape(1, DV_A),
      mlstm_norm_w.reshape(1, H_B * DV_B), lvl, wout_all)
    return out.reshape(B, T, D), (ncst, S, c_new[..., :DV_B], c_new[..., DV_B], m_new[:, :, 0, 0])


def _prep_weights(ffn_w_gu, ffn_w_d, w_ple, w_ple_gate, w_in_even, w_out_even, w_in_odd, w_out_odd):
    nbig = QKV_A + H_A * DV_A
    nmid = nbig + 2 * H_A
    nb_end = nmid + 4 * H_B * DK_B
    w_big = jnp.concatenate([w_in_even[:, :, :nbig], w_in_even[:, :, nmid:nb_end]], axis=-1).astype(BF16)
    w_sm = jnp.concatenate([w_in_even[:, :, nbig:nmid], w_in_even[:, :, nb_end:]], axis=-1)
    w_sm = jnp.pad(w_sm, ((0, 0), (0, 0), (0, GATE_PAD - w_sm.shape[-1])))
    w_sm = jnp.concatenate(_split_bf16(w_sm), axis=-1)
    return dict(wgu=ffn_w_gu.astype(BF16), wd=ffn_w_d.astype(BF16), wple=w_ple.astype(BF16),
                wgate=w_ple_gate.astype(BF16), w_big=w_big, w_sm=w_sm, w_out_even=w_out_even.astype(BF16),
                w_in_odd=w_in_odd.astype(BF16), w_out_odd=w_out_odd.astype(BF16))


def _gate_params(a_log, dt_bias, b_i, b_f):
    gp = jnp.zeros((8, GATE_PAD), F32)
    gp = gp.at[0, COL_G:COL_G + H_A].set(dt_bias).at[0, COL_LOGI:COL_LOGI + H_B].set(b_i)
    return gp.at[0, COL_LOGF:COL_LOGF + H_B].set(b_f).at[1, COL_G:COL_G + H_A].set(a_log)


def _run_trunk(x, p, init_even, init_odd, W, norm_g, final_norm, gdn_conv_w, gdn_a_log, gdn_dt_bias,
               gdn_norm_w, mlstm_b_i, mlstm_b_f, mlstm_norm_w, conv_c_w):
    B, T, D = x.shape
    depth = norm_g.shape[0]
    n = B * T
    x = x.reshape(n, D)
    new_even, new_odd = [], []
    gf = final_norm.reshape(1, D)
    for i in range(depth):
        j = i // 2
        g = lambda k: norm_g[i, k].reshape(1, D)
        x = _ffn(x, g(0), W["wgu"], W["wd"], (i, 0))
        if i % 2 == 0:
            conv_st, S0, C0, n0, m0 = init_even[j]
            gp = _gate_params(gdn_a_log[j], gdn_dt_bias[j], mlstm_b_i[j], mlstm_b_f[j])
            y, st = _even_mixer(x.reshape(B, T, D), g(1), conv_st, S0, C0, n0, m0, W["w_big"], W["w_sm"],
                                gdn_conv_w[j], gp, gdn_norm_w[j], mlstm_norm_w[j], W["w_out_even"], j, EVEN_TILE)
            new_even.append(st)
        else:
            y, st = _odd_mixer(x.reshape(B, T, D), g(1), init_odd[j], W["w_in_odd"], conv_c_w[j],
                               W["w_out_odd"], j)
            new_odd.append(st)
        x = y.reshape(n, D)
        x = _ffn(x, g(2), W["wgu"], W["wd"], (i, 1),
                 ple=(p.reshape(depth, n, -1), i, g(3), gf, W["wple"], W["wgate"], i == depth - 1))
    st_even = [jnp.stack([s[c] for s in new_even]) for c in range(5)]
    return (x.reshape(B, T, D), *st_even, jnp.stack(new_odd))


def kernel(x_prompt, x_sample, state_gdn_conv, state_gdn_S, state_mlstm_C, state_mlstm_n, state_mlstm_m,
           state_conv, p_prompt, p_sample, norm_g, final_norm, ffn_w_gu, ffn_w_d, w_ple, w_ple_gate,
           w_in_even, gdn_conv_w, gdn_a_log, gdn_dt_bias, gdn_norm_w, mlstm_b_i, mlstm_b_f, mlstm_norm_w,
           w_out_even, w_in_odd, conv_c_w, w_out_odd):
    Bp = x_prompt.shape[0]
    n_even, n_odd = state_gdn_S.shape[0], state_conv.shape[0]
    D = x_prompt.shape[-1]
    W = _prep_weights(ffn_w_gu, ffn_w_d, w_ple, w_ple_gate, w_in_even, w_out_even, w_in_odd, w_out_odd)
    zero_even = (jnp.zeros((Bp, CONV_A - 1, QKV_A), F32), jnp.zeros((Bp, H_A, DK_A, DV_A), F32),
                 jnp.zeros((Bp, H_B, DK_B, DV_B), F32), jnp.zeros((Bp, H_B, DK_B), F32),
                 jnp.zeros((Bp, H_B), F32))
    init_even_p = [zero_even] * n_even
    init_odd_p = [jnp.zeros((Bp, CONV_C - 1, D), F32)] * n_odd
    init_even_s = [(state_gdn_conv[j], state_gdn_S[j], state_mlstm_C[j], state_mlstm_n[j], state_mlstm_m[j])
                   for j in range(n_even)]
    init_odd_s = [state_conv[j] for j in range(n_odd)]
    rest = (norm_g, final_norm, gdn_conv_w, gdn_a_log, gdn_dt_bias, gdn_norm_w, mlstm_b_i, mlstm_b_f,
            mlstm_norm_w, conv_c_w)
    outs_p = _run_trunk(x_prompt, p_prompt, init_even_p, init_odd_p, W, *rest)
    outs_s = _run_trunk(x_sample, p_sample, init_even_s, init_odd_s, W, *rest)
    return (outs_p[0], outs_s[0]) + tuple(outs_p[1:]) + tuple(outs_s[1:])
```

```python
import functools

import jax
import jax.numpy as jnp
import numpy as np
from jax import lax
from jax.experimental import pallas as pl
from jax.experimental.pallas import tpu as pltpu

F32 = jnp.float32
BF16 = jnp.bfloat16

EPS = 1e-6
H_A, DK_A, DV_A, CONV_A = 4, 128, 128, 4
H_B, DK_B, DV_B = 4, 128, 128
CONV_C = 3
QKV_A = H_A * (2 * DK_A + DV_A)
FF_CHUNK = 256
GATE_PAD = 128
SEQ_CHUNK = 128
N_LEVELS = 7
EVEN_TILE = 256
A_FRONT = 10
VMEM_LIMIT = 56 * 1024 * 1024

OFF_Z = QKV_A
OFF_QB = OFF_Z + H_A * DV_A
OFF_KB = OFF_QB + H_B * DK_B
OFF_VB = OFF_KB + H_B * DK_B
OFF_OB = OFF_VB + H_B * DV_B
N_BIG = OFF_OB + H_B * DV_B
COL_BETA, COL_G, COL_LOGI, COL_LOGF = 0, H_A, 2 * H_A, 2 * H_A + H_B


def _rms(x, g):
    ms = jnp.mean(x * x, axis=-1, keepdims=True)
    return x * lax.rsqrt(ms + EPS) * g


def _dot(a, b):
    return jnp.dot(a, b, preferred_element_type=F32)


def _bdot(a, b):
    return jnp.dot(a.astype(BF16), b.astype(BF16), preferred_element_type=F32)


def _const_spec(shape):
    n = len(shape)
    return pl.BlockSpec(shape, lambda *_: (0,) * n, pipeline_mode=pl.Buffered(1))


def _sel_spec(shape, idx):
    n = len(shape)
    return pl.BlockSpec((None,) * len(idx) + tuple(shape), lambda *_: tuple(idx) + (0,) * n,
                        pipeline_mode=pl.Buffered(1))


def _params(sem):
    return pltpu.CompilerParams(dimension_semantics=sem, vmem_limit_bytes=VMEM_LIMIT)


def _row_tile(n, want):
    t = min(n, want)
    assert n % t == 0, (n, t)
    return t


def _swiglu_residual(x, g_ref, wgu_ref, wd_ref, act_ref):
    dff = wd_ref.shape[0]
    xn = _rms(x, g_ref[...]).astype(BF16)
    for c in range(dff // FF_CHUNK):
        lo = c * FF_CHUNK
        a = _dot(xn, wgu_ref[:, lo:lo + FF_CHUNK])
        b = _dot(xn, wgu_ref[:, dff + lo:dff + lo + FF_CHUNK])
        act_ref[:, lo:lo + FF_CHUNK] = (a * jax.nn.sigmoid(a) * b).astype(BF16)
        if c % 2 == 1:
            x = x + 0.5 * _dot(act_ref[:, lo - FF_CHUNK:lo + FF_CHUNK], wd_ref[lo - FF_CHUNK:lo + FF_CHUNK, :])
    return x + 0.5 * _dot(act_ref[:, lo:lo + FF_CHUNK], wd_ref[lo:lo + FF_CHUNK, :])


def _ffn_kernel(x_ref, g_ref, wgu_ref, wd_ref, o_ref, act_ref):
    o_ref[...] = _swiglu_residual(x_ref[...], g_ref, wgu_ref, wd_ref, act_ref)


def _ffn_ple_kernel(x_ref, g_ref, wgu_ref, wd_ref, p_ref, gp_ref, gf_ref, wple_ref, wgate_ref, o_ref, act_ref, *,
                    final):
    y = _swiglu_residual(x_ref[...], g_ref, wgu_ref, wd_ref, act_ref)
    gate = jax.nn.sigmoid(_dot(_rms(y, gp_ref[...]).astype(BF16), wgate_ref[...]))
    y = y + _dot(p_ref[...].astype(BF16), wple_ref[...]) * gate
    if final:
        y = _rms(y, gf_ref[...])
    o_ref[...] = y


def _ffn(x, g, wgu_all, wd_all, idx, ple=None):
    n, d = x.shape
    dff = wd_all.shape[-2]
    tm = _row_tile(n, 1024)
    row = lambda w: pl.BlockSpec((tm, w), lambda i: (i, 0))
    in_specs = [row(d), _const_spec((1, d)), _sel_spec((d, 2 * dff), idx), _sel_spec((dff, d), idx)]
    args = [x, g, wgu_all, wd_all]
    body = _ffn_kernel
    if ple is not None:
        p_all, layer, gp, gf, wple_all, wgate_all, final = ple
        dp = p_all.shape[-1]
        in_specs += [pl.BlockSpec((None, tm, dp), lambda i: (layer, i, 0)), _const_spec((1, d)), _const_spec((1, d)),
                     _sel_spec((dp, d), (layer,)), _sel_spec((d, d), (layer,))]
        args += [p_all, gp, gf, wple_all, wgate_all]
        body = functools.partial(_ffn_ple_kernel, final=final)
    return pl.pallas_call(
        body,
        out_shape=jax.ShapeDtypeStruct((n, d), F32),
        grid=(n // tm,),
        in_specs=in_specs,
        out_specs=row(d),
        scratch_shapes=[pltpu.VMEM((tm, dff), BF16)],
        compiler_params=_params(("parallel",)),
        name="ffn" if ple is None else "ffn_ple",
    )(*args)


def _odd_kernel(x_ref, g_ref, st_ref, win_ref, cw_ref, wout_ref, o_ref, newst_ref, hist_ref, y_ref):
    sb, tt, d = x_ref.shape
    x = x_ref[...].reshape(sb * tt, d)
    xn = _rms(x, g_ref[...]).astype(BF16)

    @pl.when(pl.program_id(1) == 0)
    def _():
        hist_ref[:, 6:8, :] = st_ref[...]

    for c0 in range(0, d, FF_CHUNK):
        c1 = c0 + FF_CHUNK
        h = _dot(xn, win_ref[:, c0:c1])
        bg = _dot(xn, win_ref[:, d + c0:d + c1])
        cg = _dot(xn, win_ref[:, 2 * d + c0:2 * d + c1])
        hist_ref[:, 8:8 + tt, c0:c1] = (cg * h).reshape(sb, tt, FF_CHUNK)
        conv = (hist_ref[:, 6:6 + tt, c0:c1] * cw_ref[0:1, c0:c1] + hist_ref[:, 7:7 + tt, c0:c1] * cw_ref[1:2, c0:c1]
                + hist_ref[:, 8:8 + tt, c0:c1] * cw_ref[2:3, c0:c1])
        y_ref[:, c0:c1] = (bg * conv.reshape(sb * tt, FF_CHUNK)).astype(BF16)
    o_ref[...] = (x + _dot(y_ref[...], wout_ref[...])).reshape(sb, tt, d)
    last = hist_ref[:, tt + 6:tt + 8, :]
    hist_ref[:, 6:8, :] = last
    newst_ref[...] = last


def _odd_mixer(x, g, st, win_all, cw, wout_all, j):
    b, t, d = x.shape
    tt = _row_tile(t, 1024)
    sb = b if tt < 128 else 1
    return pl.pallas_call(
        _odd_kernel,
        out_shape=(jax.ShapeDtypeStruct((b, t, d), F32),
                   jax.ShapeDtypeStruct((b, CONV_C - 1, d), F32)),
        grid=(b // sb, t // tt),
        in_specs=[pl.BlockSpec((sb, tt, d), lambda i, k: (i, k, 0)),
                  _const_spec((1, d)),
                  pl.BlockSpec((sb, CONV_C - 1, d), lambda i, k: (i, 0, 0)),
                  _sel_spec((d, 3 * d), (j,)), _const_spec((CONV_C, d)), _sel_spec((d, d), (j,))],
        out_specs=(pl.BlockSpec((sb, tt, d), lambda i, k: (i, k, 0)),
                   pl.BlockSpec((sb, CONV_C - 1, d), lambda i, k: (i, 0, 0))),
        scratch_shapes=[pltpu.VMEM((sb, tt + 8, d), F32), pltpu.VMEM((sb * tt, d), BF16)],
        compiler_params=_params(("parallel", "arbitrary")),
        name="odd_mixer",
    )(x, g, st, win_all, cw, wout_all)


def _split_bf16(a):
    hi = a.astype(BF16)
    lo = (a - hi.astype(F32)).astype(BF16)
    return hi, lo


def _softplus_parts(y):
    t = jnp.log1p(jnp.exp(-jnp.abs(y)))
    return jnp.maximum(y, 0.0) + t, jnp.maximum(-y, 0.0) + t


def _chunk_tri(tm):
    t = np.arange(tm)
    same = (t[:, None] // SEQ_CHUNK) == (t[None, :] // SEQ_CHUNK)
    return (same & (t[:, None] >= t[None, :])).astype(np.float32)


def _level_masks():
    t = np.arange(SEQ_CHUNK)
    out = []
    for lv in range(N_LEVELS):
        b = 1 << lv
        tb, sb = t[:, None] // b, t[None, :] // b
        out.append(((tb % 2 == 1) & (sb == tb - 1)).astype(np.float32))
    return np.stack(out)


def _even_mixer_kernel(*refs, nt, ntot):
    for parity in (0, 1):
        @pl.when(pl.program_id(0) % 2 == parity)
        def _():
            _even_mixer_step(*refs, nt=nt, ntot=ntot, slot_a=parity)


def _even_mixer_step(xa_ref, xb_ref, g_ref, cst_ref, s0_ref, c0_ref, m0_ref, wbig_ref, wsm_ref, convw_ref, gp_ref,
                     tri_ref, gnw_ref, mnw_ref, lvl_ref, wout_ref,
                     o_ref, ncst_ref, s_ref, c_ref, m_ref,
                     big_scr, gt_scr, cum_scr, hist_ref, mix_ref, *, nt, ntot, slot_a):
    L = SEQ_CHUNK
    tm = xa_ref.shape[0]
    nc = big_scr.shape[1] // L
    t_valid = min(tm, L)
    HA, HB = range(H_A), range(H_B)
    s = pl.program_id(0)
    slot_b = 1 - slot_a
    ja = jnp.minimum(s, ntot - 1) % nt
    jb = jnp.maximum(s - 1, 0) % nt
    big_a, gt_a, cum_a = big_scr.at[slot_a], gt_scr.at[slot_a], cum_scr.at[slot_a]
    big_b, gt_b, cum_b = big_scr.at[slot_b], gt_scr.at[slot_b], cum_scr.at[slot_b]

    @pl.when(s == 0)
    def _():
        big_scr[...] = jnp.zeros_like(big_scr)
        gt_scr[...] = jnp.zeros_like(gt_scr)
        cum_scr[...] = jnp.zeros_like(cum_scr)

    @pl.when(ja == 0)
    def _():
        hist_ref[5:8, :] = cst_ref[...]

    @pl.when(jb == 0)
    def _():
        s_ref[...] = s0_ref[...]
        c_ref[...] = c0_ref[...]
        m_ref[...] = m0_ref[...]

    xn = _rms(xa_ref[...], g_ref[...])
    xh, xl = _split_bf16(xn)

    def proj_task(c0, width):
        def run():
            r = _dot(xh, wbig_ref[:, c0:c0 + width])
            if c0 < OFF_Z:
                hist_ref[8:8 + tm, c0:c0 + width] = r
            elif OFF_QB <= c0 < OFF_KB:
                big_a[0:tm, c0:c0 + width] = r * (DK_B ** -0.5)
            else:
                big_a[0:tm, c0:c0 + width] = r
        return run

    def gate_task():
        r = _dot(xh, wsm_ref[...])
        y = r[:, :GATE_PAD] + (_dot(xl, wsm_ref[:, :GATE_PAD]) + r[:, GATE_PAD:]) + gp_ref[0:1, :]
        lane = lax.broadcasted_iota(jnp.int32, y.shape, 1)
        sp_pos, sp_neg = _softplus_parts(y)
        gt = jnp.where(lane < COL_G, jax.nn.sigmoid(y),
                       jnp.where(lane < COL_LOGI, -jnp.exp(gp_ref[1:2, :]) * sp_pos,
                                 jnp.where(lane < COL_LOGF, y, -sp_neg)))
        gt = jnp.where(lane < COL_LOGF + H_B, gt, 0.0)
        gt_a[0:tm, :] = gt
        g1 = gt.astype(BF16)
        r1 = gt - g1.astype(F32)
        g2 = r1.astype(BF16)
        g3 = (r1 - g2.astype(F32)).astype(BF16)
        tri = tri_ref[...]
        cum = _dot(tri, g1) + (_dot(tri, g2) + _dot(tri, g3))
        cum_a[0:tm, :] = cum
        if tm < L:
            cum_a[tm:L, :] = jnp.broadcast_to(cum[tm - 1:tm, :], (L - tm, GATE_PAD))

    def conv_block(blk):
        c0 = blk * 128
        acc = hist_ref[5:5 + tm, c0:c0 + 128] * convw_ref[0:1, c0:c0 + 128]
        for j in range(1, CONV_A):
            acc = acc + hist_ref[5 + j:5 + j + tm, c0:c0 + 128] * convw_ref[j:j + 1, c0:c0 + 128]
        v = acc * jax.nn.sigmoid(acc)
        if blk < 2 * H_A:
            v = v * lax.rsqrt(jnp.sum(v * v, axis=-1, keepdims=True) + EPS)
        if blk < H_A:
            v = v * (DK_A ** -0.5)
        big_a[0:tm, c0:c0 + 128] = v

    a_tasks = [proj_task(c0, 256) for c0 in range(0, N_BIG, 256)]
    a_tasks.insert(OFF_Z // 256, gate_task)

    def a_step(n=1):
        for _ in range(n):
            if a_tasks:
                a_tasks.pop(0)()

    row = lax.broadcasted_iota(jnp.int32, (L, L), 0)
    col = lax.broadcasted_iota(jnp.int32, (L, L), 1)
    incl = row >= col
    col_ok = incl if t_valid == L else (incl & (col < t_valid))
    eye = jnp.where(row == col, 1.0, 0.0)
    one_col = jnp.where(col == 0, 1.0, 0.0)

    def blk(c, off, h):
        return big_b[c * L:(c + 1) * L, off + h * 128:off + (h + 1) * 128]

    def mlstm_local(c):
        cum_c = cum_b[c * L:(c + 1) * L, :]
        cum_t, gt_t = cum_c.T, gt_b[c * L:(c + 1) * L, :].T
        p = dict(q=[blk(c, OFF_QB, h) for h in HB], k_t=[blk(c, OFF_KB, h).T for h in HB],
                 v_ext=[jnp.concatenate([blk(c, OFF_VB, h), one_col], axis=1) for h in HB],
                 b_c=[cum_c[:, COL_LOGF + h:COL_LOGF + h + 1] for h in HB])
        p["dmat"] = [jnp.where(col_ok, p["b_c"][h] - cum_t[COL_LOGF + h:COL_LOGF + h + 1, :]
                               + gt_t[COL_LOGI + h:COL_LOGI + h + 1, :], -jnp.inf) for h in HB]
        p["dmax"] = [jnp.max(p["dmat"][h], axis=-1, keepdims=True) for h in HB]
        p["s"] = [_bdot(p["q"][h], p["k_t"][h]) for h in HB]
        return p

    def mlstm_state(p, m_prev, c_cur):
        m_t = [jnp.maximum(p["b_c"][h] + m_prev[h], p["dmax"][h]) for h in HB]
        w_inter = [jnp.exp(p["b_c"][h] + m_prev[h] - m_t[h]) for h in HB]
        w_intra = [jnp.exp(p["dmat"][h] - m_t[h]) for h in HB]
        av = [_bdot(p["s"][h] * w_intra[h], p["v_ext"][h]) for h in HB]
        qc = [_bdot(p["q"][h], c_cur[h]) for h in HB]
        cu = [_bdot(p["k_t"][h] * w_intra[h][L - 1:L, :], p["v_ext"][h]) for h in HB]
        tot = [w_inter[h] * qc[h] + av[h] for h in HB]
        out = [tot[h][:, :DV_B] / jnp.maximum(jnp.abs(tot[h][:, DV_B:DV_B + 1]), jnp.exp(-m_t[h])) for h in HB]
        c_new = [w_inter[h][L - 1:L, :] * c_cur[h] + cu[h] for h in HB]
        return out, [m_t[h][L - 1:L, :] for h in HB], c_new

    def gdn_local(c):
        gt_c, cum_c = gt_b[c * L:(c + 1) * L, :], cum_b[c * L:(c + 1) * L, :]
        cum_t = cum_c.T
        p = dict(q=[blk(c, 0, h) for h in HA], k=[blk(c, H_A * DK_A, h) for h in HA],
                 v=[blk(c, 2 * H_A * DK_A, h) for h in HA],
                 beta=[gt_c[:, COL_BETA + h:COL_BETA + h + 1] for h in HA],
                 g_c=[cum_c[:, COL_G + h:COL_G + h + 1] for h in HA],
                 g_r=[cum_t[COL_G + h:COL_G + h + 1, :] for h in HA])
        p["gam"] = [jnp.where(incl, jnp.exp(jnp.where(incl, p["g_c"][h] - p["g_r"][h], 0.0)), 0.0) for h in HA]
        p["k_t"] = [p["k"][h].T for h in HA]
        p["qk_kk"] = [_bdot(jnp.concatenate([p["q"][h], p["k"][h]], axis=0), p["k_t"][h]) for h in HA]
        p["a_s"] = [p["beta"][h] * p["qk_kk"][h][L:] * p["gam"][h] for h in HA]
        p["d"] = [eye - p["a_s"][h] * lvl_ref[0] for h in HA]
        return p

    def gdn_solve(p):
        p["e_g"] = [jnp.exp(p["g_c"][h]) for h in HA]
        p["sol"] = [_bdot(p["d"][h], jnp.concatenate([p["beta"][h] * p["v"][h],
                                                      (p["beta"][h] * p["e_g"][h]) * p["k"][h]], axis=1)) for h in HA]

    def gdn_state_a(p, s_cur):
        wq = [_bdot(jnp.concatenate([p["sol"][h][:, DV_A:], p["q"][h]], axis=0), s_cur[h]) for h in HA]
        p["u"] = [p["sol"][h][:, :DV_A] - wq[h][:L] for h in HA]
        p["qs"] = [wq[h][L:] for h in HA]

    def gdn_state_b(p, s_cur):
        out = [p["e_g"][h] * p["qs"][h] + _bdot(p["qk_kk"][h][:L] * p["gam"][h], p["u"][h]) for h in HA]
        g_last = [p["g_r"][h][:, L - 1:L] for h in HA]
        s_new = [jnp.exp(g_last[h]) * s_cur[h] + _bdot(p["k_t"][h] * jnp.exp(g_last[h] - p["g_r"][h]), p["u"][h])
                 for h in HA]
        return out, s_new

    m_prev = [m_ref[h, 0:1, 0:1] for h in HB]
    c_cur = [c_ref[h] for h in HB]
    s_cur = [s_ref[h] for h in HA]
    m_out, g_out = [None] * nc, [None] * nc
    a_step(A_FRONT)
    prev = []
    for g0 in range(0, nc, 2):
        cs = list(range(g0, min(g0 + 2, nc)))
        ml = [mlstm_local(c) for c in cs]
        gd = [gdn_local(c) for c in cs]
        riders = {1 + i: ("mlstm", ml[i], c) for i, c in enumerate(cs)}
        for i, (c, p) in enumerate(prev):
            riders[3 + 2 * i], riders[4 + 2 * i] = ("gdn_a", p, c), ("gdn_b", p, c)
        for lv in range(1, N_LEVELS):
            de = [[_bdot(p["d"][h], p["a_s"][h] * lvl_ref[lv]) for h in HA] for p in gd]
            kind, arg, c = riders.get(lv, (None, None, None))
            if kind == "mlstm":
                m_out[c], m_prev, c_cur = mlstm_state(arg, m_prev, c_cur)
            elif kind == "gdn_a":
                gdn_state_a(arg, s_cur)
            elif kind == "gdn_b":
                g_out[c], s_cur = gdn_state_b(arg, s_cur)
            a_step()
            for p, de_p in zip(gd, de):
                p["d"] = [p["d"][h] - _bdot(de_p[h], p["d"][h]) for h in HA]
            a_step()
        for p in gd:
            gdn_solve(p)
        prev = list(zip(cs, gd))
    a_step(len(a_tasks))
    for c, p in prev:
        gdn_state_a(p, s_cur)
        g_out[c], s_cur = gdn_state_b(p, s_cur)

    for h in HA:
        s_ref[h] = s_cur[h]
    for h in HB:
        c_ref[h] = c_cur[h]
        m_ref[h] = jnp.broadcast_to(m_prev[h], m_ref.shape[1:])

    for b in range(QKV_A // 128):
        conv_block(b)
    last = hist_ref[tm + 5:tm + 8, :]
    hist_ref[5:8, :] = last
    ncst_ref[...] = last

    for c in range(nc):
        for h in HA:
            o = g_out[c][h]
            o = o * lax.rsqrt(jnp.mean(o * o, axis=-1, keepdims=True) + EPS) * gnw_ref[...]
            z = blk(c, OFF_Z, h)
            mix_ref[c * L:(c + 1) * L, h * DV_A:(h + 1) * DV_A] = (o * (z * jax.nn.sigmoid(z))).astype(BF16)
        for h in HB:
            hh = m_out[c][h]
            hh = hh * lax.rsqrt(jnp.mean(hh * hh, axis=-1, keepdims=True) + EPS)
            hh = hh * mnw_ref[:, h * DV_B:(h + 1) * DV_B] * jax.nn.sigmoid(blk(c, OFF_OB, h))
            c0 = H_A * DV_A + h * DV_B
            mix_ref[c * L:(c + 1) * L, c0:c0 + DV_B] = hh.astype(BF16)
    o_ref[...] = xb_ref[...] + _dot(mix_ref[...], wout_ref[...])[0:tm]


def _even_mixer(x, g, conv_st, S0, C0, n0, m0, wbig_all, wsm_all, conv_w, gp, gdn_norm_w, mlstm_norm_w, wout_all, j,
                tile):
    B, T, D = x.shape
    L = SEQ_CHUNK
    tm = _row_tile(T, tile)
    assert tm % L == 0 or tm == T < L
    nt = T // tm
    ntot = B * nt
    rows = max(tm, L)
    dmix = H_A * DV_A + H_B * DV_B
    c_ext = jnp.concatenate([C0, n0[..., None], jnp.zeros(C0.shape[:-1] + (DV_B - 1,), F32)], axis=-1)
    m_b = jnp.broadcast_to(m0[:, :, None, None], (B, H_B, 8, 128))
    lvl = jnp.asarray(_level_masks())
    tri = jnp.asarray(_chunk_tri(tm), BF16)
    xt = x.reshape(ntot, tm, D)
    a_idx = lambda s: jnp.minimum(s, ntot - 1)
    b_idx = lambda s: jnp.maximum(s - 1, 0)
    seq_a = lambda *shape: pl.BlockSpec((None,) + shape, lambda s: (a_idx(s) // nt,) + (0,) * len(shape))
    seq_b = lambda *shape: pl.BlockSpec((None,) + shape, lambda s: (b_idx(s) // nt,) + (0,) * len(shape))
    out, ncst, S, c_new, m_new = pl.pallas_call(
        functools.partial(_even_mixer_kernel, nt=nt, ntot=ntot),
        out_shape=(jax.ShapeDtypeStruct((ntot, tm, D), F32),
                   jax.ShapeDtypeStruct((B, CONV_A - 1, QKV_A), F32),
                   jax.ShapeDtypeStruct((B, H_A, DK_A, DV_A), F32),
                   jax.ShapeDtypeStruct((B, H_B, DK_B, 2 * DV_B), F32),
                   jax.ShapeDtypeStruct((B, H_B, 8, 128), F32)),
        grid=(ntot + 1,),
        in_specs=[pl.BlockSpec((None, tm, D), lambda s: (a_idx(s), 0, 0)),
                  pl.BlockSpec((None, tm, D), lambda s: (b_idx(s), 0, 0)),
                  _const_spec((1, D)), seq_a(CONV_A - 1, QKV_A),
                  seq_b(H_A, DK_A, DV_A), seq_b(H_B, DK_B, 2 * DV_B), seq_b(H_B, 8, 128),
                  _sel_spec((D, N_BIG), (j,)), _sel_spec((D, 2 * GATE_PAD), (j,)), _const_spec((CONV_A, QKV_A)),
                  _const_spec((8, GATE_PAD)), _const_spec((tm, tm)), _const_spec((1, DV_A)),
                  _const_spec((1, H_B * DV_B)), _const_spec((N_LEVELS, L, L)), _sel_spec((dmix, D), (j,))],
        out_specs=(pl.BlockSpec((None, tm, D), lambda s: (b_idx(s), 0, 0)), seq_a(CONV_A - 1, QKV_A),
                   seq_b(H_A, DK_A, DV_A), seq_b(H_B, DK_B, 2 * DV_B), seq_b(H_B, 8, 128)),
        scratch_shapes=[pltpu.VMEM((2, rows, N_BIG), F32), pltpu.VMEM((2, rows, GATE_PAD), F32),
                        pltpu.VMEM((2, rows, GATE_PAD), F32), pltpu.VMEM((tm + 8, QKV_A), F32),
                        pltpu.VMEM((rows, dmix), BF16)],
        compiler_params=_params(("arbitrary",)),
        name="even_mixer",
    )(xt, xt, g, conv_st, S0, c_ext, m_b, wbig_all, wsm_all, conv_w, gp, tri, gdn_norm_w.reshape(1, DV_A),
      mlstm_norm_w.reshape(1, H_B * DV_B), lvl, wout_all)
    return out.reshape(B, T, D), (ncst, S, c_new[..., :DV_B], c_new[..., DV_B], m_new[:, :, 0, 0])


def _prep_weights(ffn_w_gu, ffn_w_d, w_ple, w_ple_gate, w_in_even, w_out_even, w_in_odd, w_out_odd):
    nbig = QKV_A + H_A * DV_A
    nmid = nbig + 2 * H_A
    nb_end = nmid + 4 * H_B * DK_B
    w_big = jnp.concatenate([w_in_even[:, :, :nbig], w_in_even[:, :, nmid:nb_end]], axis=-1).astype(BF16)
    w_sm = jnp.concatenate([w_in_even[:, :, nbig:nmid], w_in_even[:, :, nb_end:]], axis=-1)
    w_sm = jnp.pad(w_sm, ((0, 0), (0, 0), (0, GATE_PAD - w_sm.shape[-1])))
    w_sm = jnp.concatenate(_split_bf16(w_sm), axis=-1)
    return dict(wgu=ffn_w_gu.astype(BF16), wd=ffn_w_d.astype(BF16), wple=w_ple.astype(BF16),
                wgate=w_ple_gate.astype(BF16), w_big=w_big, w_sm=w_sm, w_out_even=w_out_even.astype(BF16),
                w_in_odd=w_in_odd.astype(BF16), w_out_odd=w_out_odd.astype(BF16))


def _gate_params(a_log, dt_bias, b_i, b_f):
    gp = jnp.zeros((8, GATE_PAD), F32)
    gp = gp.at[0, COL_G:COL_G + H_A].set(dt_bias).at[0, COL_LOGI:COL_LOGI + H_B].set(b_i)
    return gp.at[0, COL_LOGF:COL_LOGF + H_B].set(b_f).at[1, COL_G:COL_G + H_A].set(a_log)


def _run_trunk(x, p, init_even, init_odd, W, norm_g, final_norm, gdn_conv_w, gdn_a_log, gdn_dt_bias,
               gdn_norm_w, mlstm_b_i, mlstm_b_f, mlstm_norm_w, conv_c_w):
    B, T, D = x.shape
    depth = norm_g.shape[0]
    n = B * T
    x = x.reshape(n, D)
    new_even, new_odd = [], []
    gf = final_norm.reshape(1, D)
    for i in range(depth):
        j = i // 2
        g = lambda k: norm_g[i, k].reshape(1, D)
        x = _ffn(x, g(0), W["wgu"], W["wd"], (i, 0))
        if i % 2 == 0:
            conv_st, S0, C0, n0, m0 = init_even[j]
            gp = _gate_params(gdn_a_log[j], gdn_dt_bias[j], mlstm_b_i[j], mlstm_b_f[j])
            y, st = _even_mixer(x.reshape(B, T, D), g(1), conv_st, S0, C0, n0, m0, W["w_big"], W["w_sm"],
                                gdn_conv_w[j], gp, gdn_norm_w[j], mlstm_norm_w[j], W["w_out_even"], j, EVEN_TILE)
            new_even.append(st)
        else:
            y, st = _odd_mixer(x.reshape(B, T, D), g(1), init_odd[j], W["w_in_odd"], conv_c_w[j],
                               W["w_out_odd"], j)
            new_odd.append(st)
        x = y.reshape(n, D)
        x = _ffn(x, g(2), W["wgu"], W["wd"], (i, 1),
                 ple=(p.reshape(depth, n, -1), i, g(3), gf, W["wple"], W["wgate"], i == depth - 1))
    st_even = [jnp.stack([s[c] for s in new_even]) for c in range(5)]
    return (x.reshape(B, T, D), *st_even, jnp.stack(new_odd))


def kernel(x_prompt, x_sample, state_gdn_conv, state_gdn_S, state_mlstm_C, state_mlstm_n, state_mlstm_m,
           state_conv, p_prompt, p_sample, norm_g, final_norm, ffn_w_gu, ffn_w_d, w_ple, w_ple_gate,
           w_in_even, gdn_conv_w, gdn_a_log, gdn_dt_bias, gdn_norm_w, mlstm_b_i, mlstm_b_f, mlstm_norm_w,
           w_out_even, w_in_odd, conv_c_w, w_out_odd):
    Bp = x_prompt.shape[0]
    n_even, n_odd = state_gdn_S.shape[0], state_conv.shape[0]
    D = x_prompt.shape[-1]
    W = _prep_weights(ffn_w_gu, ffn_w_d, w_ple, w_ple_gate, w_in_even, w_out_even, w_in_odd, w_out_odd)
    zero_even = (jnp.zeros((Bp, CONV_A - 1, QKV_A), F32), jnp.zeros((Bp, H_A, DK_A, DV_A), F32),
                 jnp.zeros((Bp, H_B, DK_B, DV_B), F32), jnp.zeros((Bp, H_B, DK_B), F32),
                 jnp.zeros((Bp, H_B), F32))
    init_even_p = [zero_even] * n_even
    init_odd_p = [jnp.zeros((Bp, CONV_C - 1, D), F32)] * n_odd
    init_even_s = [(state_gdn_conv[j], state_gdn_S[j], state_mlstm_C[j], state_mlstm_n[j], state_mlstm_m[j])
                   for j in range(n_even)]
    init_odd_s = [state_conv[j] for j in range(n_odd)]
    rest = (norm_g, final_norm, gdn_conv_w, gdn_a_log, gdn_dt_bias, gdn_norm_w, mlstm_b_i, mlstm_b_f,
            mlstm_norm_w, conv_c_w)
    outs_p = _run_trunk(x_prompt, p_prompt, init_even_p, init_odd_p, W, *rest)
    outs_s = _run_trunk(x_sample, p_sample, init_even_s, init_odd_s, W, *rest)
    return (outs_p[0], outs_s[0]) + tuple(outs_p[1:]) + tuple(outs_s[1:])
```
